```python
import math
import jax, jax.numpy as jnp
from jax import lax
import numpy as np

D_MODEL = 4096
BATCH = 2
SEQ = 4096
DEPTH = 2

CTX_LEN = 256
GRID_W = 64
RMS_EPS = 1e-6
ROPE_BASE = 10000.0
Q_BLOCK = 128

N_BRANCHES = 4
BRANCH_W = D_MODEL // N_BRANCHES

NA_HEAD_DIM = 128
NA_HEADS = BRANCH_W // NA_HEAD_DIM
NA_KH = 8
NA_KW = 16
NA_QB = 16
NA_BAND = 2 * NA_KW

DIFF_HEAD_DIM = 128
DIFF_HEADS = BRANCH_W // (2 * DIFF_HEAD_DIM)

MLA_NOPE = 128
MLA_ROPE = 64
MLA_V = 128
MLA_HEADS = BRANCH_W // MLA_V
MLA_Q_RANK = 3 * BRANCH_W // 4
MLA_KV_RANK = BRANCH_W // 2

HGRN_DK = 128
HGRN_DV = 128
HGRN_HEADS = BRANCH_W // HGRN_DV
HGRN_CHUNK = 64

FFN_DENSE = 256 * ((8 * D_MODEL // 3 + 255) // 256)
MOE_EXPERTS = 8
MOE_TOP_K = 2
FFN_EXPERT = D_MODEL // 2

IN_SPLITS = [BRANCH_W, BRANCH_W, BRANCH_W,
             2 * DIFF_HEADS * DIFF_HEAD_DIM, 2 * DIFF_HEADS * DIFF_HEAD_DIM,
             DIFF_HEADS * 2 * DIFF_HEAD_DIM,
             MLA_Q_RANK, MLA_KV_RANK, MLA_ROPE,
             HGRN_HEADS * HGRN_DK, HGRN_HEADS * HGRN_DK, HGRN_HEADS * HGRN_DK,
             HGRN_HEADS * HGRN_DV, HGRN_HEADS * HGRN_DV]
IN_WIDTH = sum(IN_SPLITS)

kernel_name = "hybrid_dit_na_diff_mla_hgrn2_moe"

f32 = jnp.float32


def rmsnorm(x, g):
    xf = x.astype(f32)
    y = xf * lax.rsqrt(jnp.mean(xf * xf, axis=-1, keepdims=True) + RMS_EPS)
    return (y * g.astype(f32)).astype(x.dtype)


def split_cols(z, sizes):
    offs = [int(o) for o in np.cumsum(sizes)[:-1]]
    return jnp.split(z, offs, axis=-1)


def heads(t, n_heads):
    b, n, w = t.shape
    return t.reshape(b, n, n_heads, w // n_heads).transpose(0, 2, 1, 3)


def merge_heads(o):
    b, h, n, d = o.shape
    return o.transpose(0, 2, 1, 3).reshape(b, n, h * d)


def rope_1d(x, pos):
    half = x.shape[-1] // 2
    freqs = ROPE_BASE ** (-(jnp.arange(half, dtype=f32) / half))
    ang = pos.astype(f32)[:, None] * freqs
    cos, sin = jnp.cos(ang), jnp.sin(ang)
    x1, x2 = x[..., :half].astype(f32), x[..., half:].astype(f32)
    return jnp.concatenate([x1 * cos - x2 * sin, x2 * cos + x1 * sin], axis=-1).astype(x.dtype)


def rope_2d(x, row, col):
    h = x.shape[-1] // 2
    return jnp.concatenate([rope_1d(x[..., :h], row), rope_1d(x[..., h:], col)], axis=-1)


def softmax_attention(q, k, v):
    s = jnp.einsum('bhqd,bhkd->bhqk', q, k).astype(f32) * (q.shape[-1] ** -0.5)
    p = jax.nn.softmax(s, axis=-1).astype(v.dtype)
    return jnp.einsum('bhqk,bhkd->bhqd', p, v)


def sweep_query_blocks(attend, queries):
    b, h, n = queries[0].shape[:3]
    nb = n // Q_BLOCK
    blocks = tuple(jnp.moveaxis(q.reshape(b, h, nb, Q_BLOCK, q.shape[-1]), 2, 0) for q in queries)
    out = lax.map(lambda qs: attend(*qs), blocks)
    return jnp.moveaxis(out, 0, 2).reshape(b, h, n, out.shape[-1])


def neighbourhood_attention(q, k, v, k_ctx, v_ctx, rpb):
    b, h, n, d = q.shape
    rows = n // GRID_W
    kh = min(NA_KH, rows)
    nj = GRID_W // NA_QB
    scale = d ** -0.5
    qcol = np.arange(GRID_W).reshape(nj, NA_QB)
    cstart = np.clip(qcol - NA_KW // 2, 0, GRID_W - NA_KW)
    bstart = np.clip(np.arange(nj) * NA_QB - NA_KW // 2, 0, GRID_W - NA_BAND)
    bcol = bstart[:, None] + np.arange(NA_BAND)
    col_ok = (bcol[:, None, :] >= cstart[:, :, None]) & (bcol[:, None, :] < cstart[:, :, None] + NA_KW)
    col_ok = jnp.asarray(col_ok)[:, :, None, :]
    dx_idx = np.clip(bcol[:, None, :] - qcol[:, :, None] + NA_KW - 1, 0, 2 * NA_KW - 2)
    kb = k.reshape(b, h, rows, GRID_W, d)[:, :, :, bcol]
    vb = v.reshape(b, h, rows, GRID_W, d)[:, :, :, bcol]
    qr = jnp.moveaxis(q.reshape(b, h, rows, nj, NA_QB, d), 2, 0)
    n_lat = kh * NA_BAND

    def row_block(args):
        r, q_r = args
        rs = jnp.clip(r - kh // 2, 0, rows - kh)
        k_r = lax.dynamic_slice_in_dim(kb, rs, kh, axis=2)
        v_r = lax.dynamic_slice_in_dim(vb, rs, kh, axis=2)
        s_lat = jnp.einsum('bhjqd,bhijud->bhjqiu', q_r, k_r).astype(f32) * scale
        dy_idx = rs + jnp.arange(kh) - r + NA_KH - 1
        bias = jnp.transpose(rpb[:, dy_idx][:, :, dx_idx], (0, 2, 3, 1, 4))
        s_lat = jnp.where(col_ok, s_lat + bias.astype(f32), -jnp.inf).reshape(b, h, nj, NA_QB, n_lat)
        s_ctx = jnp.einsum('bhjqd,bhcd->bhjqc', q_r, k_ctx).astype(f32) * scale
        p = jax.nn.softmax(jnp.concatenate([s_lat, s_ctx], axis=-1), axis=-1).astype(v.dtype)
        p_lat = p[..., :n_lat].reshape(b, h, nj, NA_QB, kh, NA_BAND)
        return (jnp.einsum('bhjqiu,bhijud->bhjqd', p_lat, v_r)
                + jnp.einsum('bhjqc,bhcd->bhjqd', p[..., n_lat:], v_ctx))

    o = lax.map(row_block, (jnp.arange(rows), qr))
    return jnp.moveaxis(o, 0, 2).reshape(b, h, n, d)


def na_branch(zl, zc, rpb, need_ctx):
    q, k, v = (heads(t, NA_HEADS) for t in zl)
    qc, kc, vc = (heads(t, NA_HEADS) for t in zc)
    o = merge_heads(neighbourhood_attention(q, k, v, kc, vc, rpb))
    oc = merge_heads(softmax_attention(qc, kc, vc)) if need_ctx else None
    return o, oc


def diff_attend(q1, q2, k1, k2, v, lam):
    scale = q1.shape[-1] ** -0.5
    p1 = jax.nn.softmax(jnp.einsum('bhqd,bhkd->bhqk', q1, k1).astype(f32) * scale, axis=-1)
    p2 = jax.nn.softmax(jnp.einsum('bhqd,bhkd->bhqk', q2, k2).astype(f32) * scale, axis=-1)
    a = (p1 - lam * p2).astype(v.dtype)
    return jnp.einsum('bhqk,bhkv->bhqv', a, v)


def diff_branch(zl, zc, lam_params, subln_g, lambda_init, row, col, need_ctx):
    lp = lam_params.astype(f32)
    lam = jnp.exp(jnp.sum(lp[0] * lp[1])) - jnp.exp(jnp.sum(lp[2] * lp[3])) + lambda_init
    q = rope_2d(heads(zl[0], 2 * DIFF_HEADS), row, col)
    k = rope_2d(heads(zl[1], 2 * DIFF_HEADS), row, col)
    v = heads(zl[2], DIFF_HEADS)
    qc, kc, vc = heads(zc[0], 2 * DIFF_HEADS), heads(zc[1], 2 * DIFF_HEADS), heads(zc[2], DIFF_HEADS)
    k_all = jnp.concatenate([kc, k], axis=2)
    v_all = jnp.concatenate([vc, v], axis=2)
    k1, k2 = k_all[:, 0::2], k_all[:, 1::2]
    o = sweep_query_blocks(lambda a, bq: diff_attend(a, bq, k1, k2, v_all, lam), (q[:, 0::2], q[:, 1::2]))

    def post(t):
        return merge_heads(rmsnorm(t, subln_g) * (1.0 - lambda_init))

    oc = post(diff_attend(qc[:, 0::2], qc[:, 1::2], kc[:, 0::2], kc[:, 1::2], vc, lam)) if need_ctx else None
    return post(o), oc


def mla_attend(qn, qr, kn, kr, v):
    scale = (MLA_NOPE + MLA_ROPE) ** -0.5
    s = (jnp.einsum('bhqd,bhkd->bhqk', qn, kn) + jnp.einsum('bhqd,bkd->bhqk', qr, kr)).astype(f32) * scale
    p = jax.nn.softmax(s, axis=-1).astype(v.dtype)
    return jnp.einsum('bhqk,bhkd->bhqd', p, v)


def mla_branch(zl, zc, q_norm_g, w_q_up, kv_norm_g, w_kv_up, row, col, need_ctx):
    def expand(zq, zkv):
        q = heads(rmsnorm(zq, q_norm_g) @ w_q_up, MLA_HEADS)
        kv = heads(rmsnorm(zkv, kv_norm_g) @ w_kv_up, MLA_HEADS)
        return q[..., :MLA_NOPE], q[..., MLA_NOPE:], kv[..., :MLA_NOPE], kv[..., MLA_NOPE:]

    qn, qr, kn, v = expand(zl[0], zl[1])
    qr = rope_2d(qr, row, col)
    kr = rope_2d(zl[2], row, col)
    qn_c, qr_c, kn_c, v_c = expand(zc[0], zc[1])
    kr_c = zc[2]
    kn_all = jnp.concatenate([kn_c, kn], axis=2)
    kr_all = jnp.concatenate([kr_c, kr], axis=1)
    v_all = jnp.concatenate([v_c, v], axis=2)
    o = sweep_query_blocks(lambda a, bq: mla_attend(a, bq, kn_all, kr_all, v_all), (qn, qr))
    oc = merge_heads(mla_attend(qn_c, qr_c, kn_c, kr_c, v_c)) if need_ctx else None
    return merge_heads(o), oc


def gla_chunkwise(q, k, v, logf, s0):
    b, h, n, dk = q.shape
    dv = v.shape[-1]
    nc = n // HGRN_CHUNK
    tri = jnp.tril(jnp.ones((HGRN_CHUNK, HGRN_CHUNK), dtype=bool))[:, :, None]

    def chunks(t):
        return jnp.moveaxis(t.reshape(b, h, nc, HGRN_CHUNK, t.shape[-1]), 2, 0)

    def step(s, inp):
        qc, kc, vc, gc = inp
        cum = jnp.cumsum(gc, axis=-2)
        o = jnp.einsum('bhtk,bhkv->bhtv', qc * jnp.exp(cum), s)
        rel = cum[:, :, :, None, :] - cum[:, :, None, :, :]
        decay = jnp.exp(jnp.where(tri, rel, -jnp.inf))
        att = jnp.einsum('bhtk,bhtsk,bhsk->bhts', qc, decay, kc)
        o = o + jnp.einsum('bhts,bhsv->bhtv', att, vc)
        last = cum[:, :, -1:, :]
        s_new = jnp.exp(last[:, :, 0, :])[..., None] * s + jnp.einsum('bhsk,bhsv->bhkv', kc * jnp.exp(last - cum), vc)
        return s_new, o

    s_fin, o = lax.scan(step, s0, (chunks(q), chunks(k), chunks(v), chunks(logf)))
    return jnp.moveaxis(o, 0, 2).reshape(b, h, n, dv), s_fin


def hgrn2_branch(zl, zc, lb, norm_g, need_ctx):
    lbh = lb.astype(f32).reshape(HGRN_HEADS, 1, HGRN_DK)

    def prep(z):
        zq, zff, zfb, zi, zg = z
        q = heads(jax.nn.silu(zq.astype(f32)), HGRN_HEADS) * (HGRN_DK ** -0.5)
        ff = lbh + (1.0 - lbh) * jax.nn.sigmoid(heads(zff.astype(f32), HGRN_HEADS))
        fb = lbh + (1.0 - lbh) * jax.nn.sigmoid(heads(zfb.astype(f32), HGRN_HEADS))
        i = heads(zi.astype(f32), HGRN_HEADS)
        g = heads(zg.astype(f32), HGRN_HEADS)
        return q, ff, fb, i, g

    def flip(t):
        return jnp.flip(t, axis=2)

    def readout(o, g, dtype):
        return merge_heads(rmsnorm(o, norm_g) * jax.nn.silu(g)).astype(dtype)

    q, ff, fb, i, g = prep(zl)
    qc, ffc, fbc, ic, gc = prep(zc)
    b = q.shape[0]
    s0 = jnp.zeros((b, HGRN_HEADS, HGRN_DK, HGRN_DV), f32)
    oc_f, s_f = gla_chunkwise(qc, 1.0 - ffc, ic, jnp.log(ffc), s0)
    oc_b, s_b = gla_chunkwise(flip(qc), flip(1.0 - fbc), flip(ic), flip(jnp.log(fbc)), s0)
    o_f, _ = gla_chunkwise(q, 1.0 - ff, i, jnp.log(ff), s_f)
    o_b, _ = gla_chunkwise(flip(q), flip(1.0 - fb), flip(i), flip(jnp.log(fb)), s_b)
    o = readout(o_f + flip(o_b), g, zl[0].dtype)
    oc = readout(oc_f + flip(oc_b), gc, zc[0].dtype) if need_ctx else None
    return o, oc


def merge_branches(h, outs, w_branch, w_gate, b_gate, w_out):
    terms = [jax.nn.sigmoid(h @ w_gate[j] + b_gate[j]) * (o @ w_branch[j]) for j, o in enumerate(outs)]
    return sum(terms[1:], terms[0]) @ w_out


def token_mixer(h, hc, row, col, lambda_init, need_ctx, w_in, na_rpb, diff_lambda, diff_subln_g,
                mla_q_norm_g, mla_w_q_up, mla_kv_norm_g, mla_w_kv_up, hgrn_lb, hgrn_norm_g,
                w_branch, w_gate, b_gate, w_out):
    zl = split_cols(h @ w_in, IN_SPLITS)
    zc = split_cols(hc @ w_in, IN_SPLITS)
    a, ac = na_branch(zl[0:3], zc[0:3], na_rpb, need_ctx)
    d, dc = diff_branch(zl[3:6], zc[3:6], diff_lambda, diff_subln_g, lambda_init, row, col, need_ctx)
    m, mc = mla_branch(zl[6:9], zc[6:9], mla_q_norm_g, mla_w_q_up, mla_kv_norm_g, mla_w_kv_up, row, col, need_ctx)
    r, rc = hgrn2_branch(zl[9:14], zc[9:14], hgrn_lb, hgrn_norm_g, need_ctx)
    y = merge_branches(h, (a, d, m, r), w_branch, w_gate, b_gate, w_out)
    yc = merge_branches(hc, (ac, dc, mc, rc), w_branch, w_gate, b_gate, w_out) if need_ctx else None
    return y, yc


def swiglu(h, w1, w3, w2):
    return (jax.nn.silu(h @ w1) * (h @ w3)) @ w2


def moe_swiglu(h, w_router, w1, w3, w2):
    logits = jnp.einsum('bnd,de->bne', h, w_router).astype(f32)
    top_val, top_idx = lax.top_k(logits, MOE_TOP_K)
    top_p = jax.nn.softmax(top_val, axis=-1)
    combine = jnp.sum(jax.nn.one_hot(top_idx, MOE_EXPERTS, dtype=f32) * top_p[..., None], axis=-2).astype(h.dtype)
    terms = [combine[..., e:e + 1] * swiglu(h, w1[e], w3[e], w2[e]) for e in range(MOE_EXPERTS)]
    return sum(terms[1:], terms[0])


def adaln(cond, w_ada, b_ada):
    return jnp.split(jax.nn.silu(cond) @ w_ada + b_ada, 6, axis=-1)


def modulate(h, shift, scale):
    return h * (1.0 + scale) + shift


def setup_inputs(seed: int = 0) -> dict:
    key = jax.random.key(seed)
    ks = jax.random.split(key, 30)
    L = DEPTH
    ND = (DEPTH + 1) // 2
    NM = DEPTH // 2
    D = D_MODEL

    def nrm(k, shape, scale):
        return jax.random.normal(k, shape, jnp.float32) * scale

    def gain(k, shape):
        return 1.0 + 0.02 * jax.random.normal(k, shape, jnp.float32)

    return {
        "x": nrm(ks[0], (BATCH, SEQ, D), 1.0),
        "c": nrm(ks[1], (BATCH, D), 1.0),
        "ctx": nrm(ks[2], (BATCH, CTX_LEN, D), 1.0),
        "c_ctx": nrm(ks[3], (D,), 1.0),
        "norm1_g": gain(ks[4], (L, D)),
        "norm2_g": gain(ks[5], (L, D)),
        "w_ada": nrm(ks[6], (L, D, 6 * D), 0.5 * D ** -0.5),
        "b_ada": nrm(ks[7], (L, 6 * D), 0.02),
        "w_in": nrm(ks[8], (L, D, IN_WIDTH), D ** -0.5),
        "na_rpb": nrm(ks[9], (L, NA_HEADS, 2 * NA_KH - 1, 2 * NA_KW - 1), 0.1),
        "diff_lambda": nrm(ks[10], (L, 4, DIFF_HEAD_DIM), 0.1),
        "diff_subln_g": gain(ks[11], (L, 2 * DIFF_HEAD_DIM)),
        "mla_q_norm_g": gain(ks[12], (L, MLA_Q_RANK)),
        "mla_w_q_up": nrm(ks[13], (L, MLA_Q_RANK, MLA_HEADS * (MLA_NOPE + MLA_ROPE)), MLA_Q_RANK ** -0.5),
        "mla_kv_norm_g": gain(ks[14], (L, MLA_KV_RANK)),
        "mla_w_kv_up": nrm(ks[15], (L, MLA_KV_RANK, MLA_HEADS * (MLA_NOPE + MLA_V)), MLA_KV_RANK ** -0.5),
        "hgrn_lower_bounds": nrm(ks[16], (L, HGRN_HEADS * HGRN_DK), 0.5),
        "hgrn_norm_g": gain(ks[17], (L, HGRN_DV)),
        "w_branch": nrm(ks[18], (L, N_BRANCHES, BRANCH_W, D), BRANCH_W ** -0.5),
        "w_gate": nrm(ks[19], (L, N_BRANCHES, D, D), D ** -0.5),
        "b_gate": nrm(ks[20], (L, N_BRANCHES, D), 0.02),
        "w_out": nrm(ks[21], (L, D, D), D ** -0.5),
        "ffn_w1": nrm(ks[22], (ND, D, FFN_DENSE), D ** -0.5),
        "ffn_w3": nrm(ks[23], (ND, D, FFN_DENSE), D ** -0.5),
        "ffn_w2": nrm(ks[24], (ND, FFN_DENSE, D), FFN_DENSE ** -0.5),
        "moe_router": nrm(ks[25], (NM, D, MOE_EXPERTS), D ** -0.5),
        "moe_w1": nrm(ks[26], (NM, MOE_EXPERTS, D, FFN_EXPERT), D ** -0.5),
        "moe_w3": nrm(ks[27], (NM, MOE_EXPERTS, D, FFN_EXPERT), D ** -0.5),
        "moe_w2": nrm(ks[28], (NM, MOE_EXPERTS, FFN_EXPERT, D), FFN_EXPERT ** -0.5),
        "final_norm_g": gain(ks[29], (D,)),
    }


def reference(x, c, ctx, c_ctx, norm1_g, norm2_g, w_ada, b_ada, w_in, na_rpb, diff_lambda, diff_subln_g,
              mla_q_norm_g, mla_w_q_up, mla_kv_norm_g, mla_w_kv_up, hgrn_lower_bounds, hgrn_norm_g,
              w_branch, w_gate, b_gate, w_out, ffn_w1, ffn_w3, ffn_w2, moe_router, moe_w1, moe_w3, moe_w2,
              final_norm_g):
    n = x.shape[1]
    pos = jnp.arange(n)
    row, col = pos // GRID_W, pos % GRID_W
    lb_all = jnp.cumsum(jax.nn.softmax(hgrn_lower_bounds.astype(f32), axis=0), axis=0)
    lb_all = lb_all - lb_all[0:1]
    cond_lat = c[:, None, :]
    cond_ctx = c_ctx[None, None, :]
    for l in range(DEPTH):
        need_ctx = l < DEPTH - 1
        lambda_init = 0.8 - 0.6 * math.exp(-0.3 * l)
        sh1, sc1, g1, sh2, sc2, g2 = adaln(cond_lat, w_ada[l], b_ada[l])
        sh1c, sc1c, g1c, sh2c, sc2c, g2c = adaln(cond_ctx, w_ada[l], b_ada[l])
        h = modulate(rmsnorm(x, norm1_g[l]), sh1, sc1)
        hc = modulate(rmsnorm(ctx, norm1_g[l]), sh1c, sc1c)
        mix, mix_c = token_mixer(h, hc, row, col, lambda_init, need_ctx, w_in[l], na_rpb[l], diff_lambda[l],
                                 diff_subln_g[l], mla_q_norm_g[l], mla_w_q_up[l], mla_kv_norm_g[l],
                                 mla_w_kv_up[l], lb_all[l], hgrn_norm_g[l], w_branch[l], w_gate[l],
                                 b_gate[l], w_out[l])
        x = x + g1 * mix
        if l % 2 == 0:
            j = l // 2
            ffn = lambda t: swiglu(t, ffn_w1[j], ffn_w3[j], ffn_w2[j])
        else:
            j = l // 2
            ffn = lambda t: moe_swiglu(t, moe_router[j], moe_w1[j], moe_w3[j], moe_w2[j])
        x = x + g2 * ffn(modulate(rmsnorm(x, norm2_g[l]), sh2, sc2))
        if need_ctx:
            ctx = ctx + g1c * mix_c
            ctx = ctx + g2c * ffn(modulate(rmsnorm(ctx, norm2_g[l]), sh2c, sc2c))
    return rmsnorm(x, final_norm_g)
```

```python
import functools
import math

import numpy as np
import jax
import jax.numpy as jnp
from jax import lax
from jax.experimental import pallas as pl
from jax.experimental.pallas import tpu as pltpu

F32 = jnp.float32
BF16 = jnp.bfloat16

GRID_W = 64
RMS_EPS = 1e-6
ROPE_BASE = 10000.0
NA_KH, NA_KW = 8, 16
HEAD_DIM = 128
MLA_ROPE = 64
HGRN_CHUNK = 64
MOE_EXPERTS = 8
NEG_BIG = -1e30

VMEM_LIMIT_V7X = 60000 * 1024
ROW_TILE = 512


def _cparams(n_axes, vmem_mb):
    return pltpu.CompilerParams(dimension_semantics=("arbitrary",) * n_axes,
                                vmem_limit_bytes=min(int(vmem_mb * 2**20), VMEM_LIMIT_V7X))


def _silu(x):
    return x * jax.nn.sigmoid(x)


def _dot(a, b):
    return jnp.dot(a, b, preferred_element_type=F32)


def _dot_nt(a, b):
    return lax.dot_general(a, b, (((1,), (1,)), ((), ())), preferred_element_type=F32)


def _dot_tn(a, b):
    return lax.dot_general(a, b, (((0,), (0,)), ((), ())), preferred_element_type=F32)


def _adaln_kernel(c_ref, w_ref, b_ref, o_ref):
    x = _silu(c_ref[...]).astype(BF16)
    o_ref[...] = _dot(x, w_ref[...].astype(BF16)) + b_ref[...]


def adaln_all(cond8, w_ada, b_ada):
    L, D, N6 = w_ada.shape
    tn = 512
    return pl.pallas_call(
        _adaln_kernel,
        grid=(L, N6 // tn),
        in_specs=[pl.BlockSpec((8, D), lambda l, j: (0, 0)),
                  pl.BlockSpec((None, D, tn), lambda l, j: (l, 0, j)),
                  pl.BlockSpec((None, 1, tn), lambda l, j: (l, 0, j))],
        out_specs=pl.BlockSpec((None, 8, tn), lambda l, j: (l, 0, j)),
        out_shape=jax.ShapeDtypeStruct((L, 8, N6), F32),
        compiler_params=_cparams(2, 40),
        name="adaln",
    )(cond8, w_ada, b_ada.reshape(L, 1, N6))


def _norm_mod_kernel(x_ref, g_ref, sh_ref, sc_ref, *rest, with_router):
    x = x_ref[...]
    y = x * lax.rsqrt(jnp.mean(x * x, axis=-1, keepdims=True) + RMS_EPS) * g_ref[...]
    h = y * (1.0 + sc_ref[...]) + sh_ref[...]
    if not with_router:
        (h_ref,) = rest
        h_ref[...] = h.astype(BF16)
        return
    wr_ref, h_ref, comb_ref = rest
    h_ref[...] = h.astype(BF16)
    logits = jnp.dot(h, wr_ref[...], precision=lax.Precision.HIGHEST, preferred_element_type=F32)
    lane = lax.broadcasted_iota(jnp.int32, logits.shape, 1).astype(F32)
    logits = jnp.where(lane < MOE_EXPERTS, logits, -jnp.inf)
    m1 = jnp.max(logits, axis=-1, keepdims=True)
    i1 = jnp.min(jnp.where(logits == m1, lane, 128.0), axis=-1, keepdims=True)
    rest_l = jnp.where(lane == i1, -jnp.inf, logits)
    m2 = jnp.max(rest_l, axis=-1, keepdims=True)
    i2 = jnp.min(jnp.where(rest_l == m2, lane, 128.0), axis=-1, keepdims=True)
    e2 = jnp.exp(m2 - m1)
    p1 = 1.0 / (1.0 + e2)
    p2 = e2 / (1.0 + e2)
    comb_ref[...] = jnp.where(lane == i1, p1, 0.0) + jnp.where(lane == i2, p2, 0.0)


def _grp_of_tile(i, tm, lat_rows_per_sample, n_samples):
    return jnp.minimum((i * tm) // lat_rows_per_sample, n_samples)


def norm_mod(x, g, shift, scale, n_lat, n_samples, w_router=None):
    M, D = x.shape
    tm = 256
    G = shift.shape[0]
    grp = lambda i: (_grp_of_tile(i, tm, n_lat, n_samples), 0, 0)
    in_specs = [pl.BlockSpec((tm, D), lambda i: (i, 0)),
                pl.BlockSpec((1, D), lambda i: (0, 0)),
                pl.BlockSpec((None, 1, D), grp),
                pl.BlockSpec((None, 1, D), grp)]
    args = [x, g.reshape(1, D), shift.reshape(G, 1, D), scale.reshape(G, 1, D)]
    out_specs = [pl.BlockSpec((tm, D), lambda i: (i, 0))]
    out_shape = [jax.ShapeDtypeStruct((M, D), BF16)]
    if w_router is not None:
        wr = jnp.zeros((D, 128), F32).at[:, :w_router.shape[1]].set(w_router)
        in_specs.append(pl.BlockSpec((D, 128), lambda i: (0, 0)))
        args.append(wr)
        out_specs.append(pl.BlockSpec((tm, 128), lambda i: (i, 0)))
        out_shape.append(jax.ShapeDtypeStruct((M, 128), F32))
    outs = pl.pallas_call(
        functools.partial(_norm_mod_kernel, with_router=w_router is not None),
        grid=(M // tm,),
        in_specs=in_specs, out_specs=out_specs, out_shape=out_shape,
        compiler_params=_cparams(1, 40),
        name="norm_mod_router" if w_router is not None else "norm_mod",
    )(*args)
    return outs if w_router is not None else outs[0]


def _final_norm_kernel(x_ref, g_ref, o_ref):
    x = x_ref[...]
    o_ref[...] = x * lax.rsqrt(jnp.mean(x * x, axis=-1, keepdims=True) + RMS_EPS) * g_ref[...]


def final_norm(x, g):
    M, D = x.shape
    tm = 256
    return pl.pallas_call(
        _final_norm_kernel, grid=(M // tm,),
        in_specs=[pl.BlockSpec((tm, D), lambda i: (i, 0)), pl.BlockSpec((1, D), lambda i: (0, 0))],
        out_specs=pl.BlockSpec((tm, D), lambda i: (i, 0)),
        out_shape=jax.ShapeDtypeStruct((M, D), F32),
        compiler_params=_cparams(1, 40), name="final_norm",
    )(x, g.reshape(1, D))


def _resident_mm_kernel(x_ref, w_ref, *rest, n_extra, epilogue):
    extras, o_ref, wbf_ref = rest[:n_extra], rest[n_extra], rest[n_extra + 1]

    @pl.when(pl.program_id(1) == 0)
    def _():
        wbf_ref[...] = w_ref[...].astype(BF16)

    acc = _dot(x_ref[...], wbf_ref[...])
    if epilogue is not None:
        acc = epilogue(acc, *[e[...] for e in extras])
    o_ref[...] = acc.astype(o_ref.dtype)


def _mm_call(x, w, K, *, n_blocks, tn, tm, out_dtype, col0_blocks=0, n_row_tiles=None, x_col_block=0,
             w_row_block=0, epilogue=None, extras=(), extra_specs=(), w_single_buffer=False, name="mm"):
    M = x.shape[0]
    n_row_tiles = M // tm if n_row_tiles is None else n_row_tiles
    w_kwargs = dict(pipeline_mode=pl.Buffered(1)) if w_single_buffer else {}
    in_specs = [pl.BlockSpec((tm, K), lambda j, i: (i, x_col_block)),
                pl.BlockSpec((K, tn), lambda j, i: (w_row_block, col0_blocks + j), **w_kwargs)]
    in_specs += list(extra_specs)
    out_bytes = jnp.dtype(out_dtype).itemsize
    vmem = (2 * tm * K * 2 + (1 if w_single_buffer else 2) * K * tn * 4 + K * tn * 2
            + 2 * tm * tn * out_bytes + 3 * tm * tn * 4 + len(extras) * 2 * tm * tn * 4) / 2**20 + 6
    return pl.pallas_call(
        functools.partial(_resident_mm_kernel, n_extra=len(extras), epilogue=epilogue),
        grid=(n_blocks, n_row_tiles),
        in_specs=in_specs,
        out_specs=pl.BlockSpec((tm, tn), lambda j, i: (i, j)),
        out_shape=jax.ShapeDtypeStruct((n_row_tiles * tm, n_blocks * tn), out_dtype),
        scratch_shapes=[pltpu.VMEM((K, tn), BF16)],
        compiler_params=_cparams(2, vmem),
        name=name,
    )(x, w, *extras)


def _residual_epilogue(acc, xres, gate):
    return xres + gate * acc


def _rope_tables(n_tokens, n_ident, group, tile_rows):
    half = group // 2
    q = half // 2
    pos = np.arange(n_tokens)
    row, col = pos // GRID_W, pos % GRID_W
    freqs = ROPE_BASE ** (-(np.arange(q, dtype=np.float64) / q))
    lane = np.arange(128)
    in_group = lane % group
    axis_pos = np.where((in_group < half)[None, :], row[:, None], col[:, None]).astype(np.float64)
    ang = axis_pos * freqs[(in_group % half) % q][None, :]
    sign = np.where((in_group % half) < q, -1.0, 1.0)[None, :]
    cos, sin = np.cos(ang), np.sin(ang) * sign
    n_pad = -(-n_ident // tile_rows) * tile_rows
    cos = np.concatenate([cos, np.ones((n_pad, 128))], axis=0)
    sin = np.concatenate([sin, np.zeros((n_pad, 128))], axis=0)
    return jnp.asarray(cos, F32), jnp.asarray(sin, F32)


def _rope_apply(x, cos, sin, q):
    lane = lax.broadcasted_iota(jnp.int32, x.shape, 1)
    partner = jnp.where((lane % (2 * q)) < q, pltpu.roll(x, 128 - q, 1), pltpu.roll(x, q, 1))
    return x * cos + partner * sin


def _diff_rope_kernel(q_ref, k_ref, cos_ref, sin_ref, qo_ref, ko_ref):
    cos, sin = cos_ref[...], sin_ref[...]
    for src, dst in ((q_ref, qo_ref), (k_ref, ko_ref)):
        for c in range(src.shape[1] // 128):
            sl = slice(c * 128, (c + 1) * 128)
            dst[:, sl] = _rope_apply(src[:, sl].astype(F32), cos, sin, 32).astype(BF16)


def _table_block(i, tm, n_tok, n_lat_total):
    return jnp.where(i * tm < n_lat_total, (i * tm % n_tok) // tm, n_tok // tm)


def diff_rope(z1, cosd, sind, n_tok, n_lat_total):
    M = z1.shape[0]
    tm = ROW_TILE
    W = 1024
    tb = lambda i: (_table_block(i, tm, n_tok, n_lat_total), 0)
    return pl.pallas_call(
        _diff_rope_kernel, grid=(M // tm,),
        in_specs=[pl.BlockSpec((tm, W), lambda i: (i, 3)), pl.BlockSpec((tm, W), lambda i: (i, 4)),
                  pl.BlockSpec((tm, 128), tb), pl.BlockSpec((tm, 128), tb)],
        out_specs=[pl.BlockSpec((tm, W), lambda i: (i, 0))] * 2,
        out_shape=[jax.ShapeDtypeStruct((M, W), BF16)] * 2,
        compiler_params=_cparams(1, 32), name="diff_rope",
    )(z1, z1, cosd, sind)


def _mla_prep_kernel(cq_ref, ckva_ref, ckvb_ref, kr_ref, cos_ref, sin_ref, gq_ref, gkv_ref, wq_ref, wkv_ref,
                     qn_ref, qr_ref, kn_ref, v_ref, kro_ref, wq_bf, wkv_bf):
    @pl.when(pl.program_id(0) == 0)
    def _():
        wq_bf[...] = wq_ref[...].astype(BF16)
        wkv_bf[...] = wkv_ref[...].astype(BF16)

    cos, sin = cos_ref[...], sin_ref[...]
    cq = cq_ref[...].astype(F32)
    cqn = (cq * lax.rsqrt(jnp.mean(cq * cq, axis=-1, keepdims=True) + RMS_EPS) * gq_ref[...]).astype(BF16)
    q = _dot(cqn, wq_bf[...])
    n_nope = qn_ref.shape[1]
    qn_ref[...] = q[:, :n_nope].astype(BF16)
    for c in range(qr_ref.shape[1] // 128):
        sl = slice(c * 128, (c + 1) * 128)
        qr_ref[:, sl] = _rope_apply(q[:, n_nope + c * 128:n_nope + (c + 1) * 128], cos, sin, 16).astype(BF16)
    a = ckva_ref[...].astype(F32)
    b = ckvb_ref[...].astype(F32)
    ha = a.shape[1]
    ms = (jnp.sum(a * a, axis=-1, keepdims=True) + jnp.sum(b * b, axis=-1, keepdims=True)) / (2 * ha)
    r = lax.rsqrt(ms + RMS_EPS)
    g = gkv_ref[...]
    an = (a * r * g[:, :ha]).astype(BF16)
    bn = (b * r * g[:, ha:]).astype(BF16)
    kv = _dot(an, wkv_bf[:ha, :]) + _dot(bn, wkv_bf[ha:, :])
    n_kn = kn_ref.shape[1]
    kn_ref[...] = kv[:, :n_kn].astype(BF16)
    v_ref[...] = kv[:, n_kn:].astype(BF16)
    kro_ref[...] = _rope_apply(kr_ref[...].astype(F32), cos, sin, 16).astype(BF16)


def mla_prep(z1, cosm, sinm, gq, wq, gkv, wkv, n_tok, n_lat_total):
    M = z1.shape[0]
    tm = ROW_TILE
    RQ, RKV = wq.shape[0], wkv.shape[0]
    NQ, NKV = wq.shape[1], wkv.shape[1]
    n_nope = NKV // 2
    tb = lambda i: (_table_block(i, tm, n_tok, n_lat_total), 0)
    const = lambda i: (0, 0)
    return pl.pallas_call(
        _mla_prep_kernel, grid=(M // tm,),
        in_specs=[pl.BlockSpec((tm, RQ), lambda i: (i, 6144 // RQ)),
                  pl.BlockSpec((tm, RKV // 2), lambda i: (i, 6912 // (RKV // 2))),
                  pl.BlockSpec((tm, RKV // 2), lambda i: (i, 6912 // (RKV // 2) + 1)),
                  pl.BlockSpec((tm, 128), lambda i: (i, 7424 // 128)),
                  pl.BlockSpec((tm, 128), tb), pl.BlockSpec((tm, 128), tb),
                  pl.BlockSpec((1, RQ), const), pl.BlockSpec((1, RKV), const),
                  pl.BlockSpec((RQ, NQ), const), pl.BlockSpec((RKV, NKV), const)],
        out_specs=[pl.BlockSpec((tm, n_nope), lambda i: (i, 0)),
                   pl.BlockSpec((tm, NQ - n_nope), lambda i: (i, 0)),
                   pl.BlockSpec((tm, n_nope), lambda i: (i, 0)),
                   pl.BlockSpec((tm, n_nope), lambda i: (i, 0)),
                   pl.BlockSpec((tm, 128), lambda i: (i, 0))],
        out_shape=[jax.ShapeDtypeStruct((M, n_nope), BF16), jax.ShapeDtypeStruct((M, NQ - n_nope), BF16),
                   jax.ShapeDtypeStruct((M, n_nope), BF16), jax.ShapeDtypeStruct((M, n_nope), BF16),
                   jax.ShapeDtypeStruct((M, 128), BF16)],
        scratch_shapes=[pltpu.VMEM((RQ, NQ), BF16), pltpu.VMEM((RKV, NKV), BF16)],
        compiler_params=_cparams(1, 48), name="mla_prep",
    )(z1, z1, z1, z1, cosm, sinm, gq.reshape(1, RQ), gkv.reshape(1, RKV), wq, wkv)


def _softmax_parts(s_list):
    m = functools.reduce(jnp.maximum, [jnp.max(s, axis=-1, keepdims=True) for s in s_list])
    p_list = [jnp.exp(s - m) for s in s_list]
    l = functools.reduce(lambda a, b: a + b, [jnp.sum(p, axis=-1, keepdims=True) for p in p_list])
    return p_list, l


def _mla_attn_kernel(qn_ref, qr_ref, *rest, with_latent, scale):
    if with_latent:
        knl_ref, vl_ref, krl_ref, knc_ref, vc_ref, krc_ref, o_ref = rest
    else:
        knc_ref, vc_ref, krc_ref, o_ref = rest
    for hh in range(2):
        dn = slice(hh * HEAD_DIM, (hh + 1) * HEAD_DIM)
        dr = slice(hh * MLA_ROPE, (hh + 1) * MLA_ROPE)
        qn, qr = qn_ref[:, dn], qr_ref[:, dr]
        s_list = [(_dot_nt(qn, knc_ref[:, dn]) + _dot_nt(qr, krc_ref[:, :MLA_ROPE])) * scale]
        v_list = [vc_ref[:, dn]]
        if with_latent:
            s_list.append((_dot_nt(qn, knl_ref[:, dn]) + _dot_nt(qr, krl_ref[:, :MLA_ROPE])) * scale)
            v_list.append(vl_ref[:, dn])
        p_list, l = _softmax_parts(s_list)
        inv = 1.0 / l
        o = functools.reduce(lambda a, b: a + b,
                             [_dot((p * inv).astype(BF16), v) for p, v in zip(p_list, v_list)])
        o_ref[:, dn] = o.astype(BF16)


def mla_attention(qn, qr, kn, v, kr, n_samples, n_tok, n_ctx, ctx_queries):
    n_lat_total = n_samples * n_tok
    tq = 256
    scale = (HEAD_DIM + MLA_ROPE) ** -0.5
    cblk = n_lat_total // n_ctx
    W = 2 * HEAD_DIM
    if ctx_queries:
        nq = n_ctx // tq
        qrow = lambda b, hp, t: cblk * (n_ctx // tq) + b * nq + t
        out_rows = n_samples * n_ctx
        orow = lambda b, hp, t: b * nq + t
    else:
        nq = n_tok // tq
        qrow = lambda b, hp, t: b * nq + t
        out_rows = n_lat_total
        orow = qrow
    in_specs = [pl.BlockSpec((tq, W), lambda b, hp, t: (qrow(b, hp, t), hp)),
                pl.BlockSpec((tq, 128), lambda b, hp, t: (qrow(b, hp, t), hp))]
    args = [qn, qr]
    if not ctx_queries:
        in_specs += [pl.BlockSpec((n_tok, W), lambda b, hp, t: (b, hp)),
                     pl.BlockSpec((n_tok, W), lambda b, hp, t: (b, hp)),
                     pl.BlockSpec((n_tok, 128), lambda b, hp, t: (b, 0))]
        args += [kn, v, kr]
    in_specs += [pl.BlockSpec((n_ctx, W), lambda b, hp, t: (cblk + b, hp)),
                 pl.BlockSpec((n_ctx, W), lambda b, hp, t: (cblk + b, hp)),
                 pl.BlockSpec((n_ctx, 128), lambda b, hp, t: (cblk + b, 0))]
    args += [kn, v, kr]
    return pl.pallas_call(
        functools.partial(_mla_attn_kernel, with_latent=not ctx_queries, scale=scale),
        grid=(n_samples, kn.shape[1] // W, nq),
        in_specs=in_specs,
        out_specs=pl.BlockSpec((tq, W), lambda b, hp, t: (orow(b, hp, t), hp)),
        out_shape=jax.ShapeDtypeStruct((out_rows, kn.shape[1]), BF16),
        compiler_params=_cparams(3, 56), name="mla_attn_ctx" if ctx_queries else "mla_attn",
    )(*args)


def _diff_attn_kernel(q_ref, *rest, with_latent, scale, out_scale):
    if with_latent:
        kl_ref, vl_ref, kc_ref, vc_ref, lam_ref, g_ref, o_ref = rest
    else:
        kc_ref, vc_ref, lam_ref, g_ref, o_ref = rest
    lp = lam_ref[...]
    lam = (jnp.exp(jnp.sum(lp[0:1] * lp[1:2], axis=-1, keepdims=True))
           - jnp.exp(jnp.sum(lp[2:3] * lp[3:4], axis=-1, keepdims=True)) + lam_ref[4:5, 0:1])
    d = HEAD_DIM
    k_refs = [kc_ref] + ([kl_ref] if with_latent else [])
    v_refs = [vc_ref] + ([vl_ref] if with_latent else [])
    probs = []
    for half in range(2):
        sl = slice(half * d, (half + 1) * d)
        q = q_ref[:, sl]
        p_list, l = _softmax_parts([_dot_nt(q, k[:, sl]) * scale for k in k_refs])
        inv = 1.0 / l
        probs.append([p * inv for p in p_list])
    o = functools.reduce(lambda a, b: a + b,
                         [_dot((p1 - lam * p2).astype(BF16), v[...]) for p1, p2, v in zip(probs[0], probs[1], v_refs)])
    y = o * lax.rsqrt(jnp.mean(o * o, axis=-1, keepdims=True) + RMS_EPS) * g_ref[...]
    o_ref[...] = (y * out_scale).astype(BF16)


def diff_attention(dq, dk, z1, lam_rows, subln_g, lambda_init, n_samples, n_tok, n_ctx, ctx_queries):
    n_lat_total = n_samples * n_tok
    tq = 256
    W = 2 * HEAD_DIM
    cblk = n_lat_total // n_ctx
    v_col0 = 5120 // W
    if ctx_queries:
        nq = n_ctx // tq
        qrow = lambda b, h, t: cblk * (n_ctx // tq) + b * nq + t
        out_rows = n_samples * n_ctx
        orow = lambda b, h, t: b * nq + t
    else:
        nq = n_tok // tq
        qrow = lambda b, h, t: b * nq + t
        out_rows = n_lat_total
        orow = qrow
    in_specs = [pl.BlockSpec((tq, W), lambda b, h, t: (qrow(b, h, t), h))]
    args = [dq]
    if not ctx_queries:
        in_specs += [pl.BlockSpec((n_tok, W), lambda b, h, t: (b, h)),
                     pl.BlockSpec((n_tok, W), lambda b, h, t: (b, v_col0 + h))]
        args += [dk, z1]
    in_specs += [pl.BlockSpec((n_ctx, W), lambda b, h, t: (cblk + b, h)),
                 pl.BlockSpec((n_ctx, W), lambda b, h, t: (cblk + b, v_col0 + h)),
                 pl.BlockSpec((8, HEAD_DIM), lambda b, h, t: (0, 0)),
                 pl.BlockSpec((1, W), lambda b, h, t: (0, 0))]
    args += [dk, z1, lam_rows, subln_g.reshape(1, W)]
    n_heads = dk.shape[1] // W
    return pl.pallas_call(
        functools.partial(_diff_attn_kernel, with_latent=not ctx_queries, scale=HEAD_DIM ** -0.5,
                          out_scale=1.0 - lambda_init),
        grid=(n_samples, n_heads, nq),
        in_specs=in_specs,
        out_specs=pl.BlockSpec((tq, W), lambda b, h, t: (orow(b, h, t), h)),
        out_shape=jax.ShapeDtypeStruct((out_rows, dk.shape[1]), BF16),
        compiler_params=_cparams(3, 56), name="diff_attn_ctx" if ctx_queries else "diff_attn",
    )(*args)


def _na_bias_tables(rpb, n_rows):
    W = GRID_W
    a = np.arange(8)[:, None, None, None]
    qc = np.arange(W)[None, :, None, None]
    kr = np.arange(16)[None, None, :, None]
    kc = np.arange(W)[None, None, None, :]
    cs = np.clip(qc - NA_KW // 2, 0, W - NA_KW)
    col_ok = (kc >= cs) & (kc < cs + NA_KW)
    dx = np.clip(kc - qc + NA_KW - 1, 0, 2 * NA_KW - 2)
    dys, oks = [], []
    for rs_l, dy in ((np.maximum(a - 4, 0), kr - a + 7), (a, kr - a + 3), (np.minimum(a + 4, 8), kr - a - 1)):
        row_ok = (kr >= rs_l) & (kr < rs_l + NA_KH)
        oks.append(np.broadcast_to(row_ok & col_ok, (8, W, 16, W)).reshape(8 * W, 16 * W))
        dys.append(np.broadcast_to(np.clip(dy, 0, 2 * NA_KH - 2), (8, W, 16, W)).reshape(8 * W, 16 * W))
    dxb = np.broadcast_to(dx, (8, W, 16, W)).reshape(8 * W, 16 * W)
    dy_idx = np.stack(dys)
    ok = np.stack(oks)
    bias = rpb.astype(F32)[:, dy_idx, dxb[None]]
    return jnp.where(jnp.asarray(ok)[None], bias, NEG_BIG)


def _na_attn_kernel(q_ref, kl_ref, vl_ref, kc_ref, vc_ref, bias_ref, o_ref, *, scale, n_rows):
    rb = pl.program_id(2)
    r0 = jnp.clip(rb * 8 - 4, 0, n_rows - 16)
    start = pl.multiple_of(r0 * GRID_W, 256)
    kw = kl_ref[pl.ds(start, 16 * GRID_W), :]
    vw = vl_ref[pl.ds(start, 16 * GRID_W), :]
    q = q_ref[...]
    s_lat = _dot_nt(q, kw) * scale + bias_ref[...]
    s_ctx = _dot_nt(q, kc_ref[...]) * scale
    (p_lat, p_ctx), l = _softmax_parts([s_lat, s_ctx])
    inv = 1.0 / l
    o = _dot((p_lat * inv).astype(BF16), vw) + _dot((p_ctx * inv).astype(BF16), vc_ref[...])
    o_ref[...] = o.astype(BF16)


def na_attention(z1, bias, n_samples, n_tok, n_ctx, n_heads):
    n_rows = n_tok // GRID_W
    assert n_rows >= 16 and n_rows % 8 == 0
    nrb = n_rows // 8
    tq = 8 * GRID_W
    d = HEAD_DIM
    cblk = n_samples * n_tok // n_ctx
    btype = lambda rb: jnp.where(rb == 0, 0, jnp.where(rb == nrb - 1, 2, 1))
    return pl.pallas_call(
        functools.partial(_na_attn_kernel, scale=d ** -0.5, n_rows=n_rows),
        grid=(n_samples, n_heads, nrb),
        in_specs=[pl.BlockSpec((tq, d), lambda b, h, rb: (b * nrb + rb, h)),
                  pl.BlockSpec((n_tok, d), lambda b, h, rb: (b, n_heads + h)),
                  pl.BlockSpec((n_tok, d), lambda b, h, rb: (b, 2 * n_heads + h)),
                  pl.BlockSpec((n_ctx, d), lambda b, h, rb: (cblk + b, n_heads + h)),
                  pl.BlockSpec((n_ctx, d), lambda b, h, rb: (cblk + b, 2 * n_heads + h)),
                  pl.BlockSpec((None, None, tq, 16 * GRID_W), lambda b, h, rb: (h, btype(rb), 0, 0))],
        out_specs=pl.BlockSpec((tq, d), lambda b, h, rb: (b * nrb + rb, h)),
        out_shape=jax.ShapeDtypeStruct((n_samples * n_tok, n_heads * d), BF16),
        compiler_params=_cparams(3, 40), name="na_attn",
    )(z1, z1, z1, z1, z1, bias)


def _ctx_attn_kernel(q_ref, k_ref, v_ref, o_ref, *, scale):
    (p,), l = _softmax_parts([_dot_nt(q_ref[...], k_ref[...]) * scale])
    o_ref[...] = _dot((p * (1.0 / l)).astype(BF16), v_ref[...]).astype(BF16)


def na_ctx_attention(z1, n_samples, n_tok, n_ctx, n_heads):
    d = HEAD_DIM
    cblk = n_samples * n_tok // n_ctx
    return pl.pallas_call(
        functools.partial(_ctx_attn_kernel, scale=d ** -0.5),
        grid=(n_samples, n_heads),
        in_specs=[pl.BlockSpec((n_ctx, d), lambda b, h: (cblk + b, h)),
                  pl.BlockSpec((n_ctx, d), lambda b, h: (cblk + b, n_heads + h)),
                  pl.BlockSpec((n_ctx, d), lambda b, h: (cblk + b, 2 * n_heads + h))],
        out_specs=pl.BlockSpec((n_ctx, d), lambda b, h: (b, h)),
        out_shape=jax.ShapeDtypeStruct((n_samples * n_ctx, n_heads * d), BF16),
        compiler_params=_cparams(2, 16), name="na_attn_ctx",
    )(z1, z1, z1)


def _hgrn_chunk(q_raw, f_raw, v, lb, st_ref, reverse):
    C, HW = q_raw.shape
    dk = HEAD_DIM
    n_heads = HW // dk
    q = _silu(q_raw) * (dk ** -0.5)
    f = lb + (1.0 - lb) * jax.nn.sigmoid(f_raw)
    g = jnp.log(f)
    k = 1.0 - f
    t_idx = lax.broadcasted_iota(jnp.int32, (C, 1), 0)
    p = (C - 1 - t_idx) if reverse else t_idx

    def prev(x, s):
        return pltpu.roll(x, (C - s) if reverse else s, 0)

    def nxt(x, s):
        return pltpu.roll(x, s if reverse else (C - s), 0)

    def incl_scan(b):
        x = g
        pb = jnp.bitwise_and(p, b - 1)
        s = 1
        while s < b:
            x = x + jnp.where(pb >= s, prev(x, s), 0.0)
            s *= 2
        return x

    def excl_rscan(b):
        if b == 1:
            return jnp.zeros_like(g)
        pb = jnp.bitwise_and(p, b - 1)
        x = jnp.where(pb <= b - 2, nxt(g, 1), 0.0)
        s = 1
        while s < b:
            x = x + jnp.where(pb + s <= b - 1, nxt(x, s), 0.0)
            s *= 2
        return x

    cum = incl_scan(C)
    q_in = (q * jnp.exp(cum)).astype(BF16)
    k_out = (k * jnp.exp(excl_rscan(C))).astype(BF16)
    total = jnp.exp(cum[0:1] if reverse else cum[C - 1:C])
    qb, kb, vb = q.astype(BF16), k.astype(BF16), v.astype(BF16)

    levels = []
    b = C // 2
    while b >= 1:
        upper = jnp.bitwise_and(p, b) != 0
        ql = jnp.where(upper, q * jnp.exp(incl_scan(b)), 0.0).astype(BF16)
        kl = jnp.where(upper, 0.0, k * jnp.exp(excl_rscan(b))).astype(BF16)
        levels.append((b, ql, kl))
        b //= 2

    s_idx = lax.broadcasted_iota(jnp.int32, (1, C), 1)
    ps = (C - 1 - s_idx) if reverse else s_idx
    outs = []
    for h in range(n_heads):
        sl = slice(h * dk, (h + 1) * dk)
        st_old = st_ref[h]
        att = jnp.where(p == ps, _dot_nt(qb[:, sl], kb[:, sl]), 0.0)
        for b, ql, kl in levels:
            same_pair = jnp.bitwise_and(p, -2 * b) == jnp.bitwise_and(ps, -2 * b)
            att = att + jnp.where(same_pair, _dot_nt(ql[:, sl], kl[:, sl]), 0.0)
        outs.append(_dot_nt(q_in[:, sl], st_old.astype(BF16)) + _dot(att.astype(BF16), vb[:, sl]))
        st_ref[h] = total[:, sl] * st_old + _dot_tn(vb[:, sl], k_out[:, sl])
    return jnp.concatenate(outs, axis=1)


def _hgrn_kernel(qf_ref, ff_ref, if_ref, qb_ref, fb_ref, ib_ref, lb_ref, of_ref, ob_ref, sf_ref, sb_ref):
    @pl.when(pl.program_id(1) == 0)
    def _():
        sf_ref[...] = jnp.zeros_like(sf_ref)
        sb_ref[...] = jnp.zeros_like(sb_ref)

    lb = lb_ref[...]
    of_ref[...] = _hgrn_chunk(qf_ref[...], ff_ref[...], if_ref[...], lb, sf_ref, reverse=False)
    ob_ref[...] = _hgrn_chunk(qb_ref[...], fb_ref[...], ib_ref[...], lb, sb_ref, reverse=True)


def hgrn_scan(z2, lb, n_samples, n_tok, n_ctx):
    M = z2.shape[0]
    HW = z2.shape[1] // 5
    C = HGRN_CHUNK
    ncc, ncl = n_ctx // C, n_tok // C
    lat_blocks = n_samples * ncl

    def fwd_row(b, c):
        return jnp.where(c < ncc, lat_blocks + b * ncc + c, b * ncl + (c - ncc))

    def bwd_row(b, c):
        return jnp.where(c < ncc, lat_blocks + b * ncc + (ncc - 1 - c), b * ncl + (ncl - 1 - (c - ncc)))

    spec = lambda rowf, col: pl.BlockSpec((C, HW), lambda b, c: (rowf(b, c), col))
    return pl.pallas_call(
        _hgrn_kernel, grid=(n_samples, ncc + ncl),
        in_specs=[spec(fwd_row, 0), spec(fwd_row, 1), spec(fwd_row, 3),
                  spec(bwd_row, 0), spec(bwd_row, 2), spec(bwd_row, 3),
                  pl.BlockSpec((1, HW), lambda b, c: (0, 0))],
        out_specs=[spec(fwd_row, 0), spec(bwd_row, 0)],
        out_shape=[jax.ShapeDtypeStruct((M, HW), F32)] * 2,
        scratch_shapes=[pltpu.VMEM((HW // HEAD_DIM, HEAD_DIM, HEAD_DIM), F32)] * 2,
        compiler_params=_cparams(2, 48), name="hgrn_scan",
    )(z2, z2, z2, z2, z2, z2, lb.reshape(1, HW))


def _hgrn_readout_kernel(of_ref, ob_ref, gz_ref, g_ref, o_ref):
    for h in range(of_ref.shape[1] // HEAD_DIM):
        sl = slice(h * HEAD_DIM, (h + 1) * HEAD_DIM)
        x = of_ref[:, sl] + ob_ref[:, sl]
        y = x * lax.rsqrt(jnp.mean(x * x, axis=-1, keepdims=True) + RMS_EPS) * g_ref[...]
        o_ref[:, sl] = (y * _silu(gz_ref[:, sl])).astype(BF16)


def hgrn_readout(o_f, o_b, z2, norm_g, n_rows):
    HW = o_f.shape[1]
    tm = ROW_TILE
    return pl.pallas_call(
        _hgrn_readout_kernel, grid=(n_rows // tm,),
        in_specs=[pl.BlockSpec((tm, HW), lambda i: (i, 0)), pl.BlockSpec((tm, HW), lambda i: (i, 0)),
                  pl.BlockSpec((tm, HW), lambda i: (i, 4)), pl.BlockSpec((1, HEAD_DIM), lambda i: (0, 0))],
        out_specs=pl.BlockSpec((tm, HW), lambda i: (i, 0)),
        out_shape=jax.ShapeDtypeStruct((n_rows, HW), BF16),
        compiler_params=_cparams(1, 32), name="hgrn_readout",
    )(o_f, o_b, z2, norm_g.reshape(1, HEAD_DIM))


def _merge_kernel(h_ref, oa_ref, od_ref, om_ref, or_ref, wg_ref, bg_ref, wb_ref, o_ref, wg_bf, wb_bf):
    @pl.when(pl.program_id(1) == 0)
    def _():
        wg_bf[...] = wg_ref[...].astype(BF16)
        wb_bf[...] = wb_ref[...].astype(BF16)

    h = h_ref[...]
    acc = None
    for j, o_j in enumerate((oa_ref, od_ref, om_ref, or_ref)):
        gate = jax.nn.sigmoid(_dot(h, wg_bf[j]) + bg_ref[j])
        term = gate * _dot(o_j[...], wb_bf[j])
        acc = term if acc is None else acc + term
    o_ref[...] = acc.astype(BF16)


def merge_branches(h, outs, w_gate, b_gate, w_branch, n_rows):
    D = h.shape[1]
    BW = w_branch.shape[1]
    tm, tn = ROW_TILE, 256
    nb = len(outs)
    single = dict(pipeline_mode=pl.Buffered(1))
    return pl.pallas_call(
        _merge_kernel, grid=(D // tn, n_rows // tm),
        in_specs=[pl.BlockSpec((tm, D), lambda j, i: (i, 0))]
                 + [pl.BlockSpec((tm, BW), lambda j, i: (i, 0))] * nb
                 + [pl.BlockSpec((nb, D, tn), lambda j, i: (0, 0, j), **single),
                    pl.BlockSpec((nb, 1, tn), lambda j, i: (0, 0, j)),
                    pl.BlockSpec((nb, BW, tn), lambda j, i: (0, 0, j), **single)],
        out_specs=pl.BlockSpec((tm, tn), lambda j, i: (i, j)),
        out_shape=jax.ShapeDtypeStruct((n_rows, D), BF16),
        scratch_shapes=[pltpu.VMEM((nb, D, tn), BF16), pltpu.VMEM((nb, BW, tn), BF16)],
        compiler_params=_cparams(2, 58), name="merge",
    )(h, *outs, w_gate, b_gate.reshape(nb, 1, D), w_branch)


def _ffn_in_kernel(x_ref, w1_ref, w3_ref, *rest, with_combine):
    if with_combine:
        c_ref, o_ref, w1_bf, w3_bf = rest
    else:
        o_ref, w1_bf, w3_bf = rest

    @pl.when(pl.program_id(1) == 0)
    def _():
        w1_bf[...] = w1_ref[...].astype(BF16)
        w3_bf[...] = w3_ref[...].astype(BF16)

    x = x_ref[...]
    u = _silu(_dot(x, w1_bf[...])) * _dot(x, w3_bf[...])
    if with_combine:
        c = c_ref[...]
        u = u * jnp.concatenate([c] * (u.shape[1] // c.shape[1]), axis=1)
    o_ref[...] = u.astype(BF16)


def ffn_in(h, w1, w3, n_rows, combine_lanes=None):
    E, D, Fd = w1.shape
    tm, tn = ROW_TILE, 256
    nbe = Fd // tn
    in_specs = [pl.BlockSpec((tm, D), lambda j, i: (i, 0)),
                pl.BlockSpec((None, D, tn), lambda j, i: (j // nbe, 0, j % nbe)),
                pl.BlockSpec((None, D, tn), lambda j, i: (j // nbe, 0, j % nbe))]
    args = [h, w1, w3]
    if combine_lanes is not None:
        in_specs.append(pl.BlockSpec((tm, 128), lambda j, i: (i, j // nbe)))
        args.append(combine_lanes)
    return pl.pallas_call(
        functools.partial(_ffn_in_kernel, with_combine=combine_lanes is not None),
        grid=(E * nbe, n_rows // tm),
        in_specs=in_specs,
        out_specs=pl.BlockSpec((tm, tn), lambda j, i: (i, j)),
        out_shape=jax.ShapeDtypeStruct((n_rows, E * Fd), BF16),
        scratch_shapes=[pltpu.VMEM((D, tn), BF16)] * 2,
        compiler_params=_cparams(2, 44), name="ffn_in",
    )(*args)


def _group_spec(tm, tn, n_lat, n_samples):
    return pl.BlockSpec((None, 1, tn), lambda j, i: (_grp_of_tile(i, tm, n_lat, n_samples), 0, j))


def ffn_out(u, w2, xres, gate, n_rows, n_lat, n_samples, k_splits=1):
    Ktot, D = w2.shape
    G = gate.shape[0]
    tm, tn = 256, 512
    Kh = Ktot // k_splits
    out = xres
    for kb in range(k_splits):
        out = _mm_call(u, w2, Kh, n_blocks=D // tn, tn=tn, tm=tm, out_dtype=F32, n_row_tiles=n_rows // tm,
                       x_col_block=kb, w_row_block=kb, epilogue=_residual_epilogue,
                       extras=(out, gate.reshape(G, 1, D)),
                       extra_specs=(pl.BlockSpec((tm, tn), lambda j, i: (i, j)), _group_spec(tm, tn, n_lat, n_samples)),
                       w_single_buffer=True, name="ffn_out")
    return out


def kernel(x, c, ctx, c_ctx, norm1_g, norm2_g, w_ada, b_ada, w_in, na_rpb, diff_lambda, diff_subln_g,
           mla_q_norm_g, mla_w_q_up, mla_kv_norm_g, mla_w_kv_up, hgrn_lower_bounds, hgrn_norm_g,
           w_branch, w_gate, b_gate, w_out, ffn_w1, ffn_w3, ffn_w2, moe_router, moe_w1, moe_w3, moe_w2,
           final_norm_g):
    B, N, D = x.shape
    NC = ctx.shape[1]
    L = w_ada.shape[0]
    n_lat, n_all = B * N, B * N + B * NC
    BW = w_branch.shape[2]
    n_heads = BW // HEAD_DIM
    HG0 = w_in.shape[2] - 5 * BW
    assert N % ROW_TILE == 0 and (B * NC) % ROW_TILE == 0 and NC % HGRN_CHUNK == 0

    xs = jnp.concatenate([x.reshape(n_lat, D), ctx.reshape(B * NC, D)], axis=0)

    cond8 = jnp.zeros((8, D), F32).at[:B].set(c).at[B].set(c_ctx)
    mods = adaln_all(cond8, w_ada, b_ada)[:, :B + 1].reshape(L, B + 1, 6, D)

    lb_all = jnp.cumsum(jax.nn.softmax(hgrn_lower_bounds.astype(F32), axis=0), axis=0)
    lb_all = lb_all - lb_all[0:1]

    cosd, sind = _rope_tables(N, B * NC, 2 * 64, ROW_TILE)
    cosm, sinm = _rope_tables(N, B * NC, 64, ROW_TILE)

    RQ, RKV = mla_w_q_up.shape[1], mla_w_kv_up.shape[1]
    wq4 = mla_w_q_up.reshape(L, RQ, n_heads, HEAD_DIM + MLA_ROPE)
    wq_perm = jnp.concatenate([wq4[..., :HEAD_DIM].reshape(L, RQ, -1), wq4[..., HEAD_DIM:].reshape(L, RQ, -1)], axis=-1)
    wkv4 = mla_w_kv_up.reshape(L, RKV, n_heads, 2 * HEAD_DIM)
    wkv_perm = jnp.concatenate([wkv4[..., :HEAD_DIM].reshape(L, RKV, -1), wkv4[..., HEAD_DIM:].reshape(L, RKV, -1)], axis=-1)

    for l in range(L):
        need_ctx = l < L - 1
        n_rows = n_all if need_ctx else n_lat
        lambda_init = 0.8 - 0.6 * math.exp(-0.3 * l)
        sh1, sc1, g1, sh2, sc2, g2 = (mods[l, :, k] for k in range(6))

        h = norm_mod(xs, norm1_g[l], sh1, sc1, N, B)
        z1 = _mm_call(h, w_in[l], D, n_blocks=15, tn=512, tm=ROW_TILE, out_dtype=BF16, name="w_in_attn")
        z2 = _mm_call(h, w_in[l][:, HG0:], D, n_blocks=5 * BW // 512, tn=512, tm=ROW_TILE, out_dtype=F32,
                      name="w_in_hgrn")

        bias = _na_bias_tables(na_rpb[l], N // GRID_W)
        o_a = na_attention(z1, bias, B, N, NC, n_heads)
        dq, dk = diff_rope(z1, cosd, sind, N, n_lat)
        lam_rows = jnp.zeros((8, HEAD_DIM), F32).at[:4].set(diff_lambda[l]).at[4].set(lambda_init)
        o_d = diff_attention(dq, dk, z1, lam_rows, diff_subln_g[l], lambda_init, B, N, NC, False)
        qn, qr, kn, v, kr = mla_prep(z1, cosm, sinm, mla_q_norm_g[l], wq_perm[l], mla_kv_norm_g[l], wkv_perm[l],
                                     N, n_lat)
        o_m = mla_attention(qn, qr, kn, v, kr, B, N, NC, False)
        o_f, o_b = hgrn_scan(z2, lb_all[l], B, N, NC)
        o_r = hgrn_readout(o_f, o_b, z2, hgrn_norm_g[l], n_rows)
        if need_ctx:
            o_a = jnp.concatenate([o_a, na_ctx_attention(z1, B, N, NC, n_heads)], axis=0)
            o_d = jnp.concatenate([o_d, diff_attention(dq, dk, z1, lam_rows, diff_subln_g[l], lambda_init,
                                                       B, N, NC, True)], axis=0)
            o_m = jnp.concatenate([o_m, mla_attention(qn, qr, kn, v, kr, B, N, NC, True)], axis=0)

        s = merge_branches(h, (o_a, o_d, o_m, o_r), w_gate[l], b_gate[l], w_branch[l], n_rows)
        G = B + 1
        xs = _mm_call(s, w_out[l], D, n_blocks=D // 512, tn=512, tm=ROW_TILE, out_dtype=F32,
                      n_row_tiles=n_rows // ROW_TILE, epilogue=_residual_epilogue,
                      extras=(xs, g1.reshape(G, 1, D)),
                      extra_specs=(pl.BlockSpec((ROW_TILE, 512), lambda j, i: (i, j)),
                                   _group_spec(ROW_TILE, 512, N, B)),
                      name="w_out")

        j = l // 2
        if l % 2 == 0:
            h2 = norm_mod(xs, norm2_g[l], sh2, sc2, N, B)
            u = ffn_in(h2, ffn_w1[j][None], ffn_w3[j][None], n_rows)
            xs = ffn_out(u, ffn_w2[j], xs, g2, n_rows, N, B)
        else:
            h2, comb = norm_mod(xs, norm2_g[l], sh2, sc2, N, B, w_router=moe_router[j])
            E = moe_w1.shape[1]
            comb_lanes = jnp.repeat(comb[:, :E], 128, axis=1)
            u = ffn_in(h2, moe_w1[j], moe_w3[j], n_rows, combine_lanes=comb_lanes)
            xs = ffn_out(u, moe_w2[j].reshape(-1, D), xs, g2, n_rows, N, B, k_splits=2)

    return final_norm(xs[:n_lat], final_norm_g).reshape(B, N, D)
```

```python
import functools
import math

import numpy as np
import jax
import jax.numpy as jnp
from jax import lax
from jax.experimental import pallas as pl
from jax.experimental.pallas import tpu as pltpu

F32 = jnp.float32
BF16 = jnp.bfloat16

GRID_W = 64
RMS_EPS = 1e-6
ROPE_BASE = 10000.0
NA_KH, NA_KW = 8, 16
HEAD_DIM = 128
MLA_ROPE = 64
HGRN_CHUNK = 64
MOE_EXPERTS = 8
NEG_BIG = -1e30

VMEM_LIMIT_V7X = 60000 * 1024
ROW_TILE = 512


def _cparams(n_axes, vmem_mb):
    return pltpu.CompilerParams(dimension_semantics=("arbitrary",) * n_axes,
                                vmem_limit_bytes=min(int(vmem_mb * 2**20), VMEM_LIMIT_V7X))


def _silu(x):
    return x * jax.nn.sigmoid(x)


def _dot(a, b):
    return jnp.dot(a, b, preferred_element_type=F32)


def _dot_nt(a, b):
    return lax.dot_general(a, b, (((1,), (1,)), ((), ())), preferred_element_type=F32)


def _dot_tn(a, b):
    return lax.dot_general(a, b, (((0,), (0,)), ((), ())), preferred_element_type=F32)


def _adaln_kernel(c_ref, w_ref, b_ref, o_ref):
    x = _silu(c_ref[...]).astype(BF16)
    o_ref[...] = _dot(x, w_ref[...].astype(BF16)) + b_ref[...]


def adaln_all(cond8, w_ada, b_ada):
    L, D, N6 = w_ada.shape
    tn = 512
    return pl.pallas_call(
        _adaln_kernel,
        grid=(L, N6 // tn),
        in_specs=[pl.BlockSpec((8, D), lambda l, j: (0, 0)),
                  pl.BlockSpec((None, D, tn), lambda l, j: (l, 0, j)),
                  pl.BlockSpec((None, 1, tn), lambda l, j: (l, 0, j))],
        out_specs=pl.BlockSpec((None, 8, tn), lambda l, j: (l, 0, j)),
        out_shape=jax.ShapeDtypeStruct((L, 8, N6), F32),
        compiler_params=_cparams(2, 40),
        name="adaln",
    )(cond8, w_ada, b_ada.reshape(L, 1, N6))


def _norm_mod_kernel(x_ref, g_ref, sh_ref, sc_ref, *rest, with_router):
    x = x_ref[...]
    y = x * lax.rsqrt(jnp.mean(x * x, axis=-1, keepdims=True) + RMS_EPS) * g_ref[...]
    h = y * (1.0 + sc_ref[...]) + sh_ref[...]
    if not with_router:
        (h_ref,) = rest
        h_ref[...] = h.astype(BF16)
        return
    wr_ref, h_ref, comb_ref = rest
    h_ref[...] = h.astype(BF16)
    logits = jnp.dot(h, wr_ref[...], precision=lax.Precision.HIGHEST, preferred_element_type=F32)
    lane = lax.broadcasted_iota(jnp.int32, logits.shape, 1).astype(F32)
    logits = jnp.where(lane < MOE_EXPERTS, logits, -jnp.inf)
    m1 = jnp.max(logits, axis=-1, keepdims=True)
    i1 = jnp.min(jnp.where(logits == m1, lane, 128.0), axis=-1, keepdims=True)
    rest_l = jnp.where(lane == i1, -jnp.inf, logits)
    m2 = jnp.max(rest_l, axis=-1, keepdims=True)
    i2 = jnp.min(jnp.where(rest_l == m2, lane, 128.0), axis=-1, keepdims=True)
    e2 = jnp.exp(m2 - m1)
    p1 = 1.0 / (1.0 + e2)
    p2 = e2 / (1.0 + e2)
    comb_ref[...] = jnp.where(lane == i1, p1, 0.0) + jnp.where(lane == i2, p2, 0.0)


def _grp_of_tile(i, tm, lat_rows_per_sample, n_samples):
    return jnp.minimum((i * tm) // lat_rows_per_sample, n_samples)


def norm_mod(x, g, shift, scale, n_lat, n_samples, w_router=None):
    M, D = x.shape
    tm = 256
    G = shift.shape[0]
    grp = lambda i: (_grp_of_tile(i, tm, n_lat, n_samples), 0, 0)
    in_specs = [pl.BlockSpec((tm, D), lambda i: (i, 0)),
                pl.BlockSpec((1, D), lambda i: (0, 0)),
                pl.BlockSpec((None, 1, D), grp),
                pl.BlockSpec((None, 1, D), grp)]
    args = [x, g.reshape(1, D), shift.reshape(G, 1, D), scale.reshape(G, 1, D)]
    out_specs = [pl.BlockSpec((tm, D), lambda i: (i, 0))]
    out_shape = [jax.ShapeDtypeStruct((M, D), BF16)]
    if w_router is not None:
        wr = jnp.zeros((D, 128), F32).at[:, :w_router.shape[1]].set(w_router)
        in_specs.append(pl.BlockSpec((D, 128), lambda i: (0, 0)))
        args.append(wr)
        out_specs.append(pl.BlockSpec((tm, 128), lambda i: (i, 0)))
        out_shape.append(jax.ShapeDtypeStruct((M, 128), F32))
    outs = pl.pallas_call(
        functools.partial(_norm_mod_kernel, with_router=w_router is not None),
        grid=(M // tm,),
        in_specs=in_specs, out_specs=out_specs, out_shape=out_shape,
        compiler_params=_cparams(1, 40),
        name="norm_mod_router" if w_router is not None else "norm_mod",
    )(*args)
    return outs if w_router is not None else outs[0]


def _final_norm_kernel(x_ref, g_ref, o_ref):
    x = x_ref[...]
    o_ref[...] = x * lax.rsqrt(jnp.mean(x * x, axis=-1, keepdims=True) + RMS_EPS) * g_ref[...]


def final_norm(x, g):
    M, D = x.shape
    tm = 256
    return pl.pallas_call(
        _final_norm_kernel, grid=(M // tm,),
        in_specs=[pl.BlockSpec((tm, D), lambda i: (i, 0)), pl.BlockSpec((1, D), lambda i: (0, 0))],
        out_specs=pl.BlockSpec((tm, D), lambda i: (i, 0)),
        out_shape=jax.ShapeDtypeStruct((M, D), F32),
        compiler_params=_cparams(1, 40), name="final_norm",
    )(x, g.reshape(1, D))


def _resident_mm_kernel(x_ref, w_ref, *rest, n_extra, epilogue):
    extras, o_ref, wbf_ref = rest[:n_extra], rest[n_extra], rest[n_extra + 1]

    @pl.when(pl.program_id(1) == 0)
    def _():
        wbf_ref[...] = w_ref[...].astype(BF16)

    acc = _dot(x_ref[...], wbf_ref[...])
    if epilogue is not None:
        acc = epilogue(acc, *[e[...] for e in extras])
    o_ref[...] = acc.astype(o_ref.dtype)


def _mm_call(x, w, K, *, n_blocks, tn, tm, out_dtype, col0_blocks=0, n_row_tiles=None, x_col_block=0,
             w_row_block=0, epilogue=None, extras=(), extra_specs=(), w_single_buffer=False, name="mm"):
    M = x.shape[0]
    n_row_tiles = M // tm if n_row_tiles is None else n_row_tiles
    w_kwargs = dict(pipeline_mode=pl.Buffered(1)) if w_single_buffer else {}
    in_specs = [pl.BlockSpec((tm, K), lambda j, i: (i, x_col_block)),
                pl.BlockSpec((K, tn), lambda j, i: (w_row_block, col0_blocks + j), **w_kwargs)]
    in_specs += list(extra_specs)
    out_bytes = jnp.dtype(out_dtype).itemsize
    vmem = (2 * tm * K * 2 + (1 if w_single_buffer else 2) * K * tn * 4 + K * tn * 2
            + 2 * tm * tn * out_bytes + 3 * tm * tn * 4 + len(extras) * 2 * tm * tn * 4) / 2**20 + 6
    return pl.pallas_call(
        functools.partial(_resident_mm_kernel, n_extra=len(extras), epilogue=epilogue),
        grid=(n_blocks, n_row_tiles),
        in_specs=in_specs,
        out_specs=pl.BlockSpec((tm, tn), lambda j, i: (i, j)),
        out_shape=jax.ShapeDtypeStruct((n_row_tiles * tm, n_blocks * tn), out_dtype),
        scratch_shapes=[pltpu.VMEM((K, tn), BF16)],
        compiler_params=_cparams(2, vmem),
        name=name,
    )(x, w, *extras)


def _residual_epilogue(acc, xres, gate):
    return xres + gate * acc


def _rope_tables(n_tokens, n_ident, group, tile_rows):
    half = group // 2
    q = half // 2
    pos = np.arange(n_tokens)
    row, col = pos // GRID_W, pos % GRID_W
    freqs = ROPE_BASE ** (-(np.arange(q, dtype=np.float64) / q))
    lane = np.arange(128)
    in_group = lane % group
    axis_pos = np.where((in_group < half)[None, :], row[:, None], col[:, None]).astype(np.float64)
    ang = axis_pos * freqs[(in_group % half) % q][None, :]
    sign = np.where((in_group % half) < q, -1.0, 1.0)[None, :]
    cos, sin = np.cos(ang), np.sin(ang) * sign
    n_pad = -(-n_ident // tile_rows) * tile_rows
    cos = np.concatenate([cos, np.ones((n_pad, 128))], axis=0)
    sin = np.concatenate([sin, np.zeros((n_pad, 128))], axis=0)
    return jnp.asarray(cos, F32), jnp.asarray(sin, F32)


def _rope_apply(x, cos, sin, q):
    lane = lax.broadcasted_iota(jnp.int32, x.shape, 1)
    partner = jnp.where((lane % (2 * q)) < q, pltpu.roll(x, 128 - q, 1), pltpu.roll(x, q, 1))
    return x * cos + partner * sin


def _diff_rope_kernel(q_ref, k_ref, cos_ref, sin_ref, qo_ref, ko_ref):
    cos, sin = cos_ref[...], sin_ref[...]
    for src, dst in ((q_ref, qo_ref), (k_ref, ko_ref)):
        for c in range(src.shape[1] // 128):
            sl = slice(c * 128, (c + 1) * 128)
            dst[:, sl] = _rope_apply(src[:, sl].astype(F32), cos, sin, 32).astype(BF16)


def _table_block(i, tm, n_tok, n_lat_total):
    return jnp.where(i * tm < n_lat_total, (i * tm % n_tok) // tm, n_tok // tm)


def diff_rope(z1, cosd, sind, n_tok, n_lat_total):
    M = z1.shape[0]
    tm = ROW_TILE
    W = 1024
    tb = lambda i: (_table_block(i, tm, n_tok, n_lat_total), 0)
    return pl.pallas_call(
        _diff_rope_kernel, grid=(M // tm,),
        in_specs=[pl.BlockSpec((tm, W), lambda i: (i, 3)), pl.BlockSpec((tm, W), lambda i: (i, 4)),
                  pl.BlockSpec((tm, 128), tb), pl.BlockSpec((tm, 128), tb)],
        out_specs=[pl.BlockSpec((tm, W), lambda i: (i, 0))] * 2,
        out_shape=[jax.ShapeDtypeStruct((M, W), BF16)] * 2,
        compiler_params=_cparams(1, 32), name="diff_rope",
    )(z1, z1, cosd, sind)


def _mla_prep_kernel(cq_ref, ckva_ref, ckvb_ref, kr_ref, cos_ref, sin_ref, gq_ref, gkv_ref, wq_ref, wkv_ref,
                     qn_ref, qr_ref, kn_ref, v_ref, kro_ref, wq_bf, wkv_bf):
    @pl.when(pl.program_id(0) == 0)
    def _():
        wq_bf[...] = wq_ref[...].astype(BF16)
        wkv_bf[...] = wkv_ref[...].astype(BF16)

    cos, sin = cos_ref[...], sin_ref[...]
    cq = cq_ref[...].astype(F32)
    cqn = (cq * lax.rsqrt(jnp.mean(cq * cq, axis=-1, keepdims=True) + RMS_EPS) * gq_ref[...]).astype(BF16)
    q = _dot(cqn, wq_bf[...])
    n_nope = qn_ref.shape[1]
    qn_ref[...] = q[:, :n_nope].astype(BF16)
    for c in range(qr_ref.shape[1] // 128):
        sl = slice(c * 128, (c + 1) * 128)
        qr_ref[:, sl] = _rope_apply(q[:, n_nope + c * 128:n_nope + (c + 1) * 128], cos, sin, 16).astype(BF16)
    a = ckva_ref[...].astype(F32)
    b = ckvb_ref[...].astype(F32)
    ha = a.shape[1]
    ms = (jnp.sum(a * a, axis=-1, keepdims=True) + jnp.sum(b * b, axis=-1, keepdims=True)) / (2 * ha)
    r = lax.rsqrt(ms + RMS_EPS)
    g = gkv_ref[...]
    an = (a * r * g[:, :ha]).astype(BF16)
    bn = (b * r * g[:, ha:]).astype(BF16)
    kv = _dot(an, wkv_bf[:ha, :]) + _dot(bn, wkv_bf[ha:, :])
    n_kn = kn_ref.shape[1]
    kn_ref[...] = kv[:, :n_kn].astype(BF16)
    v_ref[...] = kv[:, n_kn:].astype(BF16)
    kro_ref[...] = _rope_apply(kr_ref[...].astype(F32), cos, sin, 16).astype(BF16)


def mla_prep(z1, cosm, sinm, gq, wq, gkv, wkv, n_tok, n_lat_total):
    M = z1.shape[0]
    tm = ROW_TILE
    RQ, RKV = wq.shape[0], wkv.shape[0]
    NQ, NKV = wq.shape[1], wkv.shape[1]
    n_nope = NKV // 2
    tb = lambda i: (_table_block(i, tm, n_tok, n_lat_total), 0)
    const = lambda i: (0, 0)
    return pl.pallas_call(
        _mla_prep_kernel, grid=(M // tm,),
        in_specs=[pl.BlockSpec((tm, RQ), lambda i: (i, 6144 // RQ)),
                  pl.BlockSpec((tm, RKV // 2), lambda i: (i, 6912 // (RKV // 2))),
                  pl.BlockSpec((tm, RKV // 2), lambda i: (i, 6912 // (RKV // 2) + 1)),
                  pl.BlockSpec((tm, 128), lambda i: (i, 7424 // 128)),
                  pl.BlockSpec((tm, 128), tb), pl.BlockSpec((tm, 128), tb),
                  pl.BlockSpec((1, RQ), const), pl.BlockSpec((1, RKV), const),
                  pl.BlockSpec((RQ, NQ), const), pl.BlockSpec((RKV, NKV), const)],
        out_specs=[pl.BlockSpec((tm, n_nope), lambda i: (i, 0)),
                   pl.BlockSpec((tm, NQ - n_nope), lambda i: (i, 0)),
                   pl.BlockSpec((tm, n_nope), lambda i: (i, 0)),
                   pl.BlockSpec((tm, n_nope), lambda i: (i, 0)),
                   pl.BlockSpec((tm, 128), lambda i: (i, 0))],
        out_shape=[jax.ShapeDtypeStruct((M, n_nope), BF16), jax.ShapeDtypeStruct((M, NQ - n_nope), BF16),
                   jax.ShapeDtypeStruct((M, n_nope), BF16), jax.ShapeDtypeStruct((M, n_nope), BF16),
                   jax.ShapeDtypeStruct((M, 128), BF16)],
        scratch_shapes=[pltpu.VMEM((RQ, NQ), BF16), pltpu.VMEM((RKV, NKV), BF16)],
        compiler_params=_cparams(1, 48), name="mla_prep",
    )(z1, z1, z1, z1, cosm, sinm, gq.reshape(1, RQ), gkv.reshape(1, RKV), wq, wkv)


def _softmax_parts(s_list):
    m = functools.reduce(jnp.maximum, [jnp.max(s, axis=-1, keepdims=True) for s in s_list])
    p_list = [jnp.exp(s - m) for s in s_list]
    l = functools.reduce(lambda a, b: a + b, [jnp.sum(p, axis=-1, keepdims=True) for p in p_list])
    return p_list, l


def _mla_attn_kernel(qn_ref, qr_ref, *rest, with_latent, scale):
    if with_latent:
        knl_ref, vl_ref, krl_ref, knc_ref, vc_ref, krc_ref, o_ref = rest
    else:
        knc_ref, vc_ref, krc_ref, o_ref = rest
    for hh in range(2):
        dn = slice(hh * HEAD_DIM, (hh + 1) * HEAD_DIM)
        dr = slice(hh * MLA_ROPE, (hh + 1) * MLA_ROPE)
        qn, qr = qn_ref[:, dn], qr_ref[:, dr]
        s_list = [(_dot_nt(qn, knc_ref[:, dn]) + _dot_nt(qr, krc_ref[:, :MLA_ROPE])) * scale]
        v_list = [vc_ref[:, dn]]
        if with_latent:
            s_list.append((_dot_nt(qn, knl_ref[:, dn]) + _dot_nt(qr, krl_ref[:, :MLA_ROPE])) * scale)
            v_list.append(vl_ref[:, dn])
        p_list, l = _softmax_parts(s_list)
        inv = 1.0 / l
        o = functools.reduce(lambda a, b: a + b,
                             [_dot((p * inv).astype(BF16), v) for p, v in zip(p_list, v_list)])
        o_ref[:, dn] = o.astype(BF16)


def mla_attention(qn, qr, kn, v, kr, n_samples, n_tok, n_ctx, ctx_queries):
    n_lat_total = n_samples * n_tok
    tq = 256
    scale = (HEAD_DIM + MLA_ROPE) ** -0.5
    cblk = n_lat_total // n_ctx
    W = 2 * HEAD_DIM
    if ctx_queries:
        nq = n_ctx // tq
        qrow = lambda b, hp, t: cblk * (n_ctx // tq) + b * nq + t
        out_rows = n_samples * n_ctx
        orow = lambda b, hp, t: b * nq + t
    else:
        nq = n_tok // tq
        qrow = lambda b, hp, t: b * nq + t
        out_rows = n_lat_total
        orow = qrow
    in_specs = [pl.BlockSpec((tq, W), lambda b, hp, t: (qrow(b, hp, t), hp)),
                pl.BlockSpec((tq, 128), lambda b, hp, t: (qrow(b, hp, t), hp))]
    args = [qn, qr]
    if not ctx_queries:
        in_specs += [pl.BlockSpec((n_tok, W), lambda b, hp, t: (b, hp)),
                     pl.BlockSpec((n_tok, W), lambda b, hp, t: (b, hp)),
                     pl.BlockSpec((n_tok, 128), lambda b, hp, t: (b, 0))]
        args += [kn, v, kr]
    in_specs += [pl.BlockSpec((n_ctx, W), lambda b, hp, t: (cblk + b, hp)),
                 pl.BlockSpec((n_ctx, W), lambda b, hp, t: (cblk + b, hp)),
                 pl.BlockSpec((n_ctx, 128), lambda b, hp, t: (cblk + b, 0))]
    args += [kn, v, kr]
    return pl.pallas_call(
        functools.partial(_mla_attn_kernel, with_latent=not ctx_queries, scale=scale),
        grid=(n_samples, kn.shape[1] // W, nq),
        in_specs=in_specs,
        out_specs=pl.BlockSpec((tq, W), lambda b, hp, t: (orow(b, hp, t), hp)),
        out_shape=jax.ShapeDtypeStruct((out_rows, kn.shape[1]), BF16),
        compiler_params=_cparams(3, 56), name="mla_attn_ctx" if ctx_queries else "mla_attn",
    )(*args)


def _diff_attn_kernel(q_ref, *rest, with_latent, scale, out_scale):
    if with_latent:
        kl_ref, vl_ref, kc_ref, vc_ref, lam_ref, g_ref, o_ref = rest
    else:
        kc_ref, vc_ref, lam_ref, g_ref, o_ref = rest
    lp = lam_ref[...]
    lam = (jnp.exp(jnp.sum(lp[0:1] * lp[1:2], axis=-1, keepdims=True))
           - jnp.exp(jnp.sum(lp[2:3] * lp[3:4], axis=-1, keepdims=True)) + lam_ref[4:5, 0:1])
    d = HEAD_DIM
    k_refs = [kc_ref] + ([kl_ref] if with_latent else [])
    v_refs = [vc_ref] + ([vl_ref] if with_latent else [])
    probs = []
    for half in range(2):
        sl = slice(half * d, (half + 1) * d)
        q = q_ref[:, sl]
        p_list, l = _softmax_parts([_dot_nt(q, k[:, sl]) * scale for k in k_refs])
        inv = 1.0 / l
        probs.append([p * inv for p in p_list])
    o = functools.reduce(lambda a, b: a + b,
                         [_dot((p1 - lam * p2).astype(BF16), v[...]) for p1, p2, v in zip(probs[0], probs[1], v_refs)])
    y = o * lax.rsqrt(jnp.mean(o * o, axis=-1, keepdims=True) + RMS_EPS) * g_ref[...]
    o_ref[...] = (y * out_scale).astype(BF16)


def diff_attention(dq, dk, z1, lam_rows, subln_g, lambda_init, n_samples, n_tok, n_ctx, ctx_queries):
    n_lat_total = n_samples * n_tok
    tq = 256
    W = 2 * HEAD_DIM
    cblk = n_lat_total // n_ctx
    v_col0 = 5120 // W
    if ctx_queries:
        nq = n_ctx // tq
        qrow = lambda b, h, t: cblk * (n_ctx // tq) + b * nq + t
        out_rows = n_samples * n_ctx
        orow = lambda b, h, t: b * nq + t
    else:
        nq = n_tok // tq
        qrow = lambda b, h, t: b * nq + t
        out_rows = n_lat_total
        orow = qrow
    in_specs = [pl.BlockSpec((tq, W), lambda b, h, t: (qrow(b, h, t), h))]
    args = [dq]
    if not ctx_queries:
        in_specs += [pl.BlockSpec((n_tok, W), lambda b, h, t: (b, h)),
                     pl.BlockSpec((n_tok, W), lambda b, h, t: (b, v_col0 + h))]
        args += [dk, z1]
    in_specs += [pl.BlockSpec((n_ctx, W), lambda b, h, t: (cblk + b, h)),
                 pl.BlockSpec((n_ctx, W), lambda b, h, t: (cblk + b, v_col0 + h)),
                 pl.BlockSpec((8, HEAD_DIM), lambda b, h, t: (0, 0)),
                 pl.BlockSpec((1, W), lambda b, h, t: (0, 0))]
    args += [dk, z1, lam_rows, subln_g.reshape(1, W)]
    n_heads = dk.shape[1] // W
    return pl.pallas_call(
        functools.partial(_diff_attn_kernel, with_latent=not ctx_queries, scale=HEAD_DIM ** -0.5,
                          out_scale=1.0 - lambda_init),
        grid=(n_samples, n_heads, nq),
        in_specs=in_specs,
        out_specs=pl.BlockSpec((tq, W), lambda b, h, t: (orow(b, h, t), h)),
        out_shape=jax.ShapeDtypeStruct((out_rows, dk.shape[1]), BF16),
        compiler_params=_cparams(3, 56), name="diff_attn_ctx" if ctx_queries else "diff_attn",
    )(*args)


def _na_bias_tables(rpb, n_rows):
    W = GRID_W
    a = np.arange(8)[:, None, None, None]
    qc = np.arange(W)[None, :, None, None]
    kr = np.arange(16)[None, None, :, None]
    kc = np.arange(W)[None, None, None, :]
    cs = np.clip(qc - NA_KW // 2, 0, W - NA_KW)
    col_ok = (kc >= cs) & (kc < cs + NA_KW)
    dx = np.clip(kc - qc + NA_KW - 1, 0, 2 * NA_KW - 2)
    dys, oks = [], []
    for rs_l, dy in ((np.maximum(a - 4, 0), kr - a + 7), (a, kr - a + 3), (np.minimum(a + 4, 8), kr - a - 1)):
        row_ok = (kr >= rs_l) & (kr < rs_l + NA_KH)
        oks.append(np.broadcast_to(row_ok & col_ok, (8, W, 16, W)).reshape(8 * W, 16 * W))
        dys.append(np.broadcast_to(np.clip(dy, 0, 2 * NA_KH - 2), (8, 1, 16, 1)).reshape(8 * 16))
    ok = np.stack(oks)
    H = rpb.shape[0]
    by_dx = jnp.take(rpb.astype(F32), jnp.asarray(dx.reshape(W * W)), axis=2)
    by_dy = jnp.take(by_dx, jnp.asarray(np.concatenate(dys)), axis=1)
    bias = by_dy.reshape(H, 3, 8, 16, W, W).transpose(0, 1, 2, 4, 3, 5).reshape(H, 3, 8 * W, 16 * W)
    return jnp.where(jnp.asarray(ok)[None], bias, NEG_BIG)


def _na_attn_kernel(q_ref, kl_ref, vl_ref, kc_ref, vc_ref, bias_ref, o_ref, *, scale, n_rows):
    rb = pl.program_id(2)
    r0 = jnp.clip(rb * 8 - 4, 0, n_rows - 16)
    start = pl.multiple_of(r0 * GRID_W, 256)
    kw = kl_ref[pl.ds(start, 16 * GRID_W), :]
    vw = vl_ref[pl.ds(start, 16 * GRID_W), :]
    q = q_ref[...]
    s_lat = _dot_nt(q, kw) * scale + bias_ref[...]
    s_ctx = _dot_nt(q, kc_ref[...]) * scale
    (p_lat, p_ctx), l = _softmax_parts([s_lat, s_ctx])
    inv = 1.0 / l
    o = _dot((p_lat * inv).astype(BF16), vw) + _dot((p_ctx * inv).astype(BF16), vc_ref[...])
    o_ref[...] = o.astype(BF16)


def na_attention(z1, bias, n_samples, n_tok, n_ctx, n_heads):
    n_rows = n_tok // GRID_W
    assert n_rows >= 16 and n_rows % 8 == 0
    nrb = n_rows // 8
    tq = 8 * GRID_W
    d = HEAD_DIM
    cblk = n_samples * n_tok // n_ctx
    btype = lambda rb: jnp.where(rb == 0, 0, jnp.where(rb == nrb - 1, 2, 1))
    return pl.pallas_call(
        functools.partial(_na_attn_kernel, scale=d ** -0.5, n_rows=n_rows),
        grid=(n_samples, n_heads, nrb),
        in_specs=[pl.BlockSpec((tq, d), lambda b, h, rb: (b * nrb + rb, h)),
                  pl.BlockSpec((n_tok, d), lambda b, h, rb: (b, n_heads + h)),
                  pl.BlockSpec((n_tok, d), lambda b, h, rb: (b, 2 * n_heads + h)),
                  pl.BlockSpec((n_ctx, d), lambda b, h, rb: (cblk + b, n_heads + h)),
                  pl.BlockSpec((n_ctx, d), lambda b, h, rb: (cblk + b, 2 * n_heads + h)),
                  pl.BlockSpec((None, None, tq, 16 * GRID_W), lambda b, h, rb: (h, btype(rb), 0, 0))],
        out_specs=pl.BlockSpec((tq, d), lambda b, h, rb: (b * nrb + rb, h)),
        out_shape=jax.ShapeDtypeStruct((n_samples * n_tok, n_heads * d), BF16),
        compiler_params=_cparams(3, 40), name="na_attn",
    )(z1, z1, z1, z1, z1, bias)


def _ctx_attn_kernel(q_ref, k_ref, v_ref, o_ref, *, scale):
    (p,), l = _softmax_parts([_dot_nt(q_ref[...], k_ref[...]) * scale])
    o_ref[...] = _dot((p * (1.0 / l)).astype(BF16), v_ref[...]).astype(BF16)


def na_ctx_attention(z1, n_samples, n_tok, n_ctx, n_heads):
    d = HEAD_DIM
    cblk = n_samples * n_tok // n_ctx
    return pl.pallas_call(
        functools.partial(_ctx_attn_kernel, scale=d ** -0.5),
        grid=(n_samples, n_heads),
        in_specs=[pl.BlockSpec((n_ctx, d), lambda b, h: (cblk + b, h)),
                  pl.BlockSpec((n_ctx, d), lambda b, h: (cblk + b, n_heads + h)),
                  pl.BlockSpec((n_ctx, d), lambda b, h: (cblk + b, 2 * n_heads + h))],
        out_specs=pl.BlockSpec((n_ctx, d), lambda b, h: (b, h)),
        out_shape=jax.ShapeDtypeStruct((n_samples * n_ctx, n_heads * d), BF16),
        compiler_params=_cparams(2, 16), name="na_attn_ctx",
    )(z1, z1, z1)


def _hgrn_chunk(q_raw, f_raw, v, lb, st_ref, reverse):
    C, HW = q_raw.shape
    dk = HEAD_DIM
    n_heads = HW // dk
    q = _silu(q_raw) * (dk ** -0.5)
    f = lb + (1.0 - lb) * jax.nn.sigmoid(f_raw)
    g = jnp.log(f)
    k = 1.0 - f
    t_idx = lax.broadcasted_iota(jnp.int32, (C, 1), 0)
    p = (C - 1 - t_idx) if reverse else t_idx

    def prev(x, s):
        return pltpu.roll(x, (C - s) if reverse else s, 0)

    def nxt(x, s):
        return pltpu.roll(x, s if reverse else (C - s), 0)

    def incl_scan(b):
        x = g
        pb = jnp.bitwise_and(p, b - 1)
        s = 1
        while s < b:
            x = x + jnp.where(pb >= s, prev(x, s), 0.0)
            s *= 2
        return x

    def excl_rscan(b):
        if b == 1:
            return jnp.zeros_like(g)
        pb = jnp.bitwise_and(p, b - 1)
        x = jnp.where(pb <= b - 2, nxt(g, 1), 0.0)
        s = 1
        while s < b:
            x = x + jnp.where(pb + s <= b - 1, nxt(x, s), 0.0)
            s *= 2
        return x

    cum = incl_scan(C)
    q_in = (q * jnp.exp(cum)).astype(BF16)
    k_out = (k * jnp.exp(excl_rscan(C))).astype(BF16)
    total = jnp.exp(cum[0:1] if reverse else cum[C - 1:C])
    qb, kb, vb = q.astype(BF16), k.astype(BF16), v.astype(BF16)

    levels = []
    b = C // 2
    while b >= 1:
        upper = jnp.bitwise_and(p, b) != 0
        ql = jnp.where(upper, q * jnp.exp(incl_scan(b)), 0.0).astype(BF16)
        kl = jnp.where(upper, 0.0, k * jnp.exp(excl_rscan(b))).astype(BF16)
        levels.append((b, ql, kl))
        b //= 2

    s_idx = lax.broadcasted_iota(jnp.int32, (1, C), 1)
    ps = (C - 1 - s_idx) if reverse else s_idx
    outs = []
    for h in range(n_heads):
        sl = slice(h * dk, (h + 1) * dk)
        st_old = st_ref[h]
        att = jnp.where(p == ps, _dot_nt(qb[:, sl], kb[:, sl]), 0.0)
        for b, ql, kl in levels:
            same_pair = jnp.bitwise_and(p, -2 * b) == jnp.bitwise_and(ps, -2 * b)
            att = att + jnp.where(same_pair, _dot_nt(ql[:, sl], kl[:, sl]), 0.0)
        outs.append(_dot_nt(q_in[:, sl], st_old.astype(BF16)) + _dot(att.astype(BF16), vb[:, sl]))
        st_ref[h] = total[:, sl] * st_old + _dot_tn(vb[:, sl], k_out[:, sl])
    return jnp.concatenate(outs, axis=1)


def _hgrn_kernel(qf_ref, ff_ref, if_ref, qb_ref, fb_ref, ib_ref, lb_ref, of_ref, ob_ref, sf_ref, sb_ref):
    @pl.when(pl.program_id(1) == 0)
    def _():
        sf_ref[...] = jnp.zeros_like(sf_ref)
        sb_ref[...] = jnp.zeros_like(sb_ref)

    lb = lb_ref[...]
    of_ref[...] = _hgrn_chunk(qf_ref[...], ff_ref[...], if_ref[...], lb, sf_ref, reverse=False)
    ob_ref[...] = _hgrn_chunk(qb_ref[...], fb_ref[...], ib_ref[...], lb, sb_ref, reverse=True)


def hgrn_scan(z2, lb, n_samples, n_tok, n_ctx):
    M = z2.shape[0]
    HW = z2.shape[1] // 5
    C = HGRN_CHUNK
    ncc, ncl = n_ctx // C, n_tok // C
    lat_blocks = n_samples * ncl

    def fwd_row(b, c):
        return jnp.where(c < ncc, lat_blocks + b * ncc + c, b * ncl + (c - ncc))

    def bwd_row(b, c):
        return jnp.where(c < ncc, lat_blocks + b * ncc + (ncc - 1 - c), b * ncl + (ncl - 1 - (c - ncc)))

    spec = lambda rowf, col: pl.BlockSpec((C, HW), lambda b, c: (rowf(b, c), col))
    return pl.pallas_call(
        _hgrn_kernel, grid=(n_samples, ncc + ncl),
        in_specs=[spec(fwd_row, 0), spec(fwd_row, 1), spec(fwd_row, 3),
                  spec(bwd_row, 0), spec(bwd_row, 2), spec(bwd_row, 3),
                  pl.BlockSpec((1, HW), lambda b, c: (0, 0))],
        out_specs=[spec(fwd_row, 0), spec(bwd_row, 0)],
        out_shape=[jax.ShapeDtypeStruct((M, HW), F32)] * 2,
        scratch_shapes=[pltpu.VMEM((HW // HEAD_DIM, HEAD_DIM, HEAD_DIM), F32)] * 2,
        compiler_params=_cparams(2, 48), name="hgrn_scan",
    )(z2, z2, z2, z2, z2, z2, lb.reshape(1, HW))


def _hgrn_readout_kernel(of_ref, ob_ref, gz_ref, g_ref, o_ref):
    for h in range(of_ref.shape[1] // HEAD_DIM):
        sl = slice(h * HEAD_DIM, (h + 1) * HEAD_DIM)
        x = of_ref[:, sl] + ob_ref[:, sl]
        y = x * lax.rsqrt(jnp.mean(x * x, axis=-1, keepdims=True) + RMS_EPS) * g_ref[...]
        o_ref[:, sl] = (y * _silu(gz_ref[:, sl])).astype(BF16)


def hgrn_readout(o_f, o_b, z2, norm_g, n_rows):
    HW = o_f.shape[1]
    tm = ROW_TILE
    return pl.pallas_call(
        _hgrn_readout_kernel, grid=(n_rows // tm,),
        in_specs=[pl.BlockSpec((tm, HW), lambda i: (i, 0)), pl.BlockSpec((tm, HW), lambda i: (i, 0)),
                  pl.BlockSpec((tm, HW), lambda i: (i, 4)), pl.BlockSpec((1, HEAD_DIM), lambda i: (0, 0))],
        out_specs=pl.BlockSpec((tm, HW), lambda i: (i, 0)),
        out_shape=jax.ShapeDtypeStruct((n_rows, HW), BF16),
        compiler_params=_cparams(1, 32), name="hgrn_readout",
    )(o_f, o_b, z2, norm_g.reshape(1, HEAD_DIM))


def _merge_kernel(h_ref, oa_ref, od_ref, om_ref, or_ref, wg_ref, bg_ref, wb_ref, o_ref, wg_bf, wb_bf):
    @pl.when(pl.program_id(1) == 0)
    def _():
        wg_bf[...] = wg_ref[...].astype(BF16)
        wb_bf[...] = wb_ref[...].astype(BF16)

    h = h_ref[...]
    acc = None
    for j, o_j in enumerate((oa_ref, od_ref, om_ref, or_ref)):
        gate = jax.nn.sigmoid(_dot(h, wg_bf[j]) + bg_ref[j])
        term = gate * _dot(o_j[...], wb_bf[j])
        acc = term if acc is None else acc + term
    o_ref[...] = acc.astype(BF16)


def merge_branches(h, outs, w_gate, b_gate, w_branch, n_rows):
    D = h.shape[1]
    BW = w_branch.shape[1]
    tm, tn = ROW_TILE, 256
    nb = len(outs)
    single = dict(pipeline_mode=pl.Buffered(1))
    return pl.pallas_call(
        _merge_kernel, grid=(D // tn, n_rows // tm),
        in_specs=[pl.BlockSpec((tm, D), lambda j, i: (i, 0))]
                 + [pl.BlockSpec((tm, BW), lambda j, i: (i, 0))] * nb
                 + [pl.BlockSpec((nb, D, tn), lambda j, i: (0, 0, j), **single),
                    pl.BlockSpec((nb, 1, tn), lambda j, i: (0, 0, j)),
                    pl.BlockSpec((nb, BW, tn), lambda j, i: (0, 0, j), **single)],
        out_specs=pl.BlockSpec((tm, tn), lambda j, i: (i, j)),
        out_shape=jax.ShapeDtypeStruct((n_rows, D), BF16),
        scratch_shapes=[pltpu.VMEM((nb, D, tn), BF16), pltpu.VMEM((nb, BW, tn), BF16)],
        compiler_params=_cparams(2, 58), name="merge",
    )(h, *outs, w_gate, b_gate.reshape(nb, 1, D), w_branch)


def _ffn_in_kernel(x_ref, w1_ref, w3_ref, *rest, with_combine):
    if with_combine:
        c_ref, o_ref, w1_bf, w3_bf = rest
    else:
        o_ref, w1_bf, w3_bf = rest

    @pl.when(pl.program_id(1) == 0)
    def _():
        w1_bf[...] = w1_ref[...].astype(BF16)
        w3_bf[...] = w3_ref[...].astype(BF16)

    x = x_ref[...]
    u = _silu(_dot(x, w1_bf[...])) * _dot(x, w3_bf[...])
    if with_combine:
        c = c_ref[...]
        u = u * jnp.concatenate([c] * (u.shape[1] // c.shape[1]), axis=1)
    o_ref[...] = u.astype(BF16)


def ffn_in(h, w1, w3, n_rows, combine_lanes=None):
    E, D, Fd = w1.shape
    tm, tn = ROW_TILE, 256
    nbe = Fd // tn
    in_specs = [pl.BlockSpec((tm, D), lambda j, i: (i, 0)),
                pl.BlockSpec((None, D, tn), lambda j, i: (j // nbe, 0, j % nbe)),
                pl.BlockSpec((None, D, tn), lambda j, i: (j // nbe, 0, j % nbe))]
    args = [h, w1, w3]
    if combine_lanes is not None:
        in_specs.append(pl.BlockSpec((tm, 128), lambda j, i: (i, j // nbe)))
        args.append(combine_lanes)
    return pl.pallas_call(
        functools.partial(_ffn_in_kernel, with_combine=combine_lanes is not None),
        grid=(E * nbe, n_rows // tm),
        in_specs=in_specs,
        out_specs=pl.BlockSpec((tm, tn), lambda j, i: (i, j)),
        out_shape=jax.ShapeDtypeStruct((n_rows, E * Fd), BF16),
        scratch_shapes=[pltpu.VMEM((D, tn), BF16)] * 2,
        compiler_params=_cparams(2, 44), name="ffn_in",
    )(*args)


def _group_spec(tm, tn, n_lat, n_samples):
    return pl.BlockSpec((None, 1, tn), lambda j, i: (_grp_of_tile(i, tm, n_lat, n_samples), 0, j))


def ffn_out(u, w2, xres, gate, n_rows, n_lat, n_samples, k_splits=1):
    Ktot, D = w2.shape
    G = gate.shape[0]
    tm, tn = 256, 512
    Kh = Ktot // k_splits
    out = xres
    for kb in range(k_splits):
        out = _mm_call(u, w2, Kh, n_blocks=D // tn, tn=tn, tm=tm, out_dtype=F32, n_row_tiles=n_rows // tm,
                       x_col_block=kb, w_row_block=kb, epilogue=_residual_epilogue,
                       extras=(out, gate.reshape(G, 1, D)),
                       extra_specs=(pl.BlockSpec((tm, tn), lambda j, i: (i, j)), _group_spec(tm, tn, n_lat, n_samples)),
                       w_single_buffer=True, name="ffn_out")
    return out


def kernel(x, c, ctx, c_ctx, norm1_g, norm2_g, w_ada, b_ada, w_in, na_rpb, diff_lambda, diff_subln_g,
           mla_q_norm_g, mla_w_q_up, mla_kv_norm_g, mla_w_kv_up, hgrn_lower_bounds, hgrn_norm_g,
           w_branch, w_gate, b_gate, w_out, ffn_w1, ffn_w3, ffn_w2, moe_router, moe_w1, moe_w3, moe_w2,
           final_norm_g):
    B, N, D = x.shape
    NC = ctx.shape[1]
    L = w_ada.shape[0]
    n_lat, n_all = B * N, B * N + B * NC
    BW = w_branch.shape[2]
    n_heads = BW // HEAD_DIM
    HG0 = w_in.shape[2] - 5 * BW
    assert N % ROW_TILE == 0 and (B * NC) % ROW_TILE == 0 and NC % HGRN_CHUNK == 0

    xs = jnp.concatenate([x.reshape(n_lat, D), ctx.reshape(B * NC, D)], axis=0)

    cond8 = jnp.zeros((8, D), F32).at[:B].set(c).at[B].set(c_ctx)
    mods = adaln_all(cond8, w_ada, b_ada)[:, :B + 1].reshape(L, B + 1, 6, D)

    lb_all = jnp.cumsum(jax.nn.softmax(hgrn_lower_bounds.astype(F32), axis=0), axis=0)
    lb_all = lb_all - lb_all[0:1]

    cosd, sind = _rope_tables(N, B * NC, 2 * 64, ROW_TILE)
    cosm, sinm = _rope_tables(N, B * NC, 64, ROW_TILE)

    RQ, RKV = mla_w_q_up.shape[1], mla_w_kv_up.shape[1]
    wq4 = mla_w_q_up.reshape(L, RQ, n_heads, HEAD_DIM + MLA_ROPE)
    wq_perm = jnp.concatenate([wq4[..., :HEAD_DIM].reshape(L, RQ, -1), wq4[..., HEAD_DIM:].reshape(L, RQ, -1)], axis=-1)
    wkv4 = mla_w_kv_up.reshape(L, RKV, n_heads, 2 * HEAD_DIM)
    wkv_perm = jnp.concatenate([wkv4[..., :HEAD_DIM].reshape(L, RKV, -1), wkv4[..., HEAD_DIM:].reshape(L, RKV, -1)], axis=-1)

    for l in range(L):
        need_ctx = l < L - 1
        n_rows = n_all if need_ctx else n_lat
        lambda_init = 0.8 - 0.6 * math.exp(-0.3 * l)
        sh1, sc1, g1, sh2, sc2, g2 = (mods[l, :, k] for k in range(6))

        h = norm_mod(xs, norm1_g[l], sh1, sc1, N, B)
        z1 = _mm_call(h, w_in[l], D, n_blocks=15, tn=512, tm=ROW_TILE, out_dtype=BF16, name="w_in_attn")
        z2 = _mm_call(h, w_in[l][:, HG0:], D, n_blocks=5 * BW // 512, tn=512, tm=ROW_TILE, out_dtype=F32,
                      name="w_in_hgrn")

        bias = _na_bias_tables(na_rpb[l], N // GRID_W)
        o_a = na_attention(z1, bias, B, N, NC, n_heads)
        dq, dk = diff_rope(z1, cosd, sind, N, n_lat)
        lam_rows = jnp.zeros((8, HEAD_DIM), F32).at[:4].set(diff_lambda[l]).at[4].set(lambda_init)
        o_d = diff_attention(dq, dk, z1, lam_rows, diff_subln_g[l], lambda_init, B, N, NC, False)
        qn, qr, kn, v, kr = mla_prep(z1, cosm, sinm, mla_q_norm_g[l], wq_perm[l], mla_kv_norm_g[l], wkv_perm[l],
                                     N, n_lat)
        o_m = mla_attention(qn, qr, kn, v, kr, B, N, NC, False)
        o_f, o_b = hgrn_scan(z2, lb_all[l], B, N, NC)
        o_r = hgrn_readout(o_f, o_b, z2, hgrn_norm_g[l], n_rows)
        if need_ctx:
            o_a = jnp.concatenate([o_a, na_ctx_attention(z1, B, N, NC, n_heads)], axis=0)
            o_d = jnp.concatenate([o_d, diff_attention(dq, dk, z1, lam_rows, diff_subln_g[l], lambda_init,
                                                       B, N, NC, True)], axis=0)
            o_m = jnp.concatenate([o_m, mla_attention(qn, qr, kn, v, kr, B, N, NC, True)], axis=0)

        s = merge_branches(h, (o_a, o_d, o_m, o_r), w_gate[l], b_gate[l], w_branch[l], n_rows)
        G = B + 1
        xs = _mm_call(s, w_out[l], D, n_blocks=D // 512, tn=512, tm=ROW_TILE, out_dtype=F32,
                      n_row_tiles=n_rows // ROW_TILE, epilogue=_residual_epilogue,
                      extras=(xs, g1.reshape(G, 1, D)),
                      extra_specs=(pl.BlockSpec((ROW_TILE, 512), lambda j, i: (i, j)),
                                   _group_spec(ROW_TILE, 512, N, B)),
                      name="w_out")

        j = l // 2
        if l % 2 == 0:
            h2 = norm_mod(xs, norm2_g[l], sh2, sc2, N, B)
            u = ffn_in(h2, ffn_w1[j][None], ffn_w3[j][None], n_rows)
            xs = ffn_out(u, ffn_w2[j], xs, g2, n_rows, N, B)
        else:
            h2, comb = norm_mod(xs, norm2_g[l], sh2, sc2, N, B, w_router=moe_router[j])
            E = moe_w1.shape[1]
            comb_lanes = jnp.repeat(comb[:, :E], 128, axis=1)
            u = ffn_in(h2, moe_w1[j], moe_w3[j], n_rows, combine_lanes=comb_lanes)
            xs = ffn_out(u, moe_w2[j].reshape(-1, D), xs, g2, n_rows, N, B, k_splits=2)

    return final_norm(xs[:n_lat], final_norm_g).reshape(B, N, D)
```

```python
import functools
import math

import numpy as np
import jax
import jax.numpy as jnp
from jax import lax
from jax.experimental import pallas as pl
from jax.experimental.pallas import tpu as pltpu

F32 = jnp.float32
BF16 = jnp.bfloat16

GRID_W = 64
RMS_EPS = 1e-6
ROPE_BASE = 10000.0
NA_KH, NA_KW = 8, 16
HEAD_DIM = 128
MLA_ROPE = 64
HGRN_CHUNK = 64
MOE_EXPERTS = 8
NEG_BIG = -1e30

VMEM_LIMIT_V7X = 60000 * 1024
ROW_TILE = 512


def _cparams(n_axes, vmem_mb):
    return pltpu.CompilerParams(dimension_semantics=("arbitrary",) * n_axes,
                                vmem_limit_bytes=min(int(vmem_mb * 2**20), VMEM_LIMIT_V7X))


def _silu(x):
    return x * jax.nn.sigmoid(x)


def _dot(a, b):
    return jnp.dot(a, b, preferred_element_type=F32)


def _dot_nt(a, b):
    return lax.dot_general(a, b, (((1,), (1,)), ((), ())), preferred_element_type=F32)


def _dot_tn(a, b):
    return lax.dot_general(a, b, (((0,), (0,)), ((), ())), preferred_element_type=F32)


def _adaln_kernel(c_ref, w_ref, b_ref, o_ref):
    @pl.when(pl.program_id(1) == 0)
    def _():
        o_ref[...] = jnp.broadcast_to(b_ref[...], o_ref.shape)

    x = _silu(c_ref[...]).astype(BF16)
    o_ref[...] += _dot(x, w_ref[...].astype(BF16))


def adaln_all(cond8, w_ada, b_ada):
    L, D, N6 = w_ada.shape
    tk = 128
    cond_k = cond8.reshape(8, D // tk, tk).transpose(1, 0, 2)
    return pl.pallas_call(
        _adaln_kernel,
        grid=(L, D // tk),
        in_specs=[pl.BlockSpec((None, 8, tk), lambda l, k: (k, 0, 0)),
                  pl.BlockSpec((None, tk, N6), lambda l, k: (l, k, 0)),
                  pl.BlockSpec((None, 1, N6), lambda l, k: (l, 0, 0))],
        out_specs=pl.BlockSpec((None, 8, N6), lambda l, k: (l, 0, 0)),
        out_shape=jax.ShapeDtypeStruct((L, 8, N6), F32),
        compiler_params=_cparams(2, 48),
        name="adaln",
    )(cond_k, w_ada, b_ada.reshape(L, 1, N6))


def _norm_mod_kernel(x_ref, g_ref, sh_ref, sc_ref, *rest, with_router):
    x = x_ref[...]
    y = x * lax.rsqrt(jnp.mean(x * x, axis=-1, keepdims=True) + RMS_EPS) * g_ref[...]
    h = y * (1.0 + sc_ref[...]) + sh_ref[...]
    if not with_router:
        (h_ref,) = rest
        h_ref[...] = h.astype(BF16)
        return
    wr_ref, h_ref, comb_ref = rest
    h_ref[...] = h
    logits = jnp.dot(h, wr_ref[...], precision=lax.Precision.HIGHEST, preferred_element_type=F32)
    lane = lax.broadcasted_iota(jnp.int32, logits.shape, 1).astype(F32)
    logits = jnp.where(lane < MOE_EXPERTS, logits, -jnp.inf)
    m1 = jnp.max(logits, axis=-1, keepdims=True)
    i1 = jnp.min(jnp.where(logits == m1, lane, 128.0), axis=-1, keepdims=True)
    rest_l = jnp.where(lane == i1, -jnp.inf, logits)
    m2 = jnp.max(rest_l, axis=-1, keepdims=True)
    i2 = jnp.min(jnp.where(rest_l == m2, lane, 128.0), axis=-1, keepdims=True)
    e2 = jnp.exp(m2 - m1)
    p1 = 1.0 / (1.0 + e2)
    p2 = e2 / (1.0 + e2)
    comb_ref[...] = jnp.where(lane == i1, p1, 0.0) + jnp.where(lane == i2, p2, 0.0)


def _grp_of_tile(i, tm, lat_rows_per_sample, n_samples):
    return jnp.minimum((i * tm) // lat_rows_per_sample, n_samples)


def norm_mod(x, g, shift, scale, n_lat, n_samples, w_router=None):
    M, D = x.shape
    tm = 256
    G = shift.shape[0]
    grp = lambda i: (_grp_of_tile(i, tm, n_lat, n_samples), 0, 0)
    in_specs = [pl.BlockSpec((tm, D), lambda i: (i, 0)),
                pl.BlockSpec((1, D), lambda i: (0, 0)),
                pl.BlockSpec((None, 1, D), grp),
                pl.BlockSpec((None, 1, D), grp)]
    args = [x, g.reshape(1, D), shift.reshape(G, 1, D), scale.reshape(G, 1, D)]
    out_specs = [pl.BlockSpec((tm, D), lambda i: (i, 0))]
    out_shape = [jax.ShapeDtypeStruct((M, D), BF16 if w_router is None else F32)]
    if w_router is not None:
        wr = jnp.zeros((D, 128), F32).at[:, :w_router.shape[1]].set(w_router)
        in_specs.append(pl.BlockSpec((D, 128), lambda i: (0, 0)))
        args.append(wr)
        out_specs.append(pl.BlockSpec((tm, 128), lambda i: (i, 0)))
        out_shape.append(jax.ShapeDtypeStruct((M, 128), F32))
    outs = pl.pallas_call(
        functools.partial(_norm_mod_kernel, with_router=w_router is not None),
        grid=(M // tm,),
        in_specs=in_specs, out_specs=out_specs, out_shape=out_shape,
        compiler_params=_cparams(1, 40),
        name="norm_mod_router" if w_router is not None else "norm_mod",
    )(*args)
    return outs if w_router is not None else outs[0]


def _final_norm_kernel(x_ref, g_ref, o_ref):
    x = x_ref[...]
    o_ref[...] = x * lax.rsqrt(jnp.mean(x * x, axis=-1, keepdims=True) + RMS_EPS) * g_ref[...]


def final_norm(x, g):
    M, D = x.shape
    tm = 256
    return pl.pallas_call(
        _final_norm_kernel, grid=(M // tm,),
        in_specs=[pl.BlockSpec((tm, D), lambda i: (i, 0)), pl.BlockSpec((1, D), lambda i: (0, 0))],
        out_specs=pl.BlockSpec((tm, D), lambda i: (i, 0)),
        out_shape=jax.ShapeDtypeStruct((M, D), F32),
        compiler_params=_cparams(1, 40), name="final_norm",
    )(x, g.reshape(1, D))


def _resident_mm_kernel(x_ref, w_ref, *rest, n_extra, epilogue):
    extras, o_ref, wbf_ref = rest[:n_extra], rest[n_extra], rest[n_extra + 1]

    @pl.when(pl.program_id(1) == 0)
    def _():
        wbf_ref[...] = w_ref[...].astype(BF16)

    acc = _dot(x_ref[...], wbf_ref[...])
    if epilogue is not None:
        acc = epilogue(acc, *[e[...] for e in extras])
    o_ref[...] = acc.astype(o_ref.dtype)


def _mm_call(x, w, layer, K, *, n_rows, n_blocks, tn, tm, out_dtype, x_col_block=0, w_row_block=0,
             epilogue=None, extras=(), extra_specs=(), w_single_buffer=False, name="mm"):
    w_kwargs = dict(pipeline_mode=pl.Buffered(1)) if w_single_buffer else {}
    in_specs = [pl.BlockSpec((tm, K), lambda j, i: (i, x_col_block)),
                pl.BlockSpec((None, K, tn), lambda j, i: (layer, w_row_block, j), **w_kwargs)]
    in_specs += list(extra_specs)
    out_bytes = jnp.dtype(out_dtype).itemsize
    vmem = (2 * tm * K * 2 + (1 if w_single_buffer else 2) * K * tn * 4 + K * tn * 2
            + 2 * tm * tn * out_bytes + 3 * tm * tn * 4 + len(extras) * 2 * tm * tn * 4) / 2**20 + 6
    return pl.pallas_call(
        functools.partial(_resident_mm_kernel, n_extra=len(extras), epilogue=epilogue),
        grid=(n_blocks, pl.cdiv(n_rows, tm)),
        in_specs=in_specs,
        out_specs=pl.BlockSpec((tm, tn), lambda j, i: (i, j)),
        out_shape=jax.ShapeDtypeStruct((n_rows, n_blocks * tn), out_dtype),
        scratch_shapes=[pltpu.VMEM((K, tn), BF16)],
        compiler_params=_cparams(2, vmem),
        name=name,
    )(x, w, *extras)


def _residual_epilogue(acc, xres, gate):
    return xres + gate * acc


def _rope_tables(n_tokens, n_ident, group, tile_rows):
    half = group // 2
    q = half // 2
    pos = np.arange(n_tokens)
    row, col = pos // GRID_W, pos % GRID_W
    freqs = ROPE_BASE ** (-(np.arange(q, dtype=np.float64) / q))
    lane = np.arange(128)
    in_group = lane % group
    axis_pos = np.where((in_group < half)[None, :], row[:, None], col[:, None]).astype(np.float64)
    ang = axis_pos * freqs[(in_group % half) % q][None, :]
    sign = np.where((in_group % half) < q, -1.0, 1.0)[None, :]
    cos, sin = np.cos(ang), np.sin(ang) * sign
    n_pad = -(-n_ident // tile_rows) * tile_rows
    cos = np.concatenate([cos, np.ones((n_pad, 128))], axis=0)
    sin = np.concatenate([sin, np.zeros((n_pad, 128))], axis=0)
    return jnp.asarray(cos, F32), jnp.asarray(sin, F32)


def _rope_apply(x, cos, sin, q):
    lane = lax.broadcasted_iota(jnp.int32, x.shape, 1)
    partner = jnp.where((lane % (2 * q)) < q, pltpu.roll(x, 128 - q, 1), pltpu.roll(x, q, 1))
    return x * cos + partner * sin


LOG2E = math.log2(math.e)


def _diff_rope_kernel(q_ref, k_ref, cos_ref, sin_ref, qo_ref, ko_ref, *, q_scale):
    cos, sin = cos_ref[...], sin_ref[...]
    for src, dst, mult in ((q_ref, qo_ref, q_scale), (k_ref, ko_ref, None)):
        for c in range(src.shape[1] // 128):
            sl = slice(c * 128, (c + 1) * 128)
            y = _rope_apply(src[:, sl].astype(F32), cos, sin, 32)
            dst[:, sl] = (y if mult is None else y * mult).astype(BF16)


def _table_block(i, tm, n_tok, n_lat_total):
    return jnp.where(i * tm < n_lat_total, (i * tm % n_tok) // tm, n_tok // tm)


def diff_rope(z1, cosd, sind, n_tok, n_lat_total):
    M = z1.shape[0]
    tm = ROW_TILE
    W = 1024
    tb = lambda i: (_table_block(i, tm, n_tok, n_lat_total), 0)
    return pl.pallas_call(
        functools.partial(_diff_rope_kernel, q_scale=HEAD_DIM ** -0.5 * LOG2E), grid=(M // tm,),
        in_specs=[pl.BlockSpec((tm, W), lambda i: (i, 3)), pl.BlockSpec((tm, W), lambda i: (i, 4)),
                  pl.BlockSpec((tm, 128), tb), pl.BlockSpec((tm, 128), tb)],
        out_specs=[pl.BlockSpec((tm, W), lambda i: (i, 0))] * 2,
        out_shape=[jax.ShapeDtypeStruct((M, W), BF16)] * 2,
        compiler_params=_cparams(1, 32), name="diff_rope",
    )(z1, z1, cosd, sind)


MLA_SLOT = 2 * HEAD_DIM


def _mla_prep_kernel(cq_ref, ckva_ref, ckvb_ref, kr_ref, cos_ref, sin_ref, gq_ref, gkv_ref, wq_ref, wkn_ref,
                     wvt_ref, qcat_ref, kcat_ref, vt_ref, wq_bf, wkn_bf, wvt_bf, *, q_scale):
    @pl.when(pl.program_id(0) == 0)
    def _():
        wq_bf[...] = wq_ref[...].astype(BF16)
        wkn_bf[...] = wkn_ref[...].astype(BF16)
        wvt_bf[...] = wvt_ref[...].astype(BF16)

    cos, sin = cos_ref[...], sin_ref[...]
    d = HEAD_DIM
    n_heads = qcat_ref.shape[1] // MLA_SLOT
    cq = cq_ref[...].astype(F32)
    cqn = (cq * lax.rsqrt(jnp.mean(cq * cq, axis=-1, keepdims=True) + RMS_EPS) * gq_ref[...]).astype(BF16)
    q = _dot(cqn, wq_bf[...])
    a = ckva_ref[...].astype(F32)
    b = ckvb_ref[...].astype(F32)
    ha = a.shape[1]
    ms = (jnp.sum(a * a, axis=-1, keepdims=True) + jnp.sum(b * b, axis=-1, keepdims=True)) / (2 * ha)
    r = lax.rsqrt(ms + RMS_EPS)
    g = gkv_ref[...]
    ckvn = jnp.concatenate([(a * r * g[:, :ha]).astype(BF16), (b * r * g[:, ha:]).astype(BF16)], axis=1)
    kn = _dot(ckvn, wkn_bf[...])
    lane = lax.broadcasted_iota(jnp.int32, cos.shape, 1)
    kr = jnp.where(lane < MLA_ROPE, _rope_apply(kr_ref[...].astype(F32), cos, sin, 16), 0.0).astype(BF16)
    for h in range(n_heads):
        c0 = h * MLA_SLOT
        qcat_ref[:, c0:c0 + d] = (q[:, c0:c0 + d] * q_scale).astype(BF16)
        qcat_ref[:, c0 + d:c0 + 2 * d] = (_rope_apply(q[:, c0 + d:c0 + 2 * d], cos, sin, 16) * q_scale).astype(BF16)
        kcat_ref[:, c0:c0 + d] = kn[:, h * d:(h + 1) * d].astype(BF16)
        kcat_ref[:, c0 + d:c0 + 2 * d] = kr
    vt_ref[...] = _dot_nt(wvt_bf[...], ckvn).astype(BF16)


def mla_prep(z1, cosm, sinm, gq, wq_cat, gkv, wkn, wvt, n_tok, n_lat_total):
    M = z1.shape[0]
    tm = ROW_TILE
    RQ, NQ = wq_cat.shape
    RKV, NKN = wkn.shape
    tb = lambda i: (_table_block(i, tm, n_tok, n_lat_total), 0)
    const = lambda i: (0, 0)
    return pl.pallas_call(
        functools.partial(_mla_prep_kernel, q_scale=(HEAD_DIM + MLA_ROPE) ** -0.5 * LOG2E), grid=(M // tm,),
        in_specs=[pl.BlockSpec((tm, RQ), lambda i: (i, 6144 // RQ)),
                  pl.BlockSpec((tm, RKV // 2), lambda i: (i, 6912 // (RKV // 2))),
                  pl.BlockSpec((tm, RKV // 2), lambda i: (i, 6912 // (RKV // 2) + 1)),
                  pl.BlockSpec((tm, 128), lambda i: (i, 7424 // 128)),
                  pl.BlockSpec((tm, 128), tb), pl.BlockSpec((tm, 128), tb),
                  pl.BlockSpec((1, RQ), const), pl.BlockSpec((1, RKV), const),
                  pl.BlockSpec((RQ, NQ), const), pl.BlockSpec((RKV, NKN), const), pl.BlockSpec((NKN, RKV), const)],
        out_specs=[pl.BlockSpec((tm, NQ), lambda i: (i, 0)),
                   pl.BlockSpec((tm, NQ), lambda i: (i, 0)),
                   pl.BlockSpec((NKN, tm), lambda i: (0, i))],
        out_shape=[jax.ShapeDtypeStruct((M, NQ), BF16), jax.ShapeDtypeStruct((M, NQ), BF16),
                   jax.ShapeDtypeStruct((NKN, M), BF16)],
        scratch_shapes=[pltpu.VMEM((RQ, NQ), BF16), pltpu.VMEM((RKV, NKN), BF16), pltpu.VMEM((NKN, RKV), BF16)],
        compiler_params=_cparams(1, 56), name="mla_prep",
    )(z1, z1, z1, z1, cosm, sinm, gq.reshape(1, RQ), gkv.reshape(1, RKV), wq_cat, wkn, wvt)


def _softmax_parts(s_list):
    m = functools.reduce(jnp.maximum, [jnp.max(s, axis=-1, keepdims=True) for s in s_list])
    p_list = [jnp.exp(s - m) for s in s_list]
    l = functools.reduce(lambda a, b: a + b, [jnp.sum(p, axis=-1, keepdims=True) for p in p_list])
    return p_list, l


def _softmax_parts_t(st_list):
    m = functools.reduce(jnp.maximum, [jnp.max(s, axis=0, keepdims=True) for s in st_list])
    p_list = [jnp.exp2(s - m) for s in st_list]
    l = functools.reduce(lambda a, b: a + b, [jnp.sum(p, axis=0, keepdims=True) for p in p_list])
    return p_list, l


def _query_rows(n_samples, n_tok, n_ctx, tq, ctx_queries):
    if ctx_queries:
        nq = n_ctx // tq
        first = n_samples * n_tok // tq
        return nq, (lambda b, t: first + b * nq + t), n_samples * n_ctx, (lambda b, t: b * nq + t)
    nq = n_tok // tq
    qrow = lambda b, t: b * nq + t
    return nq, qrow, n_samples * n_tok, qrow


def _mla_attn_kernel(q_ref, *rest, with_latent):
    if with_latent:
        kl_ref, vtl_ref, kc_ref, vtc_ref, o_ref = rest
    else:
        kc_ref, vtc_ref, o_ref = rest
    for hh in range(q_ref.shape[1] // MLA_SLOT):
        ds = slice(hh * MLA_SLOT, (hh + 1) * MLA_SLOT)
        dv = slice(hh * HEAD_DIM, (hh + 1) * HEAD_DIM)
        q = q_ref[:, ds]
        k_list = [kc_ref[:, ds]] + ([kl_ref[:, ds]] if with_latent else [])
        vt_list = [vtc_ref[dv, :]] + ([vtl_ref[dv, :]] if with_latent else [])
        p_list, l = _softmax_parts_t([_dot_nt(k, q) for k in k_list])
        ot = functools.reduce(lambda a, b: a + b, [_dot(vt, p.astype(BF16)) for vt, p in zip(vt_list, p_list)])
        o_ref[:, dv] = jnp.transpose(ot * (1.0 / l)).astype(BF16)


def mla_attention(qcat, kcat, vt, n_samples, n_tok, n_ctx, ctx_queries):
    tq = 256
    HP = 2
    nq, qrow, out_rows, orow = _query_rows(n_samples, n_tok, n_ctx, tq, ctx_queries)
    cblk = n_samples * n_tok // n_ctx
    WS, WV = HP * MLA_SLOT, HP * HEAD_DIM
    in_specs = [pl.BlockSpec((tq, WS), lambda b, hp, t: (qrow(b, t), hp))]
    args = [qcat]
    if not ctx_queries:
        in_specs += [pl.BlockSpec((n_tok, WS), lambda b, hp, t: (b, hp)),
                     pl.BlockSpec((WV, n_tok), lambda b, hp, t: (hp, b))]
        args += [kcat, vt]
    in_specs += [pl.BlockSpec((n_ctx, WS), lambda b, hp, t: (cblk + b, hp)),
                 pl.BlockSpec((WV, n_ctx), lambda b, hp, t: (hp, cblk + b))]
    args += [kcat, vt]
    return pl.pallas_call(
        functools.partial(_mla_attn_kernel, with_latent=not ctx_queries),
        grid=(n_samples, vt.shape[0] // WV, nq),
        in_specs=in_specs,
        out_specs=pl.BlockSpec((tq, WV), lambda b, hp, t: (orow(b, t), hp)),
        out_shape=jax.ShapeDtypeStruct((out_rows, vt.shape[0]), BF16),
        compiler_params=_cparams(3, 56), name="mla_attn_ctx" if ctx_queries else "mla_attn",
    )(*args)


def _diff_attn_kernel(q_ref, *rest, with_latent, out_scale):
    if with_latent:
        kl_ref, vtl_ref, kc_ref, vtc_ref, lam_ref, g_ref, o_ref = rest
    else:
        kc_ref, vtc_ref, lam_ref, g_ref, o_ref = rest
    lp = lam_ref[...]
    lam = (jnp.exp(jnp.sum(lp[0:1] * lp[1:2], axis=-1, keepdims=True))
           - jnp.exp(jnp.sum(lp[2:3] * lp[3:4], axis=-1, keepdims=True)) + lam_ref[4:5, 0:1])
    d = HEAD_DIM
    k_refs = [kc_ref] + ([kl_ref] if with_latent else [])
    vt_refs = [vtc_ref] + ([vtl_ref] if with_latent else [])
    parts = []
    for half in range(2):
        sl = slice(half * d, (half + 1) * d)
        q = q_ref[:, sl]
        p_list, l = _softmax_parts_t([_dot_nt(k[:, sl], q) for k in k_refs])
        parts.append((p_list, 1.0 / l))
    (p1_list, inv1), (p2_list, inv2) = parts
    c2 = lam * inv2
    ot = functools.reduce(lambda a, b: a + b,
                          [_dot(vt[...], (p1 * inv1 - p2 * c2).astype(BF16))
                           for p1, p2, vt in zip(p1_list, p2_list, vt_refs)])
    o = jnp.transpose(ot)
    y = o * lax.rsqrt(jnp.mean(o * o, axis=-1, keepdims=True) + RMS_EPS) * g_ref[...]
    o_ref[...] = (y * out_scale).astype(BF16)


def diff_attention(dq, dk, vt, lam_rows, subln_g, lambda_init, n_samples, n_tok, n_ctx, ctx_queries):
    tq = 256
    W = 2 * HEAD_DIM
    nq, qrow, out_rows, orow = _query_rows(n_samples, n_tok, n_ctx, tq, ctx_queries)
    cblk = n_samples * n_tok // n_ctx
    in_specs = [pl.BlockSpec((tq, W), lambda b, h, t: (qrow(b, t), h))]
    args = [dq]
    if not ctx_queries:
        in_specs += [pl.BlockSpec((n_tok, W), lambda b, h, t: (b, h)),
                     pl.BlockSpec((W, n_tok), lambda b, h, t: (h, b))]
        args += [dk, vt]
    in_specs += [pl.BlockSpec((n_ctx, W), lambda b, h, t: (cblk + b, h)),
                 pl.BlockSpec((W, n_ctx), lambda b, h, t: (h, cblk + b)),
                 pl.BlockSpec((8, HEAD_DIM), lambda b, h, t: (0, 0)),
                 pl.BlockSpec((1, W), lambda b, h, t: (0, 0))]
    args += [dk, vt, lam_rows, subln_g.reshape(1, W)]
    n_heads = dk.shape[1] // W
    return pl.pallas_call(
        functools.partial(_diff_attn_kernel, with_latent=not ctx_queries, out_scale=1.0 - lambda_init),
        grid=(n_samples, n_heads, nq),
        in_specs=in_specs,
        out_specs=pl.BlockSpec((tq, W), lambda b, h, t: (orow(b, t), h)),
        out_shape=jax.ShapeDtypeStruct((out_rows, dk.shape[1]), BF16),
        compiler_params=_cparams(3, 56), name="diff_attn_ctx" if ctx_queries else "diff_attn",
    )(*args)


def _na_bias_tables(rpb, n_rows):
    W = GRID_W
    a = np.arange(8)[:, None, None, None]
    qc = np.arange(W)[None, :, None, None]
    kr = np.arange(16)[None, None, :, None]
    kc = np.arange(W)[None, None, None, :]
    cs = np.clip(qc - NA_KW // 2, 0, W - NA_KW)
    col_ok = (kc >= cs) & (kc < cs + NA_KW)
    dx = np.clip(kc - qc + NA_KW - 1, 0, 2 * NA_KW - 2)
    dys, oks = [], []
    for rs_l, dy in ((np.maximum(a - 4, 0), kr - a + 7), (a, kr - a + 3), (np.minimum(a + 4, 8), kr - a - 1)):
        row_ok = (kr >= rs_l) & (kr < rs_l + NA_KH)
        oks.append(np.broadcast_to(row_ok & col_ok, (8, W, 16, W)).reshape(8 * W, 16 * W))
        dys.append(np.broadcast_to(np.clip(dy, 0, 2 * NA_KH - 2), (8, 1, 16, 1)).reshape(8 * 16))
    ok = np.stack(oks)
    H = rpb.shape[0]
    by_dx = jnp.take(rpb.astype(F32), jnp.asarray(dx.reshape(W * W)), axis=2)
    by_dy = jnp.take(by_dx, jnp.asarray(np.concatenate(dys)), axis=1)
    bias = by_dy.reshape(H, 3, 8, 16, W, W).transpose(0, 1, 2, 4, 3, 5).reshape(H, 3, 8 * W, 16 * W)
    return jnp.where(jnp.asarray(ok)[None], bias, NEG_BIG)


def _na_attn_kernel(q_ref, kl_ref, vl_ref, kc_ref, vc_ref, bias_ref, o_ref, *, scale, n_rows):
    rb = pl.program_id(2)
    r0 = jnp.clip(rb * 8 - 4, 0, n_rows - 16)
    start = pl.multiple_of(r0 * GRID_W, 256)
    kw = kl_ref[pl.ds(start, 16 * GRID_W), :]
    vw = vl_ref[pl.ds(start, 16 * GRID_W), :]
    q = q_ref[...]
    s_lat = _dot_nt(q, kw) * scale + bias_ref[...]
    s_ctx = _dot_nt(q, kc_ref[...]) * scale
    (p_lat, p_ctx), l = _softmax_parts([s_lat, s_ctx])
    inv = 1.0 / l
    o = _dot((p_lat * inv).astype(BF16), vw) + _dot((p_ctx * inv).astype(BF16), vc_ref[...])
    o_ref[...] = o.astype(BF16)


def na_attention(z1, bias, n_samples, n_tok, n_ctx, n_heads):
    n_rows = n_tok // GRID_W
    assert n_rows >= 16 and n_rows % 8 == 0
    nrb = n_rows // 8
    tq = 8 * GRID_W
    d = HEAD_DIM
    cblk = n_samples * n_tok // n_ctx
    btype = lambda rb: jnp.where(rb == 0, 0, jnp.where(rb == nrb - 1, 2, 1))
    return pl.pallas_call(
        functools.partial(_na_attn_kernel, scale=d ** -0.5, n_rows=n_rows),
        grid=(n_samples, n_heads, nrb),
        in_specs=[pl.BlockSpec((tq, d), lambda b, h, rb: (b * nrb + rb, h)),
                  pl.BlockSpec((n_tok, d), lambda b, h, rb: (b, n_heads + h)),
                  pl.BlockSpec((n_tok, d), lambda b, h, rb: (b, 2 * n_heads + h)),
                  pl.BlockSpec((n_ctx, d), lambda b, h, rb: (cblk + b, n_heads + h)),
                  pl.BlockSpec((n_ctx, d), lambda b, h, rb: (cblk + b, 2 * n_heads + h)),
                  pl.BlockSpec((None, None, tq, 16 * GRID_W), lambda b, h, rb: (h, btype(rb), 0, 0))],
        out_specs=pl.BlockSpec((tq, d), lambda b, h, rb: (b * nrb + rb, h)),
        out_shape=jax.ShapeDtypeStruct((n_samples * n_tok, n_heads * d), BF16),
        compiler_params=_cparams(3, 40), name="na_attn",
    )(z1, z1, z1, z1, z1, bias)


def _ctx_attn_kernel(q_ref, k_ref, v_ref, o_ref, *, scale):
    (p,), l = _softmax_parts([_dot_nt(q_ref[...], k_ref[...]) * scale])
    o_ref[...] = _dot((p * (1.0 / l)).astype(BF16), v_ref[...]).astype(BF16)


def na_ctx_attention(z1, n_samples, n_tok, n_ctx, n_heads):
    d = HEAD_DIM
    cblk = n_samples * n_tok // n_ctx
    return pl.pallas_call(
        functools.partial(_ctx_attn_kernel, scale=d ** -0.5),
        grid=(n_samples, n_heads),
        in_specs=[pl.BlockSpec((n_ctx, d), lambda b, h: (cblk + b, h)),
                  pl.BlockSpec((n_ctx, d), lambda b, h: (cblk + b, n_heads + h)),
                  pl.BlockSpec((n_ctx, d), lambda b, h: (cblk + b, 2 * n_heads + h))],
        out_specs=pl.BlockSpec((n_ctx, d), lambda b, h: (b, h)),
        out_shape=jax.ShapeDtypeStruct((n_samples * n_ctx, n_heads * d), BF16),
        compiler_params=_cparams(2, 16), name="na_attn_ctx",
    )(z1, z1, z1)


def _hgrn_chunk(q_raw, f_raw, v, lb, st_ref, reverse):
    C, HW = q_raw.shape
    dk = HEAD_DIM
    n_heads = HW // dk
    q = _silu(q_raw) * (dk ** -0.5)
    f = lb + (1.0 - lb) * jax.nn.sigmoid(f_raw)
    g = jnp.log(f)
    k = 1.0 - f
    t_idx = lax.broadcasted_iota(jnp.int32, (C, 1), 0)
    p = (C - 1 - t_idx) if reverse else t_idx

    def prev(x, s):
        return pltpu.roll(x, (C - s) if reverse else s, 0)

    def nxt(x, s):
        return pltpu.roll(x, s if reverse else (C - s), 0)

    def incl_scan(b):
        x = g
        pb = jnp.bitwise_and(p, b - 1)
        s = 1
        while s < b:
            x = x + jnp.where(pb >= s, prev(x, s), 0.0)
            s *= 2
        return x

    def excl_rscan(b):
        if b == 1:
            return jnp.zeros_like(g)
        pb = jnp.bitwise_and(p, b - 1)
        x = jnp.where(pb <= b - 2, nxt(g, 1), 0.0)
        s = 1
        while s < b:
            x = x + jnp.where(pb + s <= b - 1, nxt(x, s), 0.0)
            s *= 2
        return x

    cum = incl_scan(C)
    q_in = (q * jnp.exp(cum)).astype(BF16)
    k_out = (k * jnp.exp(excl_rscan(C))).astype(BF16)
    total = jnp.exp(cum[0:1] if reverse else cum[C - 1:C])
    qb, kb, vb = q.astype(BF16), k.astype(BF16), v.astype(BF16)

    levels = []
    b = C // 2
    while b >= 1:
        upper = jnp.bitwise_and(p, b) != 0
        ql = jnp.where(upper, q * jnp.exp(incl_scan(b)), 0.0).astype(BF16)
        kl = jnp.where(upper, 0.0, k * jnp.exp(excl_rscan(b))).astype(BF16)
        levels.append((b, ql, kl))
        b //= 2

    s_idx = lax.broadcasted_iota(jnp.int32, (1, C), 1)
    ps = (C - 1 - s_idx) if reverse else s_idx
    outs = []
    for h in range(n_heads):
        sl = slice(h * dk, (h + 1) * dk)
        st_old = st_ref[h]
        att = jnp.where(p == ps, _dot_nt(qb[:, sl], kb[:, sl]), 0.0)
        for b, ql, kl in levels:
            same_pair = jnp.bitwise_and(p, -2 * b) == jnp.bitwise_and(ps, -2 * b)
            att = att + jnp.where(same_pair, _dot_nt(ql[:, sl], kl[:, sl]), 0.0)
        outs.append(_dot_nt(q_in[:, sl], st_old.astype(BF16)) + _dot(att.astype(BF16), vb[:, sl]))
        st_ref[h] = total[:, sl] * st_old + _dot_tn(vb[:, sl], k_out[:, sl])
    return jnp.concatenate(outs, axis=1)


def _hgrn_kernel(qf_ref, ff_ref, if_ref, qb_ref, fb_ref, ib_ref, lb_ref, of_ref, ob_ref, sf_ref, sb_ref):
    @pl.when(pl.program_id(1) == 0)
    def _():
        sf_ref[...] = jnp.zeros_like(sf_ref)
        sb_ref[...] = jnp.zeros_like(sb_ref)

    lb = lb_ref[...]
    of_ref[...] = _hgrn_chunk(qf_ref[...], ff_ref[...], if_ref[...], lb, sf_ref, reverse=False)
    ob_ref[...] = _hgrn_chunk(qb_ref[...], fb_ref[...], ib_ref[...], lb, sb_ref, reverse=True)


def hgrn_scan(z2, lb, n_samples, n_tok, n_ctx):
    M = z2.shape[0]
    HW = z2.shape[1] // 5
    C = HGRN_CHUNK
    ncc, ncl = n_ctx // C, n_tok // C
    lat_blocks = n_samples * ncl

    def fwd_row(b, c):
        return jnp.where(c < ncc, lat_blocks + b * ncc + c, b * ncl + (c - ncc))

    def bwd_row(b, c):
        return jnp.where(c < ncc, lat_blocks + b * ncc + (ncc - 1 - c), b * ncl + (ncl - 1 - (c - ncc)))

    spec = lambda rowf, col: pl.BlockSpec((C, HW), lambda b, c: (rowf(b, c), col))
    return pl.pallas_call(
        _hgrn_kernel, grid=(n_samples, ncc + ncl),
        in_specs=[spec(fwd_row, 0), spec(fwd_row, 1), spec(fwd_row, 3),
                  spec(bwd_row, 0), spec(bwd_row, 2), spec(bwd_row, 3),
                  pl.BlockSpec((1, HW), lambda b, c: (0, 0))],
        out_specs=[spec(fwd_row, 0), spec(bwd_row, 0)],
        out_shape=[jax.ShapeDtypeStruct((M, HW), F32)] * 2,
        scratch_shapes=[pltpu.VMEM((HW // HEAD_DIM, HEAD_DIM, HEAD_DIM), F32)] * 2,
        compiler_params=_cparams(2, 48), name="hgrn_scan",
    )(z2, z2, z2, z2, z2, z2, lb.reshape(1, HW))


def _hgrn_readout_kernel(of_ref, ob_ref, gz_ref, g_ref, o_ref):
    for h in range(of_ref.shape[1] // HEAD_DIM):
        sl = slice(h * HEAD_DIM, (h + 1) * HEAD_DIM)
        x = of_ref[:, sl] + ob_ref[:, sl]
        y = x * lax.rsqrt(jnp.mean(x * x, axis=-1, keepdims=True) + RMS_EPS) * g_ref[...]
        o_ref[:, sl] = (y * _silu(gz_ref[:, sl])).astype(BF16)


def hgrn_readout(o_f, o_b, z2, norm_g, n_rows):
    HW = o_f.shape[1]
    tm = ROW_TILE
    return pl.pallas_call(
        _hgrn_readout_kernel, grid=(n_rows // tm,),
        in_specs=[pl.BlockSpec((tm, HW), lambda i: (i, 0)), pl.BlockSpec((tm, HW), lambda i: (i, 0)),
                  pl.BlockSpec((tm, HW), lambda i: (i, 4)), pl.BlockSpec((1, HEAD_DIM), lambda i: (0, 0))],
        out_specs=pl.BlockSpec((tm, HW), lambda i: (i, 0)),
        out_shape=jax.ShapeDtypeStruct((n_rows, HW), BF16),
        compiler_params=_cparams(1, 32), name="hgrn_readout",
    )(o_f, o_b, z2, norm_g.reshape(1, HEAD_DIM))


def _merge_kernel(h_ref, oa_ref, od_ref, om_ref, or_ref, wg_ref, bg_ref, wb_ref, o_ref, wg_bf, wb_bf):
    @pl.when(pl.program_id(1) == 0)
    def _():
        wg_bf[...] = wg_ref[...].astype(BF16)
        wb_bf[...] = wb_ref[...].astype(BF16)

    h = h_ref[...]
    acc = None
    for j, o_j in enumerate((oa_ref, od_ref, om_ref, or_ref)):
        gate = jax.nn.sigmoid(_dot(h, wg_bf[j]) + bg_ref[j])
        term = gate * _dot(o_j[...], wb_bf[j])
        acc = term if acc is None else acc + term
    o_ref[...] = acc.astype(BF16)


def merge_branches(h, outs, w_gate, b_gate, w_branch, layer, n_rows):
    D = h.shape[1]
    L, nb, BW = w_branch.shape[:3]
    tm, tn = ROW_TILE, 256
    single = dict(pipeline_mode=pl.Buffered(1))
    return pl.pallas_call(
        _merge_kernel, grid=(D // tn, n_rows // tm),
        in_specs=[pl.BlockSpec((tm, D), lambda j, i: (i, 0))]
                 + [pl.BlockSpec((tm, BW), lambda j, i: (i, 0))] * nb
                 + [pl.BlockSpec((None, nb, D, tn), lambda j, i: (layer, 0, 0, j), **single),
                    pl.BlockSpec((None, nb, 1, tn), lambda j, i: (layer, 0, 0, j)),
                    pl.BlockSpec((None, nb, BW, tn), lambda j, i: (layer, 0, 0, j), **single)],
        out_specs=pl.BlockSpec((tm, tn), lambda j, i: (i, j)),
        out_shape=jax.ShapeDtypeStruct((n_rows, D), BF16),
        scratch_shapes=[pltpu.VMEM((nb, D, tn), BF16), pltpu.VMEM((nb, BW, tn), BF16)],
        compiler_params=_cparams(2, 58), name="merge",
    )(h, *outs, w_gate, b_gate.reshape(L, nb, 1, D), w_branch)


def _ffn_in_kernel(x_ref, w1_ref, w3_ref, o_ref, w1_bf, w3_bf):
    @pl.when(pl.program_id(1) == 0)
    def _():
        w1_bf[...] = w1_ref[...].astype(BF16)
        w3_bf[...] = w3_ref[...].astype(BF16)

    x = x_ref[...]
    o_ref[...] = (_silu(_dot(x, w1_bf[...])) * _dot(x, w3_bf[...])).astype(BF16)


def ffn_in(h, w1, w3, layer, n_rows):
    _, D, Fd = w1.shape
    tm, tn = 2 * ROW_TILE, 256
    w_spec = pl.BlockSpec((None, D, tn), lambda j, i: (layer, 0, j))
    return pl.pallas_call(
        _ffn_in_kernel,
        grid=(Fd // tn, pl.cdiv(n_rows, tm)),
        in_specs=[pl.BlockSpec((tm, D), lambda j, i: (i, 0)), w_spec, w_spec],
        out_specs=pl.BlockSpec((tm, tn), lambda j, i: (i, j)),
        out_shape=jax.ShapeDtypeStruct((n_rows, Fd), BF16),
        scratch_shapes=[pltpu.VMEM((D, tn), BF16)] * 2,
        compiler_params=_cparams(2, 50), name="ffn_in",
    )(h, w1, w3)


def _group_spec(tm, tn, n_lat, n_samples):
    return pl.BlockSpec((None, 1, tn), lambda j, i: (_grp_of_tile(i, tm, n_lat, n_samples), 0, j))


def ffn_out(u, w2, layer, xres, gate, n_rows, n_lat, n_samples):
    Fd, D = w2.shape[1:]
    G = gate.shape[0]
    tm, tn = 256, 512
    return _mm_call(u, w2, layer, Fd, n_rows=n_rows, n_blocks=D // tn, tn=tn, tm=tm, out_dtype=F32,
                    epilogue=_residual_epilogue, extras=(xres, gate.reshape(G, 1, D)),
                    extra_specs=(pl.BlockSpec((tm, tn), lambda j, i: (i, j)), _group_spec(tm, tn, n_lat, n_samples)),
                    w_single_buffer=True, name="ffn_out")


MOE_TILE = 256


def moe_route_plan(comb, n_experts):
    M = comb.shape[0]
    T = MOE_TILE
    w = comb[:, :n_experts]
    sel = w > 0
    n_tiles = 2 * M // T + n_experts
    R = n_tiles * T
    cnt = jnp.sum(sel, axis=0, dtype=jnp.int32)
    rank = jnp.cumsum(sel, axis=0, dtype=jnp.int32) - 1
    gsz = (cnt + T - 1) // T * T
    gend = jnp.cumsum(gsz)
    off = gend - gsz
    dest_all = off[None, :] + rank
    tok = lax.broadcasted_iota(jnp.int32, sel.shape, 0)
    src_tok = jnp.zeros((R,), jnp.int32).at[jnp.where(sel, dest_all, R).reshape(-1)].set(tok.reshape(-1), mode="drop")
    tile_start = jnp.arange(n_tiles, dtype=jnp.int32) * T
    tile_expert = jnp.minimum(jnp.searchsorted(gend, tile_start, side="right"), n_experts - 1).astype(jnp.int32)
    n_used = (gend[-1] // T).reshape(1).astype(jnp.int32)
    e_lo = jnp.argmax(sel, axis=1)
    e_hi = n_experts - 1 - jnp.argmax(sel[:, ::-1], axis=1)
    take = lambda a, e: jnp.take_along_axis(a, e[:, None], axis=1)[:, 0]
    two = e_hi != e_lo
    dest = jnp.stack([take(dest_all, e_lo), take(dest_all, e_hi)], axis=1).astype(jnp.int32)
    pw = jnp.zeros((M, 128), F32).at[:, 0].set(take(w, e_lo)).at[:, 1].set(jnp.where(two, take(w, e_hi), 0.0))
    return src_tok, tile_expert, n_used, dest, pw


def _row_copy(src_hbm, row, dst_buf, slot, r, sem):
    return pltpu.make_async_copy(src_hbm.at[pl.ds(row, 1)], dst_buf.at[slot, pl.ds(r, 1)], sem.at[slot])


def _gather_tile(idx_of_row, src_hbm, buf, sem, n_rows_tile):
    i = pl.program_id(0)
    n = pl.num_programs(0)

    def start_tile(t, slot):
        def body(r, carry):
            _row_copy(src_hbm, idx_of_row(t * n_rows_tile + r), buf, slot, r, sem).start()
            return carry
        lax.fori_loop(0, n_rows_tile, body, 0)

    @pl.when(i == 0)
    def _():
        start_tile(0, 0)

    @pl.when(i + 1 < n)
    def _():
        start_tile(i + 1, (i + 1) % 2)

    slot = i % 2

    def wait_body(r, carry):
        _row_copy(src_hbm, 0, buf, slot, r, sem).wait()
        return carry
    lax.fori_loop(0, n_rows_tile, wait_body, 0)
    return slot


def _moe_gather_kernel(src_ref, h_hbm, o_ref, buf, sem):
    slot = _gather_tile(lambda r: src_ref[r], h_hbm, buf, sem, o_ref.shape[0])
    o_ref[...] = buf[slot].astype(BF16)


def moe_gather(hf, src_tok):
    R = src_tok.shape[0]
    D = hf.shape[1]
    T = MOE_TILE
    return pl.pallas_call(
        _moe_gather_kernel,
        grid_spec=pltpu.PrefetchScalarGridSpec(
            num_scalar_prefetch=1, grid=(R // T,),
            in_specs=[pl.BlockSpec(memory_space=pl.ANY)],
            out_specs=pl.BlockSpec((T, D), lambda i, src: (i, 0)),
            scratch_shapes=[pltpu.VMEM((2, T, D), F32), pltpu.SemaphoreType.DMA((2,))]),
        out_shape=jax.ShapeDtypeStruct((R, D), BF16),
        compiler_params=_cparams(1, 24), name="moe_gather",
    )(src_tok, hf)


def _moe_in_kernel(te_ref, nu_ref, x_ref, w1_ref, w3_ref, o_ref, w1_bf, w3_bf):
    i = pl.program_id(1)
    fresh = jnp.logical_or(i == 0, te_ref[i] != te_ref[jnp.maximum(i - 1, 0)])

    @pl.when(fresh)
    def _():
        w1_bf[...] = w1_ref[...].astype(BF16)
        w3_bf[...] = w3_ref[...].astype(BF16)

    @pl.when(i < nu_ref[0])
    def _():
        x = x_ref[...]
        o_ref[...] = (_silu(_dot(x, w1_bf[...])) * _dot(x, w3_bf[...])).astype(BF16)

    @pl.when(i >= nu_ref[0])
    def _():
        o_ref[...] = jnp.zeros_like(o_ref)


def moe_ffn_in(xs, w1, w3, e0, tile_expert, n_used):
    R, D = xs.shape
    Fd = w1.shape[2]
    T, tn = MOE_TILE, 512
    w_spec = pl.BlockSpec((None, D, tn), lambda j, i, te, nu: (e0 + te[i], 0, j))
    return pl.pallas_call(
        _moe_in_kernel,
        grid_spec=pltpu.PrefetchScalarGridSpec(
            num_scalar_prefetch=2, grid=(Fd // tn, R // T),
            in_specs=[pl.BlockSpec((T, D), lambda j, i, te, nu: (i, 0)), w_spec, w_spec],
            out_specs=pl.BlockSpec((T, tn), lambda j, i, te, nu: (i, j)),
            scratch_shapes=[pltpu.VMEM((D, tn), BF16)] * 2),
        out_shape=jax.ShapeDtypeStruct((R, Fd), BF16),
        compiler_params=_cparams(2, 54), name="moe_ffn_in",
    )(tile_expert, n_used, xs, w1, w3)


def _moe_out_kernel(te_ref, nu_ref, u_ref, w2_ref, o_ref, w2_bf):
    i = pl.program_id(1)
    fresh = jnp.logical_or(i == 0, te_ref[i] != te_ref[jnp.maximum(i - 1, 0)])

    @pl.when(fresh)
    def _():
        w2_bf[...] = w2_ref[...].astype(BF16)

    @pl.when(i < nu_ref[0])
    def _():
        o_ref[...] = _dot(u_ref[...], w2_bf[...])

    @pl.when(i >= nu_ref[0])
    def _():
        o_ref[...] = jnp.zeros_like(o_ref)


def moe_ffn_out(u, w2, e0, tile_expert, n_used):
    R, Fd = u.shape
    D = w2.shape[2]
    T, tn = MOE_TILE, 1024
    return pl.pallas_call(
        _moe_out_kernel,
        grid_spec=pltpu.PrefetchScalarGridSpec(
            num_scalar_prefetch=2, grid=(D // tn, R // T),
            in_specs=[pl.BlockSpec((T, Fd), lambda j, i, te, nu: (i, 0)),
                      pl.BlockSpec((None, Fd, tn), lambda j, i, te, nu: (e0 + te[i], 0, j))],
            out_specs=pl.BlockSpec((T, tn), lambda j, i, te, nu: (i, j)),
            scratch_shapes=[pltpu.VMEM((Fd, tn), BF16)]),
        out_shape=jax.ShapeDtypeStruct((R, D), F32),
        compiler_params=_cparams(2, 40), name="moe_ffn_out",
    )(tile_expert, n_used, u, w2)


def _moe_combine_kernel(dest_ref, y_hbm, pw_ref, x_ref, g_ref, o_ref, buf, sem):
    T = o_ref.shape[0]
    slot = _gather_tile(lambda r: dest_ref[r], y_hbm, buf, sem, 2 * T)
    pw = pw_ref[...]
    mix = pw[:, 0:1] * buf[slot, 0:T, :] + pw[:, 1:2] * buf[slot, T:2 * T, :]
    o_ref[...] = x_ref[...] + g_ref[...] * mix


def moe_combine(y, dest, pw, xres, gate, n_lat, n_samples):
    M, D = xres.shape
    G = gate.shape[0]
    T = 128
    dest = dest.reshape(M // T, T, 2).transpose(0, 2, 1)
    return pl.pallas_call(
        _moe_combine_kernel,
        grid_spec=pltpu.PrefetchScalarGridSpec(
            num_scalar_prefetch=1, grid=(M // T,),
            in_specs=[pl.BlockSpec(memory_space=pl.ANY),
                      pl.BlockSpec((T, 128), lambda i, d: (i, 0)),
                      pl.BlockSpec((T, D), lambda i, d: (i, 0)),
                      pl.BlockSpec((None, 1, D), lambda i, d: (_grp_of_tile(i, T, n_lat, n_samples), 0, 0))],
            out_specs=pl.BlockSpec((T, D), lambda i, d: (i, 0)),
            scratch_shapes=[pltpu.VMEM((2, 2 * T, D), F32), pltpu.SemaphoreType.DMA((2,))]),
        out_shape=jax.ShapeDtypeStruct((M, D), F32),
        compiler_params=_cparams(1, 32), name="moe_combine",
    )(dest.reshape(-1), y, pw, xres, gate.reshape(G, 1, D))


def kernel(x, c, ctx, c_ctx, norm1_g, norm2_g, w_ada, b_ada, w_in, na_rpb, diff_lambda, diff_subln_g,
           mla_q_norm_g, mla_w_q_up, mla_kv_norm_g, mla_w_kv_up, hgrn_lower_bounds, hgrn_norm_g,
           w_branch, w_gate, b_gate, w_out, ffn_w1, ffn_w3, ffn_w2, moe_router, moe_w1, moe_w3, moe_w2,
           final_norm_g):
    B, N, D = x.shape
    NC = ctx.shape[1]
    L = w_ada.shape[0]
    n_lat, n_all = B * N, B * N + B * NC
    BW = w_branch.shape[2]
    n_heads = BW // HEAD_DIM
    HG0 = w_in.shape[2] - 5 * BW
    assert N % ROW_TILE == 0 and (B * NC) % ROW_TILE == 0 and NC % HGRN_CHUNK == 0

    xs = jnp.concatenate([x.reshape(n_lat, D), ctx.reshape(B * NC, D)], axis=0)

    cond8 = jnp.zeros((8, D), F32).at[:B].set(c).at[B].set(c_ctx)
    mods = adaln_all(cond8, w_ada, b_ada)[:, :B + 1].reshape(L, B + 1, 6, D)

    lb_all = jnp.cumsum(jax.nn.softmax(hgrn_lower_bounds.astype(F32), axis=0), axis=0)
    lb_all = lb_all - lb_all[0:1]

    w_in_hgrn = w_in[:, :, HG0:]

    cosd, sind = _rope_tables(N, B * NC, 2 * 64, ROW_TILE)
    cosm, sinm = _rope_tables(N, B * NC, 64, ROW_TILE)

    RQ, RKV = mla_w_q_up.shape[1], mla_w_kv_up.shape[1]
    wq4 = mla_w_q_up.reshape(L, RQ, n_heads, HEAD_DIM + MLA_ROPE)
    wq_cat = jnp.pad(wq4, ((0, 0), (0, 0), (0, 0), (0, MLA_SLOT - HEAD_DIM - MLA_ROPE))).reshape(L, RQ, -1)
    wkv4 = mla_w_kv_up.reshape(L, RKV, n_heads, 2 * HEAD_DIM)
    wkn = wkv4[..., :HEAD_DIM].reshape(L, RKV, -1)
    wvt = jnp.swapaxes(wkv4[..., HEAD_DIM:].reshape(L, RKV, -1), 1, 2)

    for l in range(L):
        need_ctx = l < L - 1
        n_rows = n_all if need_ctx else n_lat
        lambda_init = 0.8 - 0.6 * math.exp(-0.3 * l)
        sh1, sc1, g1, sh2, sc2, g2 = (mods[l, :, k] for k in range(6))

        h = norm_mod(xs, norm1_g[l], sh1, sc1, N, B)
        z1 = _mm_call(h, w_in, l, D, n_rows=n_all, n_blocks=15, tn=512, tm=2 * ROW_TILE, out_dtype=BF16,
                      name="w_in_attn")
        z2 = _mm_call(h, w_in_hgrn, l, D, n_rows=n_all, n_blocks=5 * BW // 512, tn=512, tm=2 * ROW_TILE,
                      out_dtype=F32, name="w_in_hgrn")

        bias = _na_bias_tables(na_rpb[l], N // GRID_W)
        o_a = na_attention(z1, bias, B, N, NC, n_heads)
        dq, dk = diff_rope(z1, cosd, sind, N, n_lat)
        lam_rows = jnp.zeros((8, HEAD_DIM), F32).at[:4].set(diff_lambda[l]).at[4].set(lambda_init)
        dvt = jnp.transpose(z1[:, 5 * BW:6 * BW])
        o_d = diff_attention(dq, dk, dvt, lam_rows, diff_subln_g[l], lambda_init, B, N, NC, False)
        qcat, kcat, mvt = mla_prep(z1, cosm, sinm, mla_q_norm_g[l], wq_cat[l], mla_kv_norm_g[l], wkn[l], wvt[l],
                                   N, n_lat)
        o_m = mla_attention(qcat, kcat, mvt, B, N, NC, False)
        o_f, o_b = hgrn_scan(z2, lb_all[l], B, N, NC)
        o_r = hgrn_readout(o_f, o_b, z2, hgrn_norm_g[l], n_rows)
        if need_ctx:
            o_a = jnp.concatenate([o_a, na_ctx_attention(z1, B, N, NC, n_heads)], axis=0)
            o_d = jnp.concatenate([o_d, diff_attention(dq, dk, dvt, lam_rows, diff_subln_g[l], lambda_init,
                                                       B, N, NC, True)], axis=0)
            o_m = jnp.concatenate([o_m, mla_attention(qcat, kcat, mvt, B, N, NC, True)], axis=0)

        s = merge_branches(h, (o_a, o_d, o_m, o_r), w_gate, b_gate, w_branch, l, n_rows)
        G = B + 1
        tmo = 2 * ROW_TILE
        xs = _mm_call(s, w_out, l, D, n_rows=n_rows, n_blocks=D // 512, tn=512, tm=tmo, out_dtype=F32,
                      epilogue=_residual_epilogue, extras=(xs, g1.reshape(G, 1, D)),
                      extra_specs=(pl.BlockSpec((tmo, 512), lambda j, i: (i, j)), _group_spec(tmo, 512, N, B)),
                      name="w_out")

        j = l // 2
        if l % 2 == 0:
            h2 = norm_mod(xs, norm2_g[l], sh2, sc2, N, B)
            u = ffn_in(h2, ffn_w1, ffn_w3, j, n_rows)
            xs = ffn_out(u, ffn_w2, j, xs, g2, n_rows, N, B)
        else:
            hf, comb = norm_mod(xs, norm2_g[l], sh2, sc2, N, B, w_router=moe_router[j])
            NM, E, _, FE = moe_w1.shape
            src_tok, tile_expert, n_used, dest, pw = moe_route_plan(comb, E)
            xg = moe_gather(hf, src_tok)
            u = moe_ffn_in(xg, moe_w1.reshape(NM * E, D, FE), moe_w3.reshape(NM * E, D, FE), j * E, tile_expert, n_used)
            y = moe_ffn_out(u, moe_w2.reshape(NM * E, FE, D), j * E, tile_expert, n_used)
            xs = moe_combine(y, dest, pw, xs, g2, N, B)

    return final_norm(xs[:n_lat], final_norm_g).reshape(B, N, D)
```

```python
import functools
import math

import numpy as np
import jax
import jax.numpy as jnp
from jax import lax
from jax.experimental import pallas as pl
from jax.experimental.pallas import tpu as pltpu

F32 = jnp.float32
BF16 = jnp.bfloat16

GRID_W = 64
RMS_EPS = 1e-6
ROPE_BASE = 10000.0
NA_KH, NA_KW = 8, 16
HEAD_DIM = 128
MLA_ROPE = 64
HGRN_CHUNK = 64
MOE_EXPERTS = 8
NEG_BIG = -1e30

VMEM_LIMIT_V7X = 60000 * 1024
ROW_TILE = 512


def _cparams(n_axes, vmem_mb):
    return pltpu.CompilerParams(dimension_semantics=("arbitrary",) * n_axes,
                                vmem_limit_bytes=min(int(vmem_mb * 2**20), VMEM_LIMIT_V7X))


def _silu(x):
    return x * jax.nn.sigmoid(x)


def _dot(a, b):
    return jnp.dot(a, b, preferred_element_type=F32)


def _dot_nt(a, b):
    return lax.dot_general(a, b, (((1,), (1,)), ((), ())), preferred_element_type=F32)


def _dot_tn(a, b):
    return lax.dot_general(a, b, (((0,), (0,)), ((), ())), preferred_element_type=F32)


def _adaln_kernel(c_ref, w_ref, b_ref, o_ref):
    @pl.when(pl.program_id(1) == 0)
    def _():
        o_ref[...] = jnp.broadcast_to(b_ref[...], o_ref.shape)

    x = _silu(c_ref[...]).astype(BF16)
    o_ref[...] += _dot(x, w_ref[...].astype(BF16))


def adaln_all(cond8, w_ada, b_ada):
    L, D, N6 = w_ada.shape
    tk = 128
    cond_k = cond8.reshape(8, D // tk, tk).transpose(1, 0, 2)
    return pl.pallas_call(
        _adaln_kernel,
        grid=(L, D // tk),
        in_specs=[pl.BlockSpec((None, 8, tk), lambda l, k: (k, 0, 0)),
                  pl.BlockSpec((None, tk, N6), lambda l, k: (l, k, 0)),
                  pl.BlockSpec((None, 1, N6), lambda l, k: (l, 0, 0))],
        out_specs=pl.BlockSpec((None, 8, N6), lambda l, k: (l, 0, 0)),
        out_shape=jax.ShapeDtypeStruct((L, 8, N6), F32),
        compiler_params=_cparams(2, 48),
        name="adaln",
    )(cond_k, w_ada, b_ada.reshape(L, 1, N6))


def _norm_mod_kernel(x_ref, g_ref, sh_ref, sc_ref, *rest, with_router):
    x = x_ref[...]
    y = x * lax.rsqrt(jnp.mean(x * x, axis=-1, keepdims=True) + RMS_EPS) * g_ref[...]
    h = y * (1.0 + sc_ref[...]) + sh_ref[...]
    if not with_router:
        (h_ref,) = rest
        h_ref[...] = h.astype(BF16)
        return
    wr_ref, h_ref, comb_ref = rest
    h_ref[...] = h
    logits = jnp.dot(h, wr_ref[...], precision=lax.Precision.HIGHEST, preferred_element_type=F32)
    lane = lax.broadcasted_iota(jnp.int32, logits.shape, 1).astype(F32)
    logits = jnp.where(lane < MOE_EXPERTS, logits, -jnp.inf)
    m1 = jnp.max(logits, axis=-1, keepdims=True)
    i1 = jnp.min(jnp.where(logits == m1, lane, 128.0), axis=-1, keepdims=True)
    rest_l = jnp.where(lane == i1, -jnp.inf, logits)
    m2 = jnp.max(rest_l, axis=-1, keepdims=True)
    i2 = jnp.min(jnp.where(rest_l == m2, lane, 128.0), axis=-1, keepdims=True)
    e2 = jnp.exp(m2 - m1)
    p1 = 1.0 / (1.0 + e2)
    p2 = e2 / (1.0 + e2)
    comb_ref[...] = jnp.where(lane == i1, p1, 0.0) + jnp.where(lane == i2, p2, 0.0)


def _grp_of_tile(i, tm, lat_rows_per_sample, n_samples):
    return jnp.minimum((i * tm) // lat_rows_per_sample, n_samples)


def norm_mod(x, g, shift, scale, n_lat, n_samples, w_router=None):
    M, D = x.shape
    tm = 256
    G = shift.shape[0]
    grp = lambda i: (_grp_of_tile(i, tm, n_lat, n_samples), 0, 0)
    in_specs = [pl.BlockSpec((tm, D), lambda i: (i, 0)),
                pl.BlockSpec((1, D), lambda i: (0, 0)),
                pl.BlockSpec((None, 1, D), grp),
                pl.BlockSpec((None, 1, D), grp)]
    args = [x, g.reshape(1, D), shift.reshape(G, 1, D), scale.reshape(G, 1, D)]
    out_specs = [pl.BlockSpec((tm, D), lambda i: (i, 0))]
    out_shape = [jax.ShapeDtypeStruct((M, D), BF16 if w_router is None else F32)]
    if w_router is not None:
        wr = jnp.zeros((D, 128), F32).at[:, :w_router.shape[1]].set(w_router)
        in_specs.append(pl.BlockSpec((D, 128), lambda i: (0, 0)))
        args.append(wr)
        out_specs.append(pl.BlockSpec((tm, 128), lambda i: (i, 0)))
        out_shape.append(jax.ShapeDtypeStruct((M, 128), F32))
    outs = pl.pallas_call(
        functools.partial(_norm_mod_kernel, with_router=w_router is not None),
        grid=(M // tm,),
        in_specs=in_specs, out_specs=out_specs, out_shape=out_shape,
        compiler_params=_cparams(1, 40),
        name="norm_mod_router" if w_router is not None else "norm_mod",
    )(*args)
    return outs if w_router is not None else outs[0]


def _final_norm_kernel(x_ref, g_ref, o_ref):
    x = x_ref[...]
    o_ref[...] = x * lax.rsqrt(jnp.mean(x * x, axis=-1, keepdims=True) + RMS_EPS) * g_ref[...]


def final_norm(x, g):
    M, D = x.shape
    tm = 256
    return pl.pallas_call(
        _final_norm_kernel, grid=(M // tm,),
        in_specs=[pl.BlockSpec((tm, D), lambda i: (i, 0)), pl.BlockSpec((1, D), lambda i: (0, 0))],
        out_specs=pl.BlockSpec((tm, D), lambda i: (i, 0)),
        out_shape=jax.ShapeDtypeStruct((M, D), F32),
        compiler_params=_cparams(1, 40), name="final_norm",
    )(x, g.reshape(1, D))


def _resident_mm_kernel(x_ref, w_ref, *rest, n_extra, epilogue):
    extras, o_ref, wbf_ref = rest[:n_extra], rest[n_extra], rest[n_extra + 1]

    @pl.when(pl.program_id(1) == 0)
    def _():
        wbf_ref[...] = w_ref[...].astype(BF16)

    acc = _dot(x_ref[...], wbf_ref[...])
    if epilogue is not None:
        acc = epilogue(acc, *[e[...] for e in extras])
    o_ref[...] = acc.astype(o_ref.dtype)


def _mm_call(x, w, layer, K, *, n_rows, n_blocks, tn, tm, out_dtype, x_col_block=0, w_row_block=0,
             epilogue=None, extras=(), extra_specs=(), w_single_buffer=False, name="mm"):
    w_kwargs = dict(pipeline_mode=pl.Buffered(1)) if w_single_buffer else {}
    in_specs = [pl.BlockSpec((tm, K), lambda j, i: (i, x_col_block)),
                pl.BlockSpec((None, K, tn), lambda j, i: (layer, w_row_block, j), **w_kwargs)]
    in_specs += list(extra_specs)
    out_bytes = jnp.dtype(out_dtype).itemsize
    vmem = (2 * tm * K * 2 + (1 if w_single_buffer else 2) * K * tn * 4 + K * tn * 2
            + 2 * tm * tn * out_bytes + 3 * tm * tn * 4 + len(extras) * 2 * tm * tn * 4) / 2**20 + 6
    return pl.pallas_call(
        functools.partial(_resident_mm_kernel, n_extra=len(extras), epilogue=epilogue),
        grid=(n_blocks, pl.cdiv(n_rows, tm)),
        in_specs=in_specs,
        out_specs=pl.BlockSpec((tm, tn), lambda j, i: (i, j)),
        out_shape=jax.ShapeDtypeStruct((n_rows, n_blocks * tn), out_dtype),
        scratch_shapes=[pltpu.VMEM((K, tn), BF16)],
        compiler_params=_cparams(2, vmem),
        name=name,
    )(x, w, *extras)


def _residual_epilogue(acc, xres, gate):
    return xres + gate * acc


def _shifted_mm_kernel(x_ref, wa_ref, wb_ref, o_ref, wbf_ref, *, shift):
    tn = wbf_ref.shape[1]

    @pl.when(pl.program_id(1) == 0)
    def _():
        rows = 512
        for r0 in range(0, wbf_ref.shape[0], rows):
            a = wa_ref[r0:r0 + rows, :]
            b = wb_ref[r0:r0 + rows, :]
            lane = lax.broadcasted_iota(jnp.int32, a.shape, 1)
            w = jnp.where(lane < tn - shift, pltpu.roll(a, tn - shift, 1), pltpu.roll(b, tn - shift, 1))
            wbf_ref[r0:r0 + rows, :] = w.astype(BF16)

    o_ref[...] = _dot(x_ref[...], wbf_ref[...]).astype(o_ref.dtype)


def shifted_matmul(x, w, layer, col0, n_cols, out_dtype, name):
    M, K = x.shape
    tm, tn = ROW_TILE, 512
    shift, blk0 = col0 % tn, col0 // tn
    assert n_cols % tn == 0
    return pl.pallas_call(
        functools.partial(_shifted_mm_kernel, shift=shift),
        grid=(n_cols // tn, M // tm),
        in_specs=[pl.BlockSpec((tm, K), lambda j, i: (i, 0)),
                  pl.BlockSpec((None, K, tn), lambda j, i: (layer, 0, blk0 + j)),
                  pl.BlockSpec((None, K, tn), lambda j, i: (layer, 0, blk0 + j + 1))],
        out_specs=pl.BlockSpec((tm, tn), lambda j, i: (i, j)),
        out_shape=jax.ShapeDtypeStruct((M, n_cols), out_dtype),
        scratch_shapes=[pltpu.VMEM((K, tn), BF16)],
        compiler_params=_cparams(2, 56), name=name,
    )(x, w, w)


def _rope_tables(n_tokens, n_ident, group, tile_rows):
    half = group // 2
    q = half // 2
    pos = np.arange(n_tokens)
    row, col = pos // GRID_W, pos % GRID_W
    freqs = ROPE_BASE ** (-(np.arange(q, dtype=np.float64) / q))
    lane = np.arange(128)
    in_group = lane % group
    axis_pos = np.where((in_group < half)[None, :], row[:, None], col[:, None]).astype(np.float64)
    ang = axis_pos * freqs[(in_group % half) % q][None, :]
    sign = np.where((in_group % half) < q, -1.0, 1.0)[None, :]
    cos, sin = np.cos(ang), np.sin(ang) * sign
    n_pad = -(-n_ident // tile_rows) * tile_rows
    cos = np.concatenate([cos, np.ones((n_pad, 128))], axis=0)
    sin = np.concatenate([sin, np.zeros((n_pad, 128))], axis=0)
    return jnp.asarray(cos, F32), jnp.asarray(sin, F32)


def _rope_apply(x, cos, sin, q):
    lane = lax.broadcasted_iota(jnp.int32, x.shape, 1)
    partner = jnp.where((lane % (2 * q)) < q, pltpu.roll(x, 128 - q, 1), pltpu.roll(x, q, 1))
    return x * cos + partner * sin


LOG2E = math.log2(math.e)


def _diff_rope_kernel(q_ref, k_ref, v_ref, cos_ref, sin_ref, qo_ref, ko_ref, vt_ref, *, q_scale):
    vt_ref[...] = jnp.transpose(v_ref[...].astype(F32)).astype(BF16)
    cos, sin = cos_ref[...], sin_ref[...]
    for src, dst, mult in ((q_ref, qo_ref, q_scale), (k_ref, ko_ref, None)):
        for c in range(src.shape[1] // 128):
            sl = slice(c * 128, (c + 1) * 128)
            y = _rope_apply(src[:, sl].astype(F32), cos, sin, 32)
            dst[:, sl] = (y if mult is None else y * mult).astype(BF16)


def _table_block(i, tm, n_tok, n_lat_total):
    return jnp.where(i * tm < n_lat_total, (i * tm % n_tok) // tm, n_tok // tm)


def diff_rope(z1, cosd, sind, n_tok, n_lat_total):
    M = z1.shape[0]
    tm = ROW_TILE
    W = 1024
    tb = lambda i: (_table_block(i, tm, n_tok, n_lat_total), 0)
    return pl.pallas_call(
        functools.partial(_diff_rope_kernel, q_scale=HEAD_DIM ** -0.5 * LOG2E), grid=(M // tm,),
        in_specs=[pl.BlockSpec((tm, W), lambda i: (i, 3)), pl.BlockSpec((tm, W), lambda i: (i, 4)),
                  pl.BlockSpec((tm, W), lambda i: (i, 5)),
                  pl.BlockSpec((tm, 128), tb), pl.BlockSpec((tm, 128), tb)],
        out_specs=[pl.BlockSpec((tm, W), lambda i: (i, 0))] * 2 + [pl.BlockSpec((W, tm), lambda i: (0, i))],
        out_shape=[jax.ShapeDtypeStruct((M, W), BF16)] * 2 + [jax.ShapeDtypeStruct((W, M), BF16)],
        compiler_params=_cparams(1, 40), name="diff_rope",
    )(z1, z1, z1, cosd, sind)


MLA_SLOT = 2 * HEAD_DIM


def _mla_prep_kernel(cq_ref, ckva_ref, ckvb_ref, kr_ref, cos_ref, sin_ref, gq_ref, gkv_ref, wq_ref, wkn_ref,
                     wvt_ref, qcat_ref, kcat_ref, vt_ref, wq_bf, wkn_bf, wvt_bf, *, q_scale):
    @pl.when(pl.program_id(0) == 0)
    def _():
        wq_bf[...] = wq_ref[...].astype(BF16)
        wkn_bf[...] = wkn_ref[...].astype(BF16)
        wvt_bf[...] = wvt_ref[...].astype(BF16)

    cos, sin = cos_ref[...], sin_ref[...]
    d = HEAD_DIM
    n_heads = qcat_ref.shape[1] // MLA_SLOT
    cq = cq_ref[...].astype(F32)
    cqn = (cq * lax.rsqrt(jnp.mean(cq * cq, axis=-1, keepdims=True) + RMS_EPS) * gq_ref[...]).astype(BF16)
    q = _dot(cqn, wq_bf[...])
    a = ckva_ref[...].astype(F32)
    b = ckvb_ref[...].astype(F32)
    ha = a.shape[1]
    ms = (jnp.sum(a * a, axis=-1, keepdims=True) + jnp.sum(b * b, axis=-1, keepdims=True)) / (2 * ha)
    r = lax.rsqrt(ms + RMS_EPS)
    g = gkv_ref[...]
    ckvn = jnp.concatenate([(a * r * g[:, :ha]).astype(BF16), (b * r * g[:, ha:]).astype(BF16)], axis=1)
    kn = _dot(ckvn, wkn_bf[...])
    lane = lax.broadcasted_iota(jnp.int32, cos.shape, 1)
    kr = jnp.where(lane < MLA_ROPE, _rope_apply(kr_ref[...].astype(F32), cos, sin, 16), 0.0).astype(BF16)
    for h in range(n_heads):
        c0 = h * MLA_SLOT
        qcat_ref[:, c0:c0 + d] = (q[:, c0:c0 + d] * q_scale).astype(BF16)
        qcat_ref[:, c0 + d:c0 + 2 * d] = (_rope_apply(q[:, c0 + d:c0 + 2 * d], cos, sin, 16) * q_scale).astype(BF16)
        kcat_ref[:, c0:c0 + d] = kn[:, h * d:(h + 1) * d].astype(BF16)
        kcat_ref[:, c0 + d:c0 + 2 * d] = kr
    vt_ref[...] = _dot_nt(wvt_bf[...], ckvn).astype(BF16)


def mla_prep(z1, cosm, sinm, gq, wq_cat, gkv, wkn, wvt, n_tok, n_lat_total):
    M = z1.shape[0]
    tm = ROW_TILE
    RQ, NQ = wq_cat.shape
    RKV, NKN = wkn.shape
    tb = lambda i: (_table_block(i, tm, n_tok, n_lat_total), 0)
    const = lambda i: (0, 0)
    return pl.pallas_call(
        functools.partial(_mla_prep_kernel, q_scale=(HEAD_DIM + MLA_ROPE) ** -0.5 * LOG2E), grid=(M // tm,),
        in_specs=[pl.BlockSpec((tm, RQ), lambda i: (i, 6144 // RQ)),
                  pl.BlockSpec((tm, RKV // 2), lambda i: (i, 6912 // (RKV // 2))),
                  pl.BlockSpec((tm, RKV // 2), lambda i: (i, 6912 // (RKV // 2) + 1)),
                  pl.BlockSpec((tm, 128), lambda i: (i, 7424 // 128)),
                  pl.BlockSpec((tm, 128), tb), pl.BlockSpec((tm, 128), tb),
                  pl.BlockSpec((1, RQ), const), pl.BlockSpec((1, RKV), const),
                  pl.BlockSpec((RQ, NQ), const), pl.BlockSpec((RKV, NKN), const), pl.BlockSpec((NKN, RKV), const)],
        out_specs=[pl.BlockSpec((tm, NQ), lambda i: (i, 0)),
                   pl.BlockSpec((tm, NQ), lambda i: (i, 0)),
                   pl.BlockSpec((NKN, tm), lambda i: (0, i))],
        out_shape=[jax.ShapeDtypeStruct((M, NQ), BF16), jax.ShapeDtypeStruct((M, NQ), BF16),
                   jax.ShapeDtypeStruct((NKN, M), BF16)],
        scratch_shapes=[pltpu.VMEM((RQ, NQ), BF16), pltpu.VMEM((RKV, NKN), BF16), pltpu.VMEM((NKN, RKV), BF16)],
        compiler_params=_cparams(1, 56), name="mla_prep",
    )(z1, z1, z1, z1, cosm, sinm, gq.reshape(1, RQ), gkv.reshape(1, RKV), wq_cat, wkn, wvt)


def _softmax_parts(s_list):
    m = functools.reduce(jnp.maximum, [jnp.max(s, axis=-1, keepdims=True) for s in s_list])
    p_list = [jnp.exp(s - m) for s in s_list]
    l = functools.reduce(lambda a, b: a + b, [jnp.sum(p, axis=-1, keepdims=True) for p in p_list])
    return p_list, l


def _softmax_parts_t(st_list):
    m = functools.reduce(jnp.maximum, [jnp.max(s, axis=0, keepdims=True) for s in st_list])
    p_list = [jnp.exp2(s - m) for s in st_list]
    l = functools.reduce(lambda a, b: a + b, [jnp.sum(p, axis=0, keepdims=True) for p in p_list])
    return p_list, l


def _query_rows(n_samples, n_tok, n_ctx, tq, ctx_queries):
    if ctx_queries:
        nq = n_ctx // tq
        first = n_samples * n_tok // tq
        return nq, (lambda b, t: first + b * nq + t), n_samples * n_ctx, (lambda b, t: b * nq + t)
    nq = n_tok // tq
    qrow = lambda b, t: b * nq + t
    return nq, qrow, n_samples * n_tok, qrow


def _mla_attn_kernel(q_ref, *rest, with_latent):
    if with_latent:
        kl_ref, vtl_ref, kc_ref, vtc_ref, o_ref = rest
    else:
        kc_ref, vtc_ref, o_ref = rest
    for hh in range(q_ref.shape[1] // MLA_SLOT):
        ds = slice(hh * MLA_SLOT, (hh + 1) * MLA_SLOT)
        dv = slice(hh * HEAD_DIM, (hh + 1) * HEAD_DIM)
        q = q_ref[:, ds]
        k_list = [kc_ref[:, ds]] + ([kl_ref[:, ds]] if with_latent else [])
        vt_list = [vtc_ref[dv, :]] + ([vtl_ref[dv, :]] if with_latent else [])
        p_list, l = _softmax_parts_t([_dot_nt(k, q) for k in k_list])
        ot = functools.reduce(lambda a, b: a + b, [_dot(vt, p.astype(BF16)) for vt, p in zip(vt_list, p_list)])
        o_ref[:, dv] = jnp.transpose(ot * (1.0 / l)).astype(BF16)


def mla_attention(qcat, kcat, vt, n_samples, n_tok, n_ctx, ctx_queries):
    tq = min(512, n_ctx) if ctx_queries else 512
    HP = 1
    nq, qrow, out_rows, orow = _query_rows(n_samples, n_tok, n_ctx, tq, ctx_queries)
    cblk = n_samples * n_tok // n_ctx
    WS, WV = HP * MLA_SLOT, HP * HEAD_DIM
    in_specs = [pl.BlockSpec((tq, WS), lambda b, hp, t: (qrow(b, t), hp))]
    args = [qcat]
    if not ctx_queries:
        in_specs += [pl.BlockSpec((n_tok, WS), lambda b, hp, t: (b, hp)),
                     pl.BlockSpec((WV, n_tok), lambda b, hp, t: (hp, b))]
        args += [kcat, vt]
    in_specs += [pl.BlockSpec((n_ctx, WS), lambda b, hp, t: (cblk + b, hp)),
                 pl.BlockSpec((WV, n_ctx), lambda b, hp, t: (hp, cblk + b))]
    args += [kcat, vt]
    return pl.pallas_call(
        functools.partial(_mla_attn_kernel, with_latent=not ctx_queries),
        grid=(n_samples, vt.shape[0] // WV, nq),
        in_specs=in_specs,
        out_specs=pl.BlockSpec((tq, WV), lambda b, hp, t: (orow(b, t), hp)),
        out_shape=jax.ShapeDtypeStruct((out_rows, vt.shape[0]), BF16),
        compiler_params=_cparams(3, 56), name="mla_attn_ctx" if ctx_queries else "mla_attn",
    )(*args)


def _diff_attn_kernel(q_ref, *rest, with_latent, out_scale):
    if with_latent:
        kl_ref, vtl_ref, kc_ref, vtc_ref, lam_ref, g_ref, o_ref = rest
    else:
        kc_ref, vtc_ref, lam_ref, g_ref, o_ref = rest
    lp = lam_ref[...]
    lam = (jnp.exp(jnp.sum(lp[0:1] * lp[1:2], axis=-1, keepdims=True))
           - jnp.exp(jnp.sum(lp[2:3] * lp[3:4], axis=-1, keepdims=True)) + lam_ref[4:5, 0:1])
    d = HEAD_DIM
    k_refs = [kc_ref] + ([kl_ref] if with_latent else [])
    vt_refs = [vtc_ref] + ([vtl_ref] if with_latent else [])
    parts = []
    for half in range(2):
        sl = slice(half * d, (half + 1) * d)
        q = q_ref[:, sl]
        p_list, l = _softmax_parts_t([_dot_nt(k[:, sl], q) for k in k_refs])
        parts.append((p_list, 1.0 / l))
    (p1_list, inv1), (p2_list, inv2) = parts
    c2 = lam * inv2
    ot = functools.reduce(lambda a, b: a + b,
                          [_dot(vt[...], (p1 * inv1 - p2 * c2).astype(BF16))
                           for p1, p2, vt in zip(p1_list, p2_list, vt_refs)])
    o = jnp.transpose(ot)
    y = o * lax.rsqrt(jnp.mean(o * o, axis=-1, keepdims=True) + RMS_EPS) * g_ref[...]
    o_ref[...] = (y * out_scale).astype(BF16)


def diff_attention(dq, dk, vt, lam_rows, subln_g, lambda_init, n_samples, n_tok, n_ctx, ctx_queries):
    tq = min(512, n_ctx) if ctx_queries else 512
    W = 2 * HEAD_DIM
    nq, qrow, out_rows, orow = _query_rows(n_samples, n_tok, n_ctx, tq, ctx_queries)
    cblk = n_samples * n_tok // n_ctx
    in_specs = [pl.BlockSpec((tq, W), lambda b, h, t: (qrow(b, t), h))]
    args = [dq]
    if not ctx_queries:
        in_specs += [pl.BlockSpec((n_tok, W), lambda b, h, t: (b, h)),
                     pl.BlockSpec((W, n_tok), lambda b, h, t: (h, b))]
        args += [dk, vt]
    in_specs += [pl.BlockSpec((n_ctx, W), lambda b, h, t: (cblk + b, h)),
                 pl.BlockSpec((W, n_ctx), lambda b, h, t: (h, cblk + b)),
                 pl.BlockSpec((8, HEAD_DIM), lambda b, h, t: (0, 0)),
                 pl.BlockSpec((1, W), lambda b, h, t: (0, 0))]
    args += [dk, vt, lam_rows, subln_g.reshape(1, W)]
    n_heads = dk.shape[1] // W
    return pl.pallas_call(
        functools.partial(_diff_attn_kernel, with_latent=not ctx_queries, out_scale=1.0 - lambda_init),
        grid=(n_samples, n_heads, nq),
        in_specs=in_specs,
        out_specs=pl.BlockSpec((tq, W), lambda b, h, t: (orow(b, t), h)),
        out_shape=jax.ShapeDtypeStruct((out_rows, dk.shape[1]), BF16),
        compiler_params=_cparams(3, 56), name="diff_attn_ctx" if ctx_queries else "diff_attn",
    )(*args)


NA_DY = 2 * NA_KH - 1


def _na_bias_tables(rpb):
    W = GRID_W
    H = rpb.shape[0]
    qc = np.arange(W)[:, None]
    kc = np.arange(W)[None, :]
    cs = np.clip(qc - NA_KW // 2, 0, W - NA_KW)
    col_ok = (kc >= cs) & (kc < cs + NA_KW)
    dx = np.clip(kc - qc + NA_KW - 1, 0, 2 * NA_KW - 2)
    by_dx = jnp.take(rpb.astype(F32), jnp.asarray(dx.reshape(W * W)), axis=2).reshape(H, NA_DY, W, W)
    by_dx = jnp.where(jnp.asarray(col_ok)[None, None], by_dx, NEG_BIG)
    neg = jnp.full((H, 1, W, W), NEG_BIG, F32)
    padded = jnp.concatenate([neg, by_dx, neg, neg], axis=1)
    return jnp.concatenate([padded[:, :-1], padded[:, 1:]], axis=-1)


def _na_attn_kernel(q_ref, kl_ref, vl_ref, kc_ref, vc_ref, bias_ref, o_ref, *, scale, n_rows):
    rb = pl.program_id(2)
    r0 = jnp.clip(rb * 8 - 4, 0, n_rows - 16)
    start = pl.multiple_of(r0 * GRID_W, 256)
    kw = kl_ref[pl.ds(start, 16 * GRID_W), :]
    vw = vl_ref[pl.ds(start, 16 * GRID_W), :]
    q = q_ref[...]
    first, last_blk = rb == 0, rb == n_rows // 8 - 1
    dy0 = jnp.where(first, NA_KH - 1, jnp.where(last_blk, -1, NA_KH // 2 - 1))
    lane = lax.broadcasted_iota(jnp.int32, (GRID_W, 2 * GRID_W), 1)
    bias_rows = []
    for qa in range(8):
        win0 = jnp.where(first, max(qa - 4, 0), jnp.where(last_blk, min(qa + 4, 8), qa))
        tiles = []
        for kp in range(8):
            a1 = jnp.clip(2 * kp - qa + dy0 + 1, 0, NA_DY + 1)
            ok_even = jnp.logical_and(2 * kp >= win0, 2 * kp < win0 + NA_KH).astype(jnp.int32)
            ok_odd = jnp.logical_and(2 * kp + 1 >= win0, 2 * kp + 1 < win0 + NA_KH).astype(jnp.int32)
            ok = jnp.where(lane < GRID_W, ok_even, ok_odd)
            tiles.append(jnp.where(ok > 0, bias_ref[a1], NEG_BIG))
        bias_rows.append(jnp.concatenate(tiles, axis=1))
    s_lat = _dot_nt(q, kw) * scale + jnp.concatenate(bias_rows, axis=0)
    s_ctx = _dot_nt(q, kc_ref[...]) * scale
    (p_lat, p_ctx), l = _softmax_parts([s_lat, s_ctx])
    inv = 1.0 / l
    o = _dot((p_lat * inv).astype(BF16), vw) + _dot((p_ctx * inv).astype(BF16), vc_ref[...])
    o_ref[...] = o.astype(BF16)


def na_attention(z1, bias, n_samples, n_tok, n_ctx, n_heads):
    n_rows = n_tok // GRID_W
    assert n_rows >= 16 and n_rows % 8 == 0
    nrb = n_rows // 8
    tq = 8 * GRID_W
    d = HEAD_DIM
    cblk = n_samples * n_tok // n_ctx
    return pl.pallas_call(
        functools.partial(_na_attn_kernel, scale=d ** -0.5, n_rows=n_rows),
        grid=(n_samples, n_heads, nrb),
        in_specs=[pl.BlockSpec((tq, d), lambda b, h, rb: (b * nrb + rb, h)),
                  pl.BlockSpec((n_tok, d), lambda b, h, rb: (b, n_heads + h)),
                  pl.BlockSpec((n_tok, d), lambda b, h, rb: (b, 2 * n_heads + h)),
                  pl.BlockSpec((n_ctx, d), lambda b, h, rb: (cblk + b, n_heads + h)),
                  pl.BlockSpec((n_ctx, d), lambda b, h, rb: (cblk + b, 2 * n_heads + h)),
                  pl.BlockSpec((None, NA_DY + 2, GRID_W, 2 * GRID_W), lambda b, h, rb: (h, 0, 0, 0))],
        out_specs=pl.BlockSpec((tq, d), lambda b, h, rb: (b * nrb + rb, h)),
        out_shape=jax.ShapeDtypeStruct((n_samples * n_tok, n_heads * d), BF16),
        compiler_params=_cparams(3, 40), name="na_attn",
    )(z1, z1, z1, z1, z1, bias)


def _ctx_attn_kernel(q_ref, k_ref, v_ref, o_ref, *, scale):
    (p,), l = _softmax_parts([_dot_nt(q_ref[...], k_ref[...]) * scale])
    o_ref[...] = _dot((p * (1.0 / l)).astype(BF16), v_ref[...]).astype(BF16)


def na_ctx_attention(z1, n_samples, n_tok, n_ctx, n_heads):
    d = HEAD_DIM
    cblk = n_samples * n_tok // n_ctx
    return pl.pallas_call(
        functools.partial(_ctx_attn_kernel, scale=d ** -0.5),
        grid=(n_samples, n_heads),
        in_specs=[pl.BlockSpec((n_ctx, d), lambda b, h: (cblk + b, h)),
                  pl.BlockSpec((n_ctx, d), lambda b, h: (cblk + b, n_heads + h)),
                  pl.BlockSpec((n_ctx, d), lambda b, h: (cblk + b, 2 * n_heads + h))],
        out_specs=pl.BlockSpec((n_ctx, d), lambda b, h: (b, h)),
        out_shape=jax.ShapeDtypeStruct((n_samples * n_ctx, n_heads * d), BF16),
        compiler_params=_cparams(2, 16), name="na_attn_ctx",
    )(z1, z1, z1)


def _hgrn_chunk(q_raw, f_raw, v, lb, st_ref, reverse):
    C, HW = q_raw.shape
    dk = HEAD_DIM
    n_heads = HW // dk
    q = _silu(q_raw) * (dk ** -0.5)
    f = lb + (1.0 - lb) * jax.nn.sigmoid(f_raw)
    g = jnp.maximum(jnp.log(f), -200.0)
    k = 1.0 - f
    t_idx = lax.broadcasted_iota(jnp.int32, (C, 1), 0)
    p = (C - 1 - t_idx) if reverse else t_idx

    def prev(x, s):
        return pltpu.roll(x, (C - s) if reverse else s, 0)

    def nxt(x, s):
        return pltpu.roll(x, s if reverse else (C - s), 0)

    def incl_scan(b):
        x = g
        pb = jnp.bitwise_and(p, b - 1)
        s = 1
        while s < b:
            x = x + jnp.where(pb >= s, prev(x, s), 0.0)
            s *= 2
        return x

    def excl_rscan(b):
        if b == 1:
            return jnp.zeros_like(g)
        pb = jnp.bitwise_and(p, b - 1)
        x = jnp.where(pb <= b - 2, nxt(g, 1), 0.0)
        s = 1
        while s < b:
            x = x + jnp.where(pb + s <= b - 1, nxt(x, s), 0.0)
            s *= 2
        return x

    cum = incl_scan(C)
    last = cum[0:1] if reverse else cum[C - 1:C]
    q_in = (q * jnp.exp(cum)).astype(BF16)
    k_out = (k * jnp.exp(last - cum)).astype(BF16)
    total = jnp.exp(last)
    qb, kb, vb = q.astype(BF16), k.astype(BF16), v.astype(BF16)

    def boundary_rows(b):
        rows = []
        for j in range(C // (2 * b)):
            r = 2 * b * j + (b if reverse else b - 1)
            rows.append(jnp.broadcast_to(cum[r:r + 1], (2 * b, HW)))
        return jnp.concatenate(rows, axis=0)

    levels = []
    b = C // 2
    while b >= 1:
        upper = jnp.bitwise_and(p, b) != 0
        if b >= 8:
            x = cum - boundary_rows(b)
            e_up, e_lo = jnp.minimum(x, 0.0), jnp.minimum(-x, 0.0)
        else:
            e_up, e_lo = incl_scan(b), excl_rscan(b)
        ql = jnp.where(upper, q * jnp.exp(e_up), 0.0).astype(BF16)
        kl = jnp.where(upper, 0.0, k * jnp.exp(e_lo)).astype(BF16)
        levels.append((b, ql, kl))
        b //= 2

    s_idx = lax.broadcasted_iota(jnp.int32, (1, C), 1)
    ps = (C - 1 - s_idx) if reverse else s_idx
    outs = []
    for h in range(n_heads):
        sl = slice(h * dk, (h + 1) * dk)
        st_old = st_ref[h]
        att = jnp.where(p == ps, _dot_nt(qb[:, sl], kb[:, sl]), 0.0)
        for b, ql, kl in levels:
            same_pair = jnp.bitwise_and(p, -2 * b) == jnp.bitwise_and(ps, -2 * b)
            att = att + jnp.where(same_pair, _dot_nt(ql[:, sl], kl[:, sl]), 0.0)
        outs.append(_dot_nt(q_in[:, sl], st_old.astype(BF16)) + _dot(att.astype(BF16), vb[:, sl]))
        st_ref[h] = total[:, sl] * st_old + _dot_tn(vb[:, sl], k_out[:, sl])
    return jnp.concatenate(outs, axis=1)


def _hgrn_kernel(qf_ref, ff_ref, if_ref, qb_ref, fb_ref, ib_ref, lb_ref, of_ref, ob_ref, sf_ref, sb_ref):
    @pl.when(pl.program_id(1) == 0)
    def _():
        sf_ref[...] = jnp.zeros_like(sf_ref)
        sb_ref[...] = jnp.zeros_like(sb_ref)

    lb = lb_ref[...]
    of_ref[...] = _hgrn_chunk(qf_ref[...], ff_ref[...], if_ref[...], lb, sf_ref, reverse=False)
    ob_ref[...] = _hgrn_chunk(qb_ref[...], fb_ref[...], ib_ref[...], lb, sb_ref, reverse=True)


def hgrn_scan(z2, lb, n_samples, n_tok, n_ctx):
    M = z2.shape[0]
    HW = z2.shape[1] // 5
    C = HGRN_CHUNK
    ncc, ncl = n_ctx // C, n_tok // C
    lat_blocks = n_samples * ncl

    def fwd_row(b, c):
        return jnp.where(c < ncc, lat_blocks + b * ncc + c, b * ncl + (c - ncc))

    def bwd_row(b, c):
        return jnp.where(c < ncc, lat_blocks + b * ncc + (ncc - 1 - c), b * ncl + (ncl - 1 - (c - ncc)))

    spec = lambda rowf, col: pl.BlockSpec((C, HW), lambda b, c: (rowf(b, c), col))
    return pl.pallas_call(
        _hgrn_kernel, grid=(n_samples, ncc + ncl),
        in_specs=[spec(fwd_row, 0), spec(fwd_row, 1), spec(fwd_row, 3),
                  spec(bwd_row, 0), spec(bwd_row, 2), spec(bwd_row, 3),
                  pl.BlockSpec((1, HW), lambda b, c: (0, 0))],
        out_specs=[spec(fwd_row, 0), spec(bwd_row, 0)],
        out_shape=[jax.ShapeDtypeStruct((M, HW), F32)] * 2,
        scratch_shapes=[pltpu.VMEM((HW // HEAD_DIM, HEAD_DIM, HEAD_DIM), F32)] * 2,
        compiler_params=_cparams(2, 48), name="hgrn_scan",
    )(z2, z2, z2, z2, z2, z2, lb.reshape(1, HW))


def _hgrn_readout_kernel(of_ref, ob_ref, gz_ref, g_ref, o_ref):
    for h in range(of_ref.shape[1] // HEAD_DIM):
        sl = slice(h * HEAD_DIM, (h + 1) * HEAD_DIM)
        x = of_ref[:, sl] + ob_ref[:, sl]
        y = x * lax.rsqrt(jnp.mean(x * x, axis=-1, keepdims=True) + RMS_EPS) * g_ref[...]
        o_ref[:, sl] = (y * _silu(gz_ref[:, sl])).astype(BF16)


def hgrn_readout(o_f, o_b, z2, norm_g, n_rows):
    HW = o_f.shape[1]
    tm = ROW_TILE
    return pl.pallas_call(
        _hgrn_readout_kernel, grid=(n_rows // tm,),
        in_specs=[pl.BlockSpec((tm, HW), lambda i: (i, 0)), pl.BlockSpec((tm, HW), lambda i: (i, 0)),
                  pl.BlockSpec((tm, HW), lambda i: (i, 4)), pl.BlockSpec((1, HEAD_DIM), lambda i: (0, 0))],
        out_specs=pl.BlockSpec((tm, HW), lambda i: (i, 0)),
        out_shape=jax.ShapeDtypeStruct((n_rows, HW), BF16),
        compiler_params=_cparams(1, 32), name="hgrn_readout",
    )(o_f, o_b, z2, norm_g.reshape(1, HEAD_DIM))


def _merge_kernel(h_ref, oa_ref, od_ref, om_ref, or_ref, wg_ref, bg_ref, wb_ref, o_ref, wg_bf, wb_bf):
    @pl.when(pl.program_id(1) == 0)
    def _():
        wg_bf[...] = wg_ref[...].astype(BF16)
        wb_bf[...] = wb_ref[...].astype(BF16)

    h = h_ref[...]
    acc = None
    for j, o_j in enumerate((oa_ref, od_ref, om_ref, or_ref)):
        gate = jax.nn.sigmoid(_dot(h, wg_bf[j]) + bg_ref[j])
        term = gate * _dot(o_j[...], wb_bf[j])
        acc = term if acc is None else acc + term
    o_ref[...] = acc.astype(BF16)


def merge_branches(h, outs, w_gate, b_gate, w_branch, layer, n_rows):
    D = h.shape[1]
    L, nb, BW = w_branch.shape[:3]
    tm, tn = ROW_TILE, 256
    single = dict(pipeline_mode=pl.Buffered(1))
    return pl.pallas_call(
        _merge_kernel, grid=(D // tn, n_rows // tm),
        in_specs=[pl.BlockSpec((tm, D), lambda j, i: (i, 0))]
                 + [pl.BlockSpec((tm, BW), lambda j, i: (i, 0))] * nb
                 + [pl.BlockSpec((None, nb, D, tn), lambda j, i: (layer, 0, 0, j), **single),
                    pl.BlockSpec((None, nb, 1, tn), lambda j, i: (layer, 0, 0, j)),
                    pl.BlockSpec((None, nb, BW, tn), lambda j, i: (layer, 0, 0, j), **single)],
        out_specs=pl.BlockSpec((tm, tn), lambda j, i: (i, j)),
        out_shape=jax.ShapeDtypeStruct((n_rows, D), BF16),
        scratch_shapes=[pltpu.VMEM((nb, D, tn), BF16), pltpu.VMEM((nb, BW, tn), BF16)],
        compiler_params=_cparams(2, 58), name="merge",
    )(h, *outs, w_gate, b_gate.reshape(L, nb, 1, D), w_branch)


def _ffn_in_kernel(x_ref, w1_ref, w3_ref, o_ref, w1_bf, w3_bf):
    @pl.when(pl.program_id(1) == 0)
    def _():
        w1_bf[...] = w1_ref[...].astype(BF16)
        w3_bf[...] = w3_ref[...].astype(BF16)

    x = x_ref[...]
    o_ref[...] = (_silu(_dot(x, w1_bf[...])) * _dot(x, w3_bf[...])).astype(BF16)


def ffn_in(h, w1, w3, layer, n_rows):
    _, D, Fd = w1.shape
    tm, tn = 2 * ROW_TILE, 256
    w_spec = pl.BlockSpec((None, D, tn), lambda j, i: (layer, 0, j))
    return pl.pallas_call(
        _ffn_in_kernel,
        grid=(Fd // tn, pl.cdiv(n_rows, tm)),
        in_specs=[pl.BlockSpec((tm, D), lambda j, i: (i, 0)), w_spec, w_spec],
        out_specs=pl.BlockSpec((tm, tn), lambda j, i: (i, j)),
        out_shape=jax.ShapeDtypeStruct((n_rows, Fd), BF16),
        scratch_shapes=[pltpu.VMEM((D, tn), BF16)] * 2,
        compiler_params=_cparams(2, 50), name="ffn_in",
    )(h, w1, w3)


def _group_spec(tm, tn, n_lat, n_samples):
    return pl.BlockSpec((None, 1, tn), lambda j, i: (_grp_of_tile(i, tm, n_lat, n_samples), 0, j))


def ffn_out(u, w2, layer, xres, gate, n_rows, n_lat, n_samples):
    Fd, D = w2.shape[1:]
    G = gate.shape[0]
    tm, tn = 256, 512
    return _mm_call(u, w2, layer, Fd, n_rows=n_rows, n_blocks=D // tn, tn=tn, tm=tm, out_dtype=F32,
                    epilogue=_residual_epilogue, extras=(xres, gate.reshape(G, 1, D)),
                    extra_specs=(pl.BlockSpec((tm, tn), lambda j, i: (i, j)), _group_spec(tm, tn, n_lat, n_samples)),
                    w_single_buffer=True, name="ffn_out")


MOE_TILE = 256


def moe_route_plan(comb, n_experts):
    M = comb.shape[0]
    T = MOE_TILE
    w = comb[:, :n_experts]
    sel = w > 0
    n_tiles = 2 * M // T + n_experts
    R = n_tiles * T
    cnt = jnp.sum(sel, axis=0, dtype=jnp.int32)
    rank = jnp.cumsum(sel, axis=0, dtype=jnp.int32) - 1
    gsz = (cnt + T - 1) // T * T
    gend = jnp.cumsum(gsz)
    off = gend - gsz
    dest_all = off[None, :] + rank
    tile_start = jnp.arange(n_tiles, dtype=jnp.int32) * T
    tile_expert = jnp.minimum(jnp.searchsorted(gend, tile_start, side="right"), n_experts - 1).astype(jnp.int32)
    n_used = (gend[-1] // T).reshape(1).astype(jnp.int32)
    e_lo = jnp.argmax(sel, axis=1)
    e_hi = n_experts - 1 - jnp.argmax(sel[:, ::-1], axis=1)
    take = lambda a, e: jnp.take_along_axis(a, e[:, None], axis=1)[:, 0]
    two = e_hi != e_lo
    dest = jnp.stack([take(dest_all, e_lo), take(dest_all, e_hi)], axis=1).astype(jnp.int32)
    tok = lax.broadcasted_iota(jnp.int32, dest.shape, 0)
    src_tok = jnp.zeros((R,), jnp.int32).at[dest.reshape(-1)].set(tok.reshape(-1))
    pw = jnp.zeros((M, 128), F32).at[:, 0].set(take(w, e_lo)).at[:, 1].set(jnp.where(two, take(w, e_hi), 0.0))
    return src_tok, tile_expert, n_used, dest, pw


def _row_copy(src_hbm, row, dst_buf, slot, r, sem):
    return pltpu.make_async_copy(src_hbm.at[pl.ds(row, 1)], dst_buf.at[slot, pl.ds(r, 1)], sem.at[slot])


def _gather_tile(idx_of_row, src_hbm, buf, sem, n_rows_tile):
    i = pl.program_id(0)
    n = pl.num_programs(0)

    def start_tile(t, slot):
        def body(r2, carry):
            for pri in range(2):
                r = 2 * r2 + pri
                _row_copy(src_hbm, idx_of_row(t * n_rows_tile + r), buf, slot, r, sem).start(priority=pri)
            return carry
        lax.fori_loop(0, n_rows_tile // 2, body, 0, unroll=4)

    @pl.when(i == 0)
    def _():
        start_tile(0, 0)

    @pl.when(i + 1 < n)
    def _():
        start_tile(i + 1, (i + 1) % 2)

    slot = i % 2

    def wait_body(r, carry):
        _row_copy(src_hbm, 0, buf, slot, r, sem).wait()
        return carry
    lax.fori_loop(0, n_rows_tile, wait_body, 0, unroll=8)
    return slot


def _moe_gather_kernel(src_ref, h_hbm, o_ref, buf, sem):
    slot = _gather_tile(lambda r: src_ref[r], h_hbm, buf, sem, o_ref.shape[0])
    o_ref[...] = buf[slot].astype(BF16)


def moe_gather(hf, src_tok):
    R = src_tok.shape[0]
    D = hf.shape[1]
    T = MOE_TILE
    return pl.pallas_call(
        _moe_gather_kernel,
        grid_spec=pltpu.PrefetchScalarGridSpec(
            num_scalar_prefetch=1, grid=(R // T,),
            in_specs=[pl.BlockSpec(memory_space=pl.ANY)],
            out_specs=pl.BlockSpec((T, D), lambda i, src: (i, 0)),
            scratch_shapes=[pltpu.VMEM((2, T, D), F32), pltpu.SemaphoreType.DMA((2,))]),
        out_shape=jax.ShapeDtypeStruct((R, D), BF16),
        compiler_params=_cparams(1, 24), name="moe_gather",
    )(src_tok, hf)


def _moe_in_kernel(te_ref, nu_ref, x_ref, w1_ref, w3_ref, o_ref, w1_bf, w3_bf):
    i = pl.program_id(1)
    fresh = jnp.logical_or(i == 0, te_ref[i] != te_ref[jnp.maximum(i - 1, 0)])

    @pl.when(fresh)
    def _():
        w1_bf[...] = w1_ref[...].astype(BF16)
        w3_bf[...] = w3_ref[...].astype(BF16)

    @pl.when(i < nu_ref[0])
    def _():
        x = x_ref[...]
        o_ref[...] = (_silu(_dot(x, w1_bf[...])) * _dot(x, w3_bf[...])).astype(BF16)

    @pl.when(i >= nu_ref[0])
    def _():
        o_ref[...] = jnp.zeros_like(o_ref)


def moe_ffn_in(xs, w1, w3, e0, tile_expert, n_used):
    R, D = xs.shape
    Fd = w1.shape[2]
    T, tn = MOE_TILE, 512
    w_spec = pl.BlockSpec((None, D, tn), lambda j, i, te, nu: (e0 + te[i], 0, j))
    return pl.pallas_call(
        _moe_in_kernel,
        grid_spec=pltpu.PrefetchScalarGridSpec(
            num_scalar_prefetch=2, grid=(Fd // tn, R // T),
            in_specs=[pl.BlockSpec((T, D), lambda j, i, te, nu: (i, 0)), w_spec, w_spec],
            out_specs=pl.BlockSpec((T, tn), lambda j, i, te, nu: (i, j)),
            scratch_shapes=[pltpu.VMEM((D, tn), BF16)] * 2),
        out_shape=jax.ShapeDtypeStruct((R, Fd), BF16),
        compiler_params=_cparams(2, 54), name="moe_ffn_in",
    )(tile_expert, n_used, xs, w1, w3)


def _moe_out_kernel(te_ref, nu_ref, u_ref, w2_ref, o_ref, w2_bf):
    i = pl.program_id(1)
    fresh = jnp.logical_or(i == 0, te_ref[i] != te_ref[jnp.maximum(i - 1, 0)])

    @pl.when(fresh)
    def _():
        w2_bf[...] = w2_ref[...].astype(BF16)

    @pl.when(i < nu_ref[0])
    def _():
        o_ref[...] = _dot(u_ref[...], w2_bf[...])

    @pl.when(i >= nu_ref[0])
    def _():
        o_ref[...] = jnp.zeros_like(o_ref)


def moe_ffn_out(u, w2, e0, tile_expert, n_used):
    R, Fd = u.shape
    D = w2.shape[2]
    T, tn = MOE_TILE, 1024
    return pl.pallas_call(
        _moe_out_kernel,
        grid_spec=pltpu.PrefetchScalarGridSpec(
            num_scalar_prefetch=2, grid=(D // tn, R // T),
            in_specs=[pl.BlockSpec((T, Fd), lambda j, i, te, nu: (i, 0)),
                      pl.BlockSpec((None, Fd, tn), lambda j, i, te, nu: (e0 + te[i], 0, j))],
            out_specs=pl.BlockSpec((T, tn), lambda j, i, te, nu: (i, j)),
            scratch_shapes=[pltpu.VMEM((Fd, tn), BF16)]),
        out_shape=jax.ShapeDtypeStruct((R, D), F32),
        compiler_params=_cparams(2, 40), name="moe_ffn_out",
    )(tile_expert, n_used, u, w2)


def _moe_combine_kernel(dest_ref, y_hbm, pw_ref, x_ref, g_ref, o_ref, buf, sem):
    T = o_ref.shape[0]
    slot = _gather_tile(lambda r: dest_ref[r], y_hbm, buf, sem, 2 * T)
    pw = pw_ref[...]
    mix = pw[:, 0:1] * buf[slot, 0:T, :] + pw[:, 1:2] * buf[slot, T:2 * T, :]
    o_ref[...] = x_ref[...] + g_ref[...] * mix


def moe_combine(y, dest, pw, xres, gate, n_lat, n_samples):
    M, D = xres.shape
    G = gate.shape[0]
    T = 128
    dest = dest.reshape(M // T, T, 2).transpose(0, 2, 1)
    return pl.pallas_call(
        _moe_combine_kernel,
        grid_spec=pltpu.PrefetchScalarGridSpec(
            num_scalar_prefetch=1, grid=(M // T,),
            in_specs=[pl.BlockSpec(memory_space=pl.ANY),
                      pl.BlockSpec((T, 128), lambda i, d: (i, 0)),
                      pl.BlockSpec((T, D), lambda i, d: (i, 0)),
                      pl.BlockSpec((None, 1, D), lambda i, d: (_grp_of_tile(i, T, n_lat, n_samples), 0, 0))],
            out_specs=pl.BlockSpec((T, D), lambda i, d: (i, 0)),
            scratch_shapes=[pltpu.VMEM((2, 2 * T, D), F32), pltpu.SemaphoreType.DMA((2,))]),
        out_shape=jax.ShapeDtypeStruct((M, D), F32),
        compiler_params=_cparams(1, 32), name="moe_combine",
    )(dest.reshape(-1), y, pw, xres, gate.reshape(G, 1, D))


def kernel(x, c, ctx, c_ctx, norm1_g, norm2_g, w_ada, b_ada, w_in, na_rpb, diff_lambda, diff_subln_g,
           mla_q_norm_g, mla_w_q_up, mla_kv_norm_g, mla_w_kv_up, hgrn_lower_bounds, hgrn_norm_g,
           w_branch, w_gate, b_gate, w_out, ffn_w1, ffn_w3, ffn_w2, moe_router, moe_w1, moe_w3, moe_w2,
           final_norm_g):
    B, N, D = x.shape
    NC = ctx.shape[1]
    L = w_ada.shape[0]
    n_lat, n_all = B * N, B * N + B * NC
    BW = w_branch.shape[2]
    n_heads = BW // HEAD_DIM
    HG0 = w_in.shape[2] - 5 * BW
    assert N % ROW_TILE == 0 and (B * NC) % ROW_TILE == 0 and NC % HGRN_CHUNK == 0

    xs = jnp.concatenate([x.reshape(n_lat, D), ctx.reshape(B * NC, D)], axis=0)

    cond8 = jnp.zeros((8, D), F32).at[:B].set(c).at[B].set(c_ctx)
    mods = adaln_all(cond8, w_ada, b_ada)[:, :B + 1].reshape(L, B + 1, 6, D)

    lb_all = jnp.cumsum(jax.nn.softmax(hgrn_lower_bounds.astype(F32), axis=0), axis=0)
    lb_all = lb_all - lb_all[0:1]

    cosd, sind = _rope_tables(N, B * NC, 2 * 64, ROW_TILE)
    cosm, sinm = _rope_tables(N, B * NC, 64, ROW_TILE)

    RQ, RKV = mla_w_q_up.shape[1], mla_w_kv_up.shape[1]
    wq4 = mla_w_q_up.reshape(L, RQ, n_heads, HEAD_DIM + MLA_ROPE)
    wq_cat = jnp.pad(wq4, ((0, 0), (0, 0), (0, 0), (0, MLA_SLOT - HEAD_DIM - MLA_ROPE))).reshape(L, RQ, -1)
    wkv4 = mla_w_kv_up.reshape(L, RKV, n_heads, 2 * HEAD_DIM)
    wkn = wkv4[..., :HEAD_DIM].reshape(L, RKV, -1)
    wvt = jnp.swapaxes(wkv4[..., HEAD_DIM:].reshape(L, RKV, -1), 1, 2)

    for l in range(L):
        need_ctx = l < L - 1
        n_rows = n_all if need_ctx else n_lat
        lambda_init = 0.8 - 0.6 * math.exp(-0.3 * l)
        sh1, sc1, g1, sh2, sc2, g2 = (mods[l, :, k] for k in range(6))

        h = norm_mod(xs, norm1_g[l], sh1, sc1, N, B)
        z1 = _mm_call(h, w_in, l, D, n_rows=n_all, n_blocks=15, tn=512, tm=2 * ROW_TILE, out_dtype=BF16,
                      name="w_in_attn")
        z2 = shifted_matmul(h, w_in, l, HG0, 5 * BW, F32, "w_in_hgrn")

        bias = _na_bias_tables(na_rpb[l])
        o_a = na_attention(z1, bias, B, N, NC, n_heads)
        dq, dk, dvt = diff_rope(z1, cosd, sind, N, n_lat)
        lam_rows = jnp.zeros((8, HEAD_DIM), F32).at[:4].set(diff_lambda[l]).at[4].set(lambda_init)
        o_d = diff_attention(dq, dk, dvt, lam_rows, diff_subln_g[l], lambda_init, B, N, NC, False)
        qcat, kcat, mvt = mla_prep(z1, cosm, sinm, mla_q_norm_g[l], wq_cat[l], mla_kv_norm_g[l], wkn[l], wvt[l],
                                   N, n_lat)
        o_m = mla_attention(qcat, kcat, mvt, B, N, NC, False)
        o_f, o_b = hgrn_scan(z2, lb_all[l], B, N, NC)
        o_r = hgrn_readout(o_f, o_b, z2, hgrn_norm_g[l], n_rows)
        if need_ctx:
            o_a = jnp.concatenate([o_a, na_ctx_attention(z1, B, N, NC, n_heads)], axis=0)
            o_d = jnp.concatenate([o_d, diff_attention(dq, dk, dvt, lam_rows, diff_subln_g[l], lambda_init,
                                                       B, N, NC, True)], axis=0)
            o_m = jnp.concatenate([o_m, mla_attention(qcat, kcat, mvt, B, N, NC, True)], axis=0)

        s = merge_branches(h, (o_a, o_d, o_m, o_r), w_gate, b_gate, w_branch, l, n_rows)
        G = B + 1
        tmo = 2 * ROW_TILE
        xs = _mm_call(s, w_out, l, D, n_rows=n_rows, n_blocks=D // 512, tn=512, tm=tmo, out_dtype=F32,
                      epilogue=_residual_epilogue, extras=(xs, g1.reshape(G, 1, D)),
                      extra_specs=(pl.BlockSpec((tmo, 512), lambda j, i: (i, j)), _group_spec(tmo, 512, N, B)),
                      name="w_out")

        j = l // 2
        if l % 2 == 0:
            h2 = norm_mod(xs, norm2_g[l], sh2, sc2, N, B)
            u = ffn_in(h2, ffn_w1, ffn_w3, j, n_rows)
            xs = ffn_out(u, ffn_w2, j, xs, g2, n_rows, N, B)
        else:
            hf, comb = norm_mod(xs, norm2_g[l], sh2, sc2, N, B, w_router=moe_router[j])
            NM, E, _, FE = moe_w1.shape
            src_tok, tile_expert, n_used, dest, pw = moe_route_plan(comb, E)
            xg = moe_gather(hf, src_tok)
            u = moe_ffn_in(xg, moe_w1.reshape(NM * E, D, FE), moe_w3.reshape(NM * E, D, FE), j * E, tile_expert, n_used)
            y = moe_ffn_out(u, moe_w2.reshape(NM * E, FE, D), j * E, tile_expert, n_used)
            xs = moe_combine(y, dest, pw, xs, g2, N, B)

    return final_norm(xs[:n_lat], final_norm_g).reshape(B, N, D)
```

```python
import functools
import math

import numpy as np
import jax
import jax.numpy as jnp
from jax import lax
from jax.experimental import pallas as pl
from jax.experimental.pallas import tpu as pltpu

F32 = jnp.float32
BF16 = jnp.bfloat16

GRID_W = 64
RMS_EPS = 1e-6
ROPE_BASE = 10000.0
NA_KH, NA_KW = 8, 16
HEAD_DIM = 128
MLA_ROPE = 64
HGRN_CHUNK = 64
MOE_EXPERTS = 8
NEG_BIG = -1e30

VMEM_LIMIT_V7X = 60000 * 1024
ROW_TILE = 512


def _cparams(n_axes, vmem_mb):
    return pltpu.CompilerParams(dimension_semantics=("arbitrary",) * n_axes,
                                vmem_limit_bytes=min(int(vmem_mb * 2**20), VMEM_LIMIT_V7X))


def _silu(x):
    return x * jax.nn.sigmoid(x)


def _dot(a, b):
    return jnp.dot(a, b, preferred_element_type=F32)


def _dot_nt(a, b):
    return lax.dot_general(a, b, (((1,), (1,)), ((), ())), preferred_element_type=F32)


def _dot_tn(a, b):
    return lax.dot_general(a, b, (((0,), (0,)), ((), ())), preferred_element_type=F32)


def _adaln_kernel(c_ref, w_ref, b_ref, o_ref):
    @pl.when(pl.program_id(1) == 0)
    def _():
        o_ref[...] = jnp.broadcast_to(b_ref[...], o_ref.shape)

    x = _silu(c_ref[...]).astype(BF16)
    o_ref[...] += _dot(x, w_ref[...].astype(BF16))


def adaln_all(cond8, w_ada, b_ada):
    L, D, N6 = w_ada.shape
    tk = 128
    cond_k = cond8.reshape(8, D // tk, tk).transpose(1, 0, 2)
    return pl.pallas_call(
        _adaln_kernel,
        grid=(L, D // tk),
        in_specs=[pl.BlockSpec((None, 8, tk), lambda l, k: (k, 0, 0)),
                  pl.BlockSpec((None, tk, N6), lambda l, k: (l, k, 0)),
                  pl.BlockSpec((None, 1, N6), lambda l, k: (l, 0, 0))],
        out_specs=pl.BlockSpec((None, 8, N6), lambda l, k: (l, 0, 0)),
        out_shape=jax.ShapeDtypeStruct((L, 8, N6), F32),
        compiler_params=_cparams(2, 48),
        name="adaln",
    )(cond_k, w_ada, b_ada.reshape(L, 1, N6))


def _norm_mod_kernel(x_ref, g_ref, sh_ref, sc_ref, *rest, with_router):
    x = x_ref[...]
    y = x * lax.rsqrt(jnp.mean(x * x, axis=-1, keepdims=True) + RMS_EPS) * g_ref[...]
    h = y * (1.0 + sc_ref[...]) + sh_ref[...]
    if not with_router:
        (h_ref,) = rest
        h_ref[...] = h.astype(BF16)
        return
    wr_ref, h_ref, comb_ref = rest
    h_ref[...] = h
    logits = jnp.dot(h, wr_ref[...], precision=lax.Precision.HIGHEST, preferred_element_type=F32)
    lane = lax.broadcasted_iota(jnp.int32, logits.shape, 1).astype(F32)
    logits = jnp.where(lane < MOE_EXPERTS, logits, -jnp.inf)
    m1 = jnp.max(logits, axis=-1, keepdims=True)
    i1 = jnp.min(jnp.where(logits == m1, lane, 128.0), axis=-1, keepdims=True)
    rest_l = jnp.where(lane == i1, -jnp.inf, logits)
    m2 = jnp.max(rest_l, axis=-1, keepdims=True)
    i2 = jnp.min(jnp.where(rest_l == m2, lane, 128.0), axis=-1, keepdims=True)
    e2 = jnp.exp(m2 - m1)
    p1 = 1.0 / (1.0 + e2)
    p2 = e2 / (1.0 + e2)
    comb_ref[...] = jnp.where(lane == i1, p1, 0.0) + jnp.where(lane == i2, p2, 0.0)


def _grp_of_tile(i, tm, lat_rows_per_sample, n_samples):
    return jnp.minimum((i * tm) // lat_rows_per_sample, n_samples)


def norm_mod(x, g, shift, scale, n_lat, n_samples, w_router=None):
    M, D = x.shape
    tm = 256
    G = shift.shape[0]
    grp = lambda i: (_grp_of_tile(i, tm, n_lat, n_samples), 0, 0)
    in_specs = [pl.BlockSpec((tm, D), lambda i: (i, 0)),
                pl.BlockSpec((1, D), lambda i: (0, 0)),
                pl.BlockSpec((None, 1, D), grp),
                pl.BlockSpec((None, 1, D), grp)]
    args = [x, g.reshape(1, D), shift.reshape(G, 1, D), scale.reshape(G, 1, D)]
    out_specs = [pl.BlockSpec((tm, D), lambda i: (i, 0))]
    out_shape = [jax.ShapeDtypeStruct((M, D), BF16 if w_router is None else F32)]
    if w_router is not None:
        wr = jnp.zeros((D, 128), F32).at[:, :w_router.shape[1]].set(w_router)
        in_specs.append(pl.BlockSpec((D, 128), lambda i: (0, 0)))
        args.append(wr)
        out_specs.append(pl.BlockSpec((tm, 128), lambda i: (i, 0)))
        out_shape.append(jax.ShapeDtypeStruct((M, 128), F32))
    outs = pl.pallas_call(
        functools.partial(_norm_mod_kernel, with_router=w_router is not None),
        grid=(M // tm,),
        in_specs=in_specs, out_specs=out_specs, out_shape=out_shape,
        compiler_params=_cparams(1, 40),
        name="norm_mod_router" if w_router is not None else "norm_mod",
    )(*args)
    return outs if w_router is not None else outs[0]


def _final_norm_kernel(x_ref, g_ref, o_ref):
    x = x_ref[...]
    o_ref[...] = x * lax.rsqrt(jnp.mean(x * x, axis=-1, keepdims=True) + RMS_EPS) * g_ref[...]


def final_norm(x, g):
    M, D = x.shape
    tm = 256
    return pl.pallas_call(
        _final_norm_kernel, grid=(M // tm,),
        in_specs=[pl.BlockSpec((tm, D), lambda i: (i, 0)), pl.BlockSpec((1, D), lambda i: (0, 0))],
        out_specs=pl.BlockSpec((tm, D), lambda i: (i, 0)),
        out_shape=jax.ShapeDtypeStruct((M, D), F32),
        compiler_params=_cparams(1, 40), name="final_norm",
    )(x, g.reshape(1, D))


def _cast_rows(src_ref, dst_ref, rows=512):
    n = src_ref.shape[0]
    for r0 in range(0, n, rows):
        r1 = min(r0 + rows, n)
        dst_ref[r0:r1, :] = src_ref[r0:r1, :].astype(BF16)


def _stream_mm_kernel(x_ref, w_hbm, *rest, layer, n_extra, epilogue):
    extras = rest[:n_extra]
    o_ref, stage, wbf, sem = rest[n_extra:]
    j, i, nj = pl.program_id(0), pl.program_id(1), pl.num_programs(0)
    tn = stage.shape[1]

    def block_copy(jj):
        return pltpu.make_async_copy(w_hbm.at[layer, :, pl.ds(pl.multiple_of(jj * tn, tn), tn)], stage, sem)

    @pl.when(i == 0)
    def _():
        @pl.when(j == 0)
        def _():
            block_copy(0).start()

        block_copy(j).wait()
        _cast_rows(stage, wbf)

        @pl.when(j + 1 < nj)
        def _():
            block_copy(j + 1).start()

    acc = _dot(x_ref[...], wbf[...])
    if epilogue is not None:
        acc = epilogue(acc, *[e[...] for e in extras])
    o_ref[...] = acc.astype(o_ref.dtype)


def _mm_call(x, w, layer, *, n_rows, tn, tm, out_dtype, epilogue=None, extras=(), extra_specs=(), name="mm"):
    K, N = w.shape[1:]
    in_specs = [pl.BlockSpec((tm, K), lambda j, i: (i, 0)), pl.BlockSpec(memory_space=pl.ANY)]
    in_specs += list(extra_specs)
    out_bytes = jnp.dtype(out_dtype).itemsize
    vmem = (2 * tm * K * 2 + K * tn * 4 + K * tn * 2 + 2 * tm * tn * out_bytes + 3 * tm * tn * 4
            + len(extras) * 2 * tm * tn * 4) / 2**20 + 6
    return pl.pallas_call(
        functools.partial(_stream_mm_kernel, layer=layer, n_extra=len(extras), epilogue=epilogue),
        grid=(N // tn, pl.cdiv(n_rows, tm)),
        in_specs=in_specs,
        out_specs=pl.BlockSpec((tm, tn), lambda j, i: (i, j)),
        out_shape=jax.ShapeDtypeStruct((n_rows, N), out_dtype),
        scratch_shapes=[pltpu.VMEM((K, tn), F32), pltpu.VMEM((K, tn), BF16), pltpu.SemaphoreType.DMA(())],
        compiler_params=_cparams(2, vmem),
        name=name,
    )(x, w, *extras)


def _residual_epilogue(acc, xres, gate):
    return xres + gate * acc


def _wt_stream_mm_kernel(x_ref, wt_hbm, o_ref, stage, wbf, sem, *, layer, row0):
    j, i, nj = pl.program_id(0), pl.program_id(1), pl.num_programs(0)
    tn, K = stage.shape

    def block_copy(jj):
        return pltpu.make_async_copy(wt_hbm.at[layer, pl.ds(row0 + jj * tn, tn), :], stage, sem)

    @pl.when(i == 0)
    def _():
        @pl.when(j == 0)
        def _():
            block_copy(0).start()

        block_copy(j).wait()
        for c in range(0, K, tn):
            wbf[c:c + tn, :] = jnp.transpose(stage[:, c:c + tn]).astype(BF16)

        @pl.when(j + 1 < nj)
        def _():
            block_copy(j + 1).start()

    o_ref[...] = _dot(x_ref[...], wbf[...]).astype(o_ref.dtype)


def wt_stream_matmul(x, wt, layer, row0, n_cols, out_dtype, name):
    M, K = x.shape
    tm, tn = 2 * ROW_TILE, 512
    assert n_cols % tn == 0 and K % tn == 0 and row0 % 8 == 0
    return pl.pallas_call(
        functools.partial(_wt_stream_mm_kernel, layer=layer, row0=row0),
        grid=(n_cols // tn, pl.cdiv(M, tm)),
        in_specs=[pl.BlockSpec((tm, K), lambda j, i: (i, 0)), pl.BlockSpec(memory_space=pl.ANY)],
        out_specs=pl.BlockSpec((tm, tn), lambda j, i: (i, j)),
        out_shape=jax.ShapeDtypeStruct((M, n_cols), out_dtype),
        scratch_shapes=[pltpu.VMEM((tn, K), F32), pltpu.VMEM((K, tn), BF16), pltpu.SemaphoreType.DMA(())],
        compiler_params=_cparams(2, 48), name=name,
    )(x, wt)


def _rope_tables(n_tokens, n_ident, group, tile_rows):
    half = group // 2
    q = half // 2
    pos = np.arange(n_tokens)
    row, col = pos // GRID_W, pos % GRID_W
    freqs = ROPE_BASE ** (-(np.arange(q, dtype=np.float64) / q))
    lane = np.arange(128)
    in_group = lane % group
    axis_pos = np.where((in_group < half)[None, :], row[:, None], col[:, None]).astype(np.float64)
    ang = axis_pos * freqs[(in_group % half) % q][None, :]
    sign = np.where((in_group % half) < q, -1.0, 1.0)[None, :]
    cos, sin = np.cos(ang), np.sin(ang) * sign
    n_pad = -(-n_ident // tile_rows) * tile_rows
    cos = np.concatenate([cos, np.ones((n_pad, 128))], axis=0)
    sin = np.concatenate([sin, np.zeros((n_pad, 128))], axis=0)
    return jnp.asarray(cos, F32), jnp.asarray(sin, F32)


def _rope_apply(x, cos, sin, q):
    lane = lax.broadcasted_iota(jnp.int32, x.shape, 1)
    partner = jnp.where((lane % (2 * q)) < q, pltpu.roll(x, 128 - q, 1), pltpu.roll(x, q, 1))
    return x * cos + partner * sin


LOG2E = math.log2(math.e)


def _diff_rope_kernel(q_ref, k_ref, v_ref, cos_ref, sin_ref, qo_ref, ko_ref, vt_ref, *, q_scale):
    vt_ref[...] = jnp.transpose(v_ref[...].astype(F32)).astype(BF16)
    cos, sin = cos_ref[...], sin_ref[...]
    for src, dst, mult in ((q_ref, qo_ref, q_scale), (k_ref, ko_ref, None)):
        for c in range(src.shape[1] // 128):
            sl = slice(c * 128, (c + 1) * 128)
            y = _rope_apply(src[:, sl].astype(F32), cos, sin, 32)
            dst[:, sl] = (y if mult is None else y * mult).astype(BF16)


def _table_block(i, tm, n_tok, n_lat_total):
    return jnp.where(i * tm < n_lat_total, (i * tm % n_tok) // tm, n_tok // tm)


def diff_rope(z1, cosd, sind, n_tok, n_lat_total):
    M = z1.shape[0]
    tm = ROW_TILE
    W = 1024
    tb = lambda i: (_table_block(i, tm, n_tok, n_lat_total), 0)
    return pl.pallas_call(
        functools.partial(_diff_rope_kernel, q_scale=HEAD_DIM ** -0.5 * LOG2E), grid=(M // tm,),
        in_specs=[pl.BlockSpec((tm, W), lambda i: (i, 3)), pl.BlockSpec((tm, W), lambda i: (i, 4)),
                  pl.BlockSpec((tm, W), lambda i: (i, 5)),
                  pl.BlockSpec((tm, 128), tb), pl.BlockSpec((tm, 128), tb)],
        out_specs=[pl.BlockSpec((tm, W), lambda i: (i, 0))] * 2 + [pl.BlockSpec((W, tm), lambda i: (0, i))],
        out_shape=[jax.ShapeDtypeStruct((M, W), BF16)] * 2 + [jax.ShapeDtypeStruct((W, M), BF16)],
        compiler_params=_cparams(1, 40), name="diff_rope",
    )(z1, z1, z1, cosd, sind)


MLA_SLOT = 2 * HEAD_DIM


def _mla_prep_kernel(cq_ref, ckva_ref, ckvb_ref, kr_ref, cos_ref, sin_ref, gq_ref, gkv_ref, wq_ref, wkn_ref,
                     wvt_ref, qcat_ref, kcat_ref, vt_ref, wq_bf, wkn_bf, wvt_bf, *, q_scale):
    @pl.when(pl.program_id(0) == 0)
    def _():
        wq_bf[...] = wq_ref[...].astype(BF16)
        wkn_bf[...] = wkn_ref[...].astype(BF16)
        wvt_bf[...] = wvt_ref[...].astype(BF16)

    cos, sin = cos_ref[...], sin_ref[...]
    d = HEAD_DIM
    n_heads = qcat_ref.shape[1] // MLA_SLOT
    cq = cq_ref[...].astype(F32)
    cqn = (cq * lax.rsqrt(jnp.mean(cq * cq, axis=-1, keepdims=True) + RMS_EPS) * gq_ref[...]).astype(BF16)
    q = _dot(cqn, wq_bf[...])
    a = ckva_ref[...].astype(F32)
    b = ckvb_ref[...].astype(F32)
    ha = a.shape[1]
    ms = (jnp.sum(a * a, axis=-1, keepdims=True) + jnp.sum(b * b, axis=-1, keepdims=True)) / (2 * ha)
    r = lax.rsqrt(ms + RMS_EPS)
    g = gkv_ref[...]
    ckvn = jnp.concatenate([(a * r * g[:, :ha]).astype(BF16), (b * r * g[:, ha:]).astype(BF16)], axis=1)
    kn = _dot(ckvn, wkn_bf[...])
    lane = lax.broadcasted_iota(jnp.int32, cos.shape, 1)
    kr = jnp.where(lane < MLA_ROPE, _rope_apply(kr_ref[...].astype(F32), cos, sin, 16), 0.0).astype(BF16)
    for h in range(n_heads):
        c0 = h * MLA_SLOT
        qcat_ref[:, c0:c0 + d] = (q[:, c0:c0 + d] * q_scale).astype(BF16)
        qcat_ref[:, c0 + d:c0 + 2 * d] = (_rope_apply(q[:, c0 + d:c0 + 2 * d], cos, sin, 16) * q_scale).astype(BF16)
        kcat_ref[:, c0:c0 + d] = kn[:, h * d:(h + 1) * d].astype(BF16)
        kcat_ref[:, c0 + d:c0 + 2 * d] = kr
    vt_ref[...] = _dot_nt(wvt_bf[...], ckvn).astype(BF16)


def mla_prep(z1, cosm, sinm, gq, wq_cat, gkv, wkn, wvt, n_tok, n_lat_total):
    M = z1.shape[0]
    tm = ROW_TILE
    RQ, NQ = wq_cat.shape
    RKV, NKN = wkn.shape
    tb = lambda i: (_table_block(i, tm, n_tok, n_lat_total), 0)
    const = lambda i: (0, 0)
    return pl.pallas_call(
        functools.partial(_mla_prep_kernel, q_scale=(HEAD_DIM + MLA_ROPE) ** -0.5 * LOG2E), grid=(M // tm,),
        in_specs=[pl.BlockSpec((tm, RQ), lambda i: (i, 6144 // RQ)),
                  pl.BlockSpec((tm, RKV // 2), lambda i: (i, 6912 // (RKV // 2))),
                  pl.BlockSpec((tm, RKV // 2), lambda i: (i, 6912 // (RKV // 2) + 1)),
                  pl.BlockSpec((tm, 128), lambda i: (i, 7424 // 128)),
                  pl.BlockSpec((tm, 128), tb), pl.BlockSpec((tm, 128), tb),
                  pl.BlockSpec((1, RQ), const), pl.BlockSpec((1, RKV), const),
                  pl.BlockSpec((RQ, NQ), const), pl.BlockSpec((RKV, NKN), const), pl.BlockSpec((NKN, RKV), const)],
        out_specs=[pl.BlockSpec((tm, NQ), lambda i: (i, 0)),
                   pl.BlockSpec((tm, NQ), lambda i: (i, 0)),
                   pl.BlockSpec((NKN, tm), lambda i: (0, i))],
        out_shape=[jax.ShapeDtypeStruct((M, NQ), BF16), jax.ShapeDtypeStruct((M, NQ), BF16),
                   jax.ShapeDtypeStruct((NKN, M), BF16)],
        scratch_shapes=[pltpu.VMEM((RQ, NQ), BF16), pltpu.VMEM((RKV, NKN), BF16), pltpu.VMEM((NKN, RKV), BF16)],
        compiler_params=_cparams(1, 56), name="mla_prep",
    )(z1, z1, z1, z1, cosm, sinm, gq.reshape(1, RQ), gkv.reshape(1, RKV), wq_cat, wkn, wvt)


def _softmax_parts(s_list):
    m = functools.reduce(jnp.maximum, [jnp.max(s, axis=-1, keepdims=True) for s in s_list])
    p_list = [jnp.exp(s - m) for s in s_list]
    l = functools.reduce(lambda a, b: a + b, [jnp.sum(p, axis=-1, keepdims=True) for p in p_list])
    return p_list, l


def _softmax_parts_t(st_list):
    m = functools.reduce(jnp.maximum, [jnp.max(s, axis=0, keepdims=True) for s in st_list])
    p_list = [jnp.exp2(s - m) for s in st_list]
    l = functools.reduce(lambda a, b: a + b, [jnp.sum(p, axis=0, keepdims=True) for p in p_list])
    return p_list, l


def _query_rows(n_samples, n_tok, n_ctx, tq, ctx_queries):
    if ctx_queries:
        nq = n_ctx // tq
        first = n_samples * n_tok // tq
        return nq, (lambda b, t: first + b * nq + t), n_samples * n_ctx, (lambda b, t: b * nq + t)
    nq = n_tok // tq
    qrow = lambda b, t: b * nq + t
    return nq, qrow, n_samples * n_tok, qrow


def _mla_attn_kernel(q_ref, *rest, with_latent):
    if with_latent:
        kl_ref, vtl_ref, kc_ref, vtc_ref, o_ref = rest
    else:
        kc_ref, vtc_ref, o_ref = rest
    for hh in range(q_ref.shape[1] // MLA_SLOT):
        ds = slice(hh * MLA_SLOT, (hh + 1) * MLA_SLOT)
        dv = slice(hh * HEAD_DIM, (hh + 1) * HEAD_DIM)
        q = q_ref[:, ds]
        k_list = [kc_ref[:, ds]] + ([kl_ref[:, ds]] if with_latent else [])
        vt_list = [vtc_ref[dv, :]] + ([vtl_ref[dv, :]] if with_latent else [])
        p_list, l = _softmax_parts_t([_dot_nt(k, q) for k in k_list])
        ot = functools.reduce(lambda a, b: a + b, [_dot(vt, p.astype(BF16)) for vt, p in zip(vt_list, p_list)])
        o_ref[:, dv] = jnp.transpose(ot * (1.0 / l)).astype(BF16)


def mla_attention(qcat, kcat, vt, n_samples, n_tok, n_ctx, ctx_queries):
    tq = min(512, n_ctx) if ctx_queries else 512
    HP = 1
    nq, qrow, out_rows, orow = _query_rows(n_samples, n_tok, n_ctx, tq, ctx_queries)
    cblk = n_samples * n_tok // n_ctx
    WS, WV = HP * MLA_SLOT, HP * HEAD_DIM
    in_specs = [pl.BlockSpec((tq, WS), lambda b, hp, t: (qrow(b, t), hp))]
    args = [qcat]
    if not ctx_queries:
        in_specs += [pl.BlockSpec((n_tok, WS), lambda b, hp, t: (b, hp)),
                     pl.BlockSpec((WV, n_tok), lambda b, hp, t: (hp, b))]
        args += [kcat, vt]
    in_specs += [pl.BlockSpec((n_ctx, WS), lambda b, hp, t: (cblk + b, hp)),
                 pl.BlockSpec((WV, n_ctx), lambda b, hp, t: (hp, cblk + b))]
    args += [kcat, vt]
    return pl.pallas_call(
        functools.partial(_mla_attn_kernel, with_latent=not ctx_queries),
        grid=(n_samples, vt.shape[0] // WV, nq),
        in_specs=in_specs,
        out_specs=pl.BlockSpec((tq, WV), lambda b, hp, t: (orow(b, t), hp)),
        out_shape=jax.ShapeDtypeStruct((out_rows, vt.shape[0]), BF16),
        compiler_params=_cparams(3, 56), name="mla_attn_ctx" if ctx_queries else "mla_attn",
    )(*args)


def _diff_attn_kernel(q_ref, *rest, with_latent, out_scale):
    if with_latent:
        kl_ref, vtl_ref, kc_ref, vtc_ref, lam_ref, g_ref, o_ref = rest
    else:
        kc_ref, vtc_ref, lam_ref, g_ref, o_ref = rest
    lp = lam_ref[...]
    lam = (jnp.exp(jnp.sum(lp[0:1] * lp[1:2], axis=-1, keepdims=True))
           - jnp.exp(jnp.sum(lp[2:3] * lp[3:4], axis=-1, keepdims=True)) + lam_ref[4:5, 0:1])
    d = HEAD_DIM
    k_refs = [kc_ref] + ([kl_ref] if with_latent else [])
    vt_refs = [vtc_ref] + ([vtl_ref] if with_latent else [])
    parts = []
    for half in range(2):
        sl = slice(half * d, (half + 1) * d)
        q = q_ref[:, sl]
        p_list, l = _softmax_parts_t([_dot_nt(k[:, sl], q) for k in k_refs])
        parts.append((p_list, 1.0 / l))
    (p1_list, inv1), (p2_list, inv2) = parts
    c2 = lam * inv2
    ot = functools.reduce(lambda a, b: a + b,
                          [_dot(vt[...], (p1 * inv1 - p2 * c2).astype(BF16))
                           for p1, p2, vt in zip(p1_list, p2_list, vt_refs)])
    o = jnp.transpose(ot)
    y = o * lax.rsqrt(jnp.mean(o * o, axis=-1, keepdims=True) + RMS_EPS) * g_ref[...]
    o_ref[...] = (y * out_scale).astype(BF16)


def diff_attention(dq, dk, vt, lam_rows, subln_g, lambda_init, n_samples, n_tok, n_ctx, ctx_queries):
    tq = min(512, n_ctx) if ctx_queries else 512
    W = 2 * HEAD_DIM
    nq, qrow, out_rows, orow = _query_rows(n_samples, n_tok, n_ctx, tq, ctx_queries)
    cblk = n_samples * n_tok // n_ctx
    in_specs = [pl.BlockSpec((tq, W), lambda b, h, t: (qrow(b, t), h))]
    args = [dq]
    if not ctx_queries:
        in_specs += [pl.BlockSpec((n_tok, W), lambda b, h, t: (b, h)),
                     pl.BlockSpec((W, n_tok), lambda b, h, t: (h, b))]
        args += [dk, vt]
    in_specs += [pl.BlockSpec((n_ctx, W), lambda b, h, t: (cblk + b, h)),
                 pl.BlockSpec((W, n_ctx), lambda b, h, t: (h, cblk + b)),
                 pl.BlockSpec((8, HEAD_DIM), lambda b, h, t: (0, 0)),
                 pl.BlockSpec((1, W), lambda b, h, t: (0, 0))]
    args += [dk, vt, lam_rows, subln_g.reshape(1, W)]
    n_heads = dk.shape[1] // W
    return pl.pallas_call(
        functools.partial(_diff_attn_kernel, with_latent=not ctx_queries, out_scale=1.0 - lambda_init),
        grid=(n_samples, n_heads, nq),
        in_specs=in_specs,
        out_specs=pl.BlockSpec((tq, W), lambda b, h, t: (orow(b, t), h)),
        out_shape=jax.ShapeDtypeStruct((out_rows, dk.shape[1]), BF16),
        compiler_params=_cparams(3, 56), name="diff_attn_ctx" if ctx_queries else "diff_attn",
    )(*args)


NA_DY = 2 * NA_KH - 1


def _na_bias_tables(rpb):
    W = GRID_W
    H = rpb.shape[0]
    qc = np.arange(W)[:, None]
    kc = np.arange(W)[None, :]
    cs = np.clip(qc - NA_KW // 2, 0, W - NA_KW)
    col_ok = (kc >= cs) & (kc < cs + NA_KW)
    dx = np.clip(kc - qc + NA_KW - 1, 0, 2 * NA_KW - 2)
    by_dx = jnp.take(rpb.astype(F32), jnp.asarray(dx.reshape(W * W)), axis=2).reshape(H, NA_DY, W, W)
    by_dx = jnp.where(jnp.asarray(col_ok)[None, None], by_dx, NEG_BIG)
    neg = jnp.full((H, 1, W, W), NEG_BIG, F32)
    padded = jnp.concatenate([neg, by_dx, neg, neg], axis=1)
    return jnp.concatenate([padded[:, :-1], padded[:, 1:]], axis=-1)


def _na_attn_kernel(q_ref, kl_ref, vl_ref, kc_ref, vc_ref, bias_ref, o_ref, *, scale, n_rows):
    rb = pl.program_id(2)
    r0 = jnp.clip(rb * 8 - 4, 0, n_rows - 16)
    start = pl.multiple_of(r0 * GRID_W, 256)
    kw = kl_ref[pl.ds(start, 16 * GRID_W), :]
    vw = vl_ref[pl.ds(start, 16 * GRID_W), :]
    q = q_ref[...]
    first, last_blk = rb == 0, rb == n_rows // 8 - 1
    dy0 = jnp.where(first, NA_KH - 1, jnp.where(last_blk, -1, NA_KH // 2 - 1))
    lane = lax.broadcasted_iota(jnp.int32, (GRID_W, 2 * GRID_W), 1)
    bias_rows = []
    for qa in range(8):
        win0 = jnp.where(first, max(qa - 4, 0), jnp.where(last_blk, min(qa + 4, 8), qa))
        tiles = []
        for kp in range(8):
            a1 = jnp.clip(2 * kp - qa + dy0 + 1, 0, NA_DY + 1)
            ok_even = jnp.logical_and(2 * kp >= win0, 2 * kp < win0 + NA_KH).astype(jnp.int32)
            ok_odd = jnp.logical_and(2 * kp + 1 >= win0, 2 * kp + 1 < win0 + NA_KH).astype(jnp.int32)
            ok = jnp.where(lane < GRID_W, ok_even, ok_odd)
            tiles.append(jnp.where(ok > 0, bias_ref[a1], NEG_BIG))
        bias_rows.append(jnp.concatenate(tiles, axis=1))
    s_lat = _dot_nt(q, kw) * scale + jnp.concatenate(bias_rows, axis=0)
    s_ctx = _dot_nt(q, kc_ref[...]) * scale
    (p_lat, p_ctx), l = _softmax_parts([s_lat, s_ctx])
    inv = 1.0 / l
    o = _dot((p_lat * inv).astype(BF16), vw) + _dot((p_ctx * inv).astype(BF16), vc_ref[...])
    o_ref[...] = o.astype(BF16)


def na_attention(z1, bias, n_samples, n_tok, n_ctx, n_heads):
    n_rows = n_tok // GRID_W
    assert n_rows >= 16 and n_rows % 8 == 0
    nrb = n_rows // 8
    tq = 8 * GRID_W
    d = HEAD_DIM
    cblk = n_samples * n_tok // n_ctx
    return pl.pallas_call(
        functools.partial(_na_attn_kernel, scale=d ** -0.5, n_rows=n_rows),
        grid=(n_samples, n_heads, nrb),
        in_specs=[pl.BlockSpec((tq, d), lambda b, h, rb: (b * nrb + rb, h)),
                  pl.BlockSpec((n_tok, d), lambda b, h, rb: (b, n_heads + h)),
                  pl.BlockSpec((n_tok, d), lambda b, h, rb: (b, 2 * n_heads + h)),
                  pl.BlockSpec((n_ctx, d), lambda b, h, rb: (cblk + b, n_heads + h)),
                  pl.BlockSpec((n_ctx, d), lambda b, h, rb: (cblk + b, 2 * n_heads + h)),
                  pl.BlockSpec((None, NA_DY + 2, GRID_W, 2 * GRID_W), lambda b, h, rb: (h, 0, 0, 0))],
        out_specs=pl.BlockSpec((tq, d), lambda b, h, rb: (b * nrb + rb, h)),
        out_shape=jax.ShapeDtypeStruct((n_samples * n_tok, n_heads * d), BF16),
        compiler_params=_cparams(3, 40), name="na_attn",
    )(z1, z1, z1, z1, z1, bias)


def _ctx_attn_kernel(q_ref, k_ref, v_ref, o_ref, *, scale):
    (p,), l = _softmax_parts([_dot_nt(q_ref[...], k_ref[...]) * scale])
    o_ref[...] = _dot((p * (1.0 / l)).astype(BF16), v_ref[...]).astype(BF16)


def na_ctx_attention(z1, n_samples, n_tok, n_ctx, n_heads):
    d = HEAD_DIM
    cblk = n_samples * n_tok // n_ctx
    return pl.pallas_call(
        functools.partial(_ctx_attn_kernel, scale=d ** -0.5),
        grid=(n_samples, n_heads),
        in_specs=[pl.BlockSpec((n_ctx, d), lambda b, h: (cblk + b, h)),
                  pl.BlockSpec((n_ctx, d), lambda b, h: (cblk + b, n_heads + h)),
                  pl.BlockSpec((n_ctx, d), lambda b, h: (cblk + b, 2 * n_heads + h))],
        out_specs=pl.BlockSpec((n_ctx, d), lambda b, h: (b, h)),
        out_shape=jax.ShapeDtypeStruct((n_samples * n_ctx, n_heads * d), BF16),
        compiler_params=_cparams(2, 16), name="na_attn_ctx",
    )(z1, z1, z1)


def _hgrn_chunk(q_raw, f_raw, v, lb, st_ref, reverse):
    C, HW = q_raw.shape
    dk = HEAD_DIM
    n_heads = HW // dk
    q = _silu(q_raw) * (dk ** -0.5)
    f = lb + (1.0 - lb) * jax.nn.sigmoid(f_raw)
    g = jnp.maximum(jnp.log(f), -200.0)
    k = 1.0 - f
    t_idx = lax.broadcasted_iota(jnp.int32, (C, 1), 0)
    p = (C - 1 - t_idx) if reverse else t_idx

    def prev(x, s):
        return pltpu.roll(x, (C - s) if reverse else s, 0)

    def nxt(x, s):
        return pltpu.roll(x, s if reverse else (C - s), 0)

    def incl_scan(b):
        x = g
        pb = jnp.bitwise_and(p, b - 1)
        s = 1
        while s < b:
            x = x + jnp.where(pb >= s, prev(x, s), 0.0)
            s *= 2
        return x

    def excl_rscan(b):
        if b == 1:
            return jnp.zeros_like(g)
        pb = jnp.bitwise_and(p, b - 1)
        x = jnp.where(pb <= b - 2, nxt(g, 1), 0.0)
        s = 1
        while s < b:
            x = x + jnp.where(pb + s <= b - 1, nxt(x, s), 0.0)
            s *= 2
        return x

    cum = incl_scan(C)
    last = cum[0:1] if reverse else cum[C - 1:C]
    q_in = (q * jnp.exp(cum)).astype(BF16)
    k_out = (k * jnp.exp(last - cum)).astype(BF16)
    total = jnp.exp(last)
    qb, kb, vb = q.astype(BF16), k.astype(BF16), v.astype(BF16)

    def boundary_rows(b):
        rows = []
        for j in range(C // (2 * b)):
            r = 2 * b * j + (b if reverse else b - 1)
            rows.append(jnp.broadcast_to(cum[r:r + 1], (2 * b, HW)))
        return jnp.concatenate(rows, axis=0)

    levels = []
    b = C // 2
    while b >= 1:
        upper = jnp.bitwise_and(p, b) != 0
        if b >= 8:
            x = cum - boundary_rows(b)
            e_up, e_lo = jnp.minimum(x, 0.0), jnp.minimum(-x, 0.0)
        else:
            e_up, e_lo = incl_scan(b), excl_rscan(b)
        ql = jnp.where(upper, q * jnp.exp(e_up), 0.0).astype(BF16)
        kl = jnp.where(upper, 0.0, k * jnp.exp(e_lo)).astype(BF16)
        levels.append((b, ql, kl))
        b //= 2

    s_idx = lax.broadcasted_iota(jnp.int32, (1, C), 1)
    ps = (C - 1 - s_idx) if reverse else s_idx
    outs = []
    for h in range(n_heads):
        sl = slice(h * dk, (h + 1) * dk)
        st_old = st_ref[h]
        att = jnp.where(p == ps, _dot_nt(qb[:, sl], kb[:, sl]), 0.0)
        for b, ql, kl in levels:
            same_pair = jnp.bitwise_and(p, -2 * b) == jnp.bitwise_and(ps, -2 * b)
            att = att + jnp.where(same_pair, _dot_nt(ql[:, sl], kl[:, sl]), 0.0)
        outs.append(_dot_nt(q_in[:, sl], st_old.astype(BF16)) + _dot(att.astype(BF16), vb[:, sl]))
        st_ref[h] = total[:, sl] * st_old + _dot_tn(vb[:, sl], k_out[:, sl])
    return jnp.concatenate(outs, axis=1)


def _hgrn_kernel(qf_ref, ff_ref, if_ref, qb_ref, fb_ref, ib_ref, lb_ref, of_ref, ob_ref, sf_ref, sb_ref):
    @pl.when(pl.program_id(1) == 0)
    def _():
        sf_ref[...] = jnp.zeros_like(sf_ref)
        sb_ref[...] = jnp.zeros_like(sb_ref)

    lb = lb_ref[...]
    of_ref[...] = _hgrn_chunk(qf_ref[...], ff_ref[...], if_ref[...], lb, sf_ref, reverse=False)
    ob_ref[...] = _hgrn_chunk(qb_ref[...], fb_ref[...], ib_ref[...], lb, sb_ref, reverse=True)


def hgrn_scan(z2, lb, n_samples, n_tok, n_ctx):
    M = z2.shape[0]
    HW = z2.shape[1] // 5
    C = HGRN_CHUNK
    ncc, ncl = n_ctx // C, n_tok // C
    lat_blocks = n_samples * ncl

    def fwd_row(b, c):
        return jnp.where(c < ncc, lat_blocks + b * ncc + c, b * ncl + (c - ncc))

    def bwd_row(b, c):
        return jnp.where(c < ncc, lat_blocks + b * ncc + (ncc - 1 - c), b * ncl + (ncl - 1 - (c - ncc)))

    spec = lambda rowf, col: pl.BlockSpec((C, HW), lambda b, c: (rowf(b, c), col))
    return pl.pallas_call(
        _hgrn_kernel, grid=(n_samples, ncc + ncl),
        in_specs=[spec(fwd_row, 0), spec(fwd_row, 1), spec(fwd_row, 3),
                  spec(bwd_row, 0), spec(bwd_row, 2), spec(bwd_row, 3),
                  pl.BlockSpec((1, HW), lambda b, c: (0, 0))],
        out_specs=[spec(fwd_row, 0), spec(bwd_row, 0)],
        out_shape=[jax.ShapeDtypeStruct((M, HW), F32)] * 2,
        scratch_shapes=[pltpu.VMEM((HW // HEAD_DIM, HEAD_DIM, HEAD_DIM), F32)] * 2,
        compiler_params=_cparams(2, 48), name="hgrn_scan",
    )(z2, z2, z2, z2, z2, z2, lb.reshape(1, HW))


def _hgrn_readout_kernel(of_ref, ob_ref, gz_ref, g_ref, o_ref):
    for h in range(of_ref.shape[1] // HEAD_DIM):
        sl = slice(h * HEAD_DIM, (h + 1) * HEAD_DIM)
        x = of_ref[:, sl] + ob_ref[:, sl]
        y = x * lax.rsqrt(jnp.mean(x * x, axis=-1, keepdims=True) + RMS_EPS) * g_ref[...]
        o_ref[:, sl] = (y * _silu(gz_ref[:, sl])).astype(BF16)


def hgrn_readout(o_f, o_b, z2, norm_g, n_rows):
    HW = o_f.shape[1]
    tm = ROW_TILE
    return pl.pallas_call(
        _hgrn_readout_kernel, grid=(n_rows // tm,),
        in_specs=[pl.BlockSpec((tm, HW), lambda i: (i, 0)), pl.BlockSpec((tm, HW), lambda i: (i, 0)),
                  pl.BlockSpec((tm, HW), lambda i: (i, 4)), pl.BlockSpec((1, HEAD_DIM), lambda i: (0, 0))],
        out_specs=pl.BlockSpec((tm, HW), lambda i: (i, 0)),
        out_shape=jax.ShapeDtypeStruct((n_rows, HW), BF16),
        compiler_params=_cparams(1, 32), name="hgrn_readout",
    )(o_f, o_b, z2, norm_g.reshape(1, HEAD_DIM))


def _merge_kernel(h_ref, oa_ref, od_ref, om_ref, or_ref, wg_ref, bg_ref, wb_ref, o_ref, wg_bf, wb_bf):
    @pl.when(pl.program_id(1) == 0)
    def _():
        wg_bf[...] = wg_ref[...].astype(BF16)
        wb_bf[...] = wb_ref[...].astype(BF16)

    h = h_ref[...]
    acc = None
    for j, o_j in enumerate((oa_ref, od_ref, om_ref, or_ref)):
        gate = jax.nn.sigmoid(_dot(h, wg_bf[j]) + bg_ref[j])
        term = gate * _dot(o_j[...], wb_bf[j])
        acc = term if acc is None else acc + term
    o_ref[...] = acc.astype(BF16)


def merge_branches(h, outs, w_gate, b_gate, w_branch, layer, n_rows):
    D = h.shape[1]
    L, nb, BW = w_branch.shape[:3]
    tm, tn = ROW_TILE, 256
    single = dict(pipeline_mode=pl.Buffered(1))
    return pl.pallas_call(
        _merge_kernel, grid=(D // tn, n_rows // tm),
        in_specs=[pl.BlockSpec((tm, D), lambda j, i: (i, 0))]
                 + [pl.BlockSpec((tm, BW), lambda j, i: (i, 0))] * nb
                 + [pl.BlockSpec((None, nb, D, tn), lambda j, i: (layer, 0, 0, j), **single),
                    pl.BlockSpec((None, nb, 1, tn), lambda j, i: (layer, 0, 0, j)),
                    pl.BlockSpec((None, nb, BW, tn), lambda j, i: (layer, 0, 0, j), **single)],
        out_specs=pl.BlockSpec((tm, tn), lambda j, i: (i, j)),
        out_shape=jax.ShapeDtypeStruct((n_rows, D), BF16),
        scratch_shapes=[pltpu.VMEM((nb, D, tn), BF16), pltpu.VMEM((nb, BW, tn), BF16)],
        compiler_params=_cparams(2, 58), name="merge",
    )(h, *outs, w_gate, b_gate.reshape(L, nb, 1, D), w_branch)


def _ffn_in_kernel(x_ref, w1_ref, w3_ref, o_ref, w1_bf, w3_bf):
    @pl.when(pl.program_id(1) == 0)
    def _():
        w1_bf[...] = w1_ref[...].astype(BF16)
        w3_bf[...] = w3_ref[...].astype(BF16)

    x = x_ref[...]
    o_ref[...] = (_silu(_dot(x, w1_bf[...])) * _dot(x, w3_bf[...])).astype(BF16)


def ffn_in(h, w1, w3, layer, n_rows):
    _, D, Fd = w1.shape
    tm, tn = 2 * ROW_TILE, 256
    w_spec = pl.BlockSpec((None, D, tn), lambda j, i: (layer, 0, j))
    return pl.pallas_call(
        _ffn_in_kernel,
        grid=(Fd // tn, pl.cdiv(n_rows, tm)),
        in_specs=[pl.BlockSpec((tm, D), lambda j, i: (i, 0)), w_spec, w_spec],
        out_specs=pl.BlockSpec((tm, tn), lambda j, i: (i, j)),
        out_shape=jax.ShapeDtypeStruct((n_rows, Fd), BF16),
        scratch_shapes=[pltpu.VMEM((D, tn), BF16)] * 2,
        compiler_params=_cparams(2, 50), name="ffn_in",
    )(h, w1, w3)


def _group_spec(tm, tn, n_lat, n_samples):
    return pl.BlockSpec((None, 1, tn), lambda j, i: (_grp_of_tile(i, tm, n_lat, n_samples), 0, j))


def ffn_out(u, w2, layer, xres, gate, n_rows, n_lat, n_samples):
    D = w2.shape[2]
    G = gate.shape[0]
    tm, tn = 256, 512
    return _mm_call(u, w2, layer, n_rows=n_rows, tn=tn, tm=tm, out_dtype=F32,
                    epilogue=_residual_epilogue, extras=(xres, gate.reshape(G, 1, D)),
                    extra_specs=(pl.BlockSpec((tm, tn), lambda j, i: (i, j)), _group_spec(tm, tn, n_lat, n_samples)),
                    name="ffn_out")


MOE_TILE = 256


def moe_route_plan(comb, n_experts):
    M = comb.shape[0]
    T = MOE_TILE
    w = comb[:, :n_experts]
    sel = w > 0
    n_tiles = 2 * M // T + n_experts
    R = n_tiles * T
    cnt = jnp.sum(sel, axis=0, dtype=jnp.int32)
    rank = jnp.cumsum(sel, axis=0, dtype=jnp.int32) - 1
    gsz = (cnt + T - 1) // T * T
    gend = jnp.cumsum(gsz)
    off = gend - gsz
    dest_all = off[None, :] + rank
    tile_start = jnp.arange(n_tiles, dtype=jnp.int32) * T
    tile_expert = jnp.minimum(jnp.searchsorted(gend, tile_start, side="right"), n_experts - 1).astype(jnp.int32)
    n_used = (gend[-1] // T).reshape(1).astype(jnp.int32)
    e_lo = jnp.argmax(sel, axis=1)
    e_hi = n_experts - 1 - jnp.argmax(sel[:, ::-1], axis=1)
    take = lambda a, e: jnp.take_along_axis(a, e[:, None], axis=1)[:, 0]
    two = e_hi != e_lo
    dest = jnp.stack([take(dest_all, e_lo), take(dest_all, e_hi)], axis=1).astype(jnp.int32)
    tok = lax.broadcasted_iota(jnp.int32, dest.shape, 0)
    src_tok = jnp.zeros((R,), jnp.int32).at[dest.reshape(-1)].set(tok.reshape(-1))
    pw = jnp.zeros((M, 128), F32).at[:, 0].set(take(w, e_lo)).at[:, 1].set(jnp.where(two, take(w, e_hi), 0.0))
    return src_tok, tile_expert, n_used, dest, pw


def _row_copy(src_hbm, row, dst_buf, slot, r, sem):
    return pltpu.make_async_copy(src_hbm.at[pl.ds(row, 1)], dst_buf.at[slot, pl.ds(r, 1)], sem.at[slot])


def _gather_tile(idx_of_row, src_hbm, buf, sem, n_rows_tile):
    i = pl.program_id(0)
    n = pl.num_programs(0)

    def start_tile(t, slot):
        def body(r2, carry):
            for pri in range(2):
                r = 2 * r2 + pri
                _row_copy(src_hbm, idx_of_row(t * n_rows_tile + r), buf, slot, r, sem).start(priority=pri)
            return carry
        lax.fori_loop(0, n_rows_tile // 2, body, 0, unroll=4)

    @pl.when(i == 0)
    def _():
        start_tile(0, 0)

    @pl.when(i + 1 < n)
    def _():
        start_tile(i + 1, (i + 1) % 2)

    slot = i % 2

    def wait_body(r, carry):
        _row_copy(src_hbm, 0, buf, slot, r, sem).wait()
        return carry
    lax.fori_loop(0, n_rows_tile, wait_body, 0, unroll=8)
    return slot


def _moe_gather_kernel(src_ref, h_hbm, o_ref, buf, sem):
    slot = _gather_tile(lambda r: src_ref[r], h_hbm, buf, sem, o_ref.shape[0])
    o_ref[...] = buf[slot].astype(BF16)


def moe_gather(hf, src_tok):
    R = src_tok.shape[0]
    D = hf.shape[1]
    T = MOE_TILE
    return pl.pallas_call(
        _moe_gather_kernel,
        grid_spec=pltpu.PrefetchScalarGridSpec(
            num_scalar_prefetch=1, grid=(R // T,),
            in_specs=[pl.BlockSpec(memory_space=pl.ANY)],
            out_specs=pl.BlockSpec((T, D), lambda i, src: (i, 0)),
            scratch_shapes=[pltpu.VMEM((2, T, D), F32), pltpu.SemaphoreType.DMA((2,))]),
        out_shape=jax.ShapeDtypeStruct((R, D), BF16),
        compiler_params=_cparams(1, 24), name="moe_gather",
    )(src_tok, hf)


def _moe_in_kernel(te_ref, nu_ref, x_ref, w1_ref, w3_ref, o_ref, w1_bf, w3_bf):
    i = pl.program_id(1)
    fresh = jnp.logical_or(i == 0, te_ref[i] != te_ref[jnp.maximum(i - 1, 0)])

    @pl.when(fresh)
    def _():
        w1_bf[...] = w1_ref[...].astype(BF16)
        w3_bf[...] = w3_ref[...].astype(BF16)

    @pl.when(i < nu_ref[0])
    def _():
        x = x_ref[...]
        o_ref[...] = (_silu(_dot(x, w1_bf[...])) * _dot(x, w3_bf[...])).astype(BF16)

    @pl.when(i >= nu_ref[0])
    def _():
        o_ref[...] = jnp.zeros_like(o_ref)


def moe_ffn_in(xs, w1, w3, e0, tile_expert, n_used):
    R, D = xs.shape
    Fd = w1.shape[2]
    T, tn = MOE_TILE, 512
    w_spec = pl.BlockSpec((None, D, tn), lambda j, i, te, nu: (e0 + te[i], 0, j))
    return pl.pallas_call(
        _moe_in_kernel,
        grid_spec=pltpu.PrefetchScalarGridSpec(
            num_scalar_prefetch=2, grid=(Fd // tn, R // T),
            in_specs=[pl.BlockSpec((T, D), lambda j, i, te, nu: (i, 0)), w_spec, w_spec],
            out_specs=pl.BlockSpec((T, tn), lambda j, i, te, nu: (i, j)),
            scratch_shapes=[pltpu.VMEM((D, tn), BF16)] * 2),
        out_shape=jax.ShapeDtypeStruct((R, Fd), BF16),
        compiler_params=_cparams(2, 54), name="moe_ffn_in",
    )(tile_expert, n_used, xs, w1, w3)


def _moe_out_kernel(te_ref, nu_ref, u_ref, w2_ref, o_ref, w2_bf):
    i = pl.program_id(1)
    fresh = jnp.logical_or(i == 0, te_ref[i] != te_ref[jnp.maximum(i - 1, 0)])

    @pl.when(fresh)
    def _():
        w2_bf[...] = w2_ref[...].astype(BF16)

    @pl.when(i < nu_ref[0])
    def _():
        o_ref[...] = _dot(u_ref[...], w2_bf[...])

    @pl.when(i >= nu_ref[0])
    def _():
        o_ref[...] = jnp.zeros_like(o_ref)


def moe_ffn_out(u, w2, e0, tile_expert, n_used):
    R, Fd = u.shape
    D = w2.shape[2]
    T, tn = MOE_TILE, 1024
    return pl.pallas_call(
        _moe_out_kernel,
        grid_spec=pltpu.PrefetchScalarGridSpec(
            num_scalar_prefetch=2, grid=(D // tn, R // T),
            in_specs=[pl.BlockSpec((T, Fd), lambda j, i, te, nu: (i, 0)),
                      pl.BlockSpec((None, Fd, tn), lambda j, i, te, nu: (e0 + te[i], 0, j))],
            out_specs=pl.BlockSpec((T, tn), lambda j, i, te, nu: (i, j)),
            scratch_shapes=[pltpu.VMEM((Fd, tn), BF16)]),
        out_shape=jax.ShapeDtypeStruct((R, D), F32),
        compiler_params=_cparams(2, 40), name="moe_ffn_out",
    )(tile_expert, n_used, u, w2)


def _moe_combine_kernel(dest_ref, y_hbm, pw_ref, x_ref, g_ref, o_ref, buf, sem):
    T = o_ref.shape[0]
    slot = _gather_tile(lambda r: dest_ref[r], y_hbm, buf, sem, 2 * T)
    pw = pw_ref[...]
    mix = pw[:, 0:1] * buf[slot, 0:T, :] + pw[:, 1:2] * buf[slot, T:2 * T, :]
    o_ref[...] = x_ref[...] + g_ref[...] * mix


def moe_combine(y, dest, pw, xres, gate, n_lat, n_samples):
    M, D = xres.shape
    G = gate.shape[0]
    T = 128
    dest = dest.reshape(M // T, T, 2).transpose(0, 2, 1)
    return pl.pallas_call(
        _moe_combine_kernel,
        grid_spec=pltpu.PrefetchScalarGridSpec(
            num_scalar_prefetch=1, grid=(M // T,),
            in_specs=[pl.BlockSpec(memory_space=pl.ANY),
                      pl.BlockSpec((T, 128), lambda i, d: (i, 0)),
                      pl.BlockSpec((T, D), lambda i, d: (i, 0)),
                      pl.BlockSpec((None, 1, D), lambda i, d: (_grp_of_tile(i, T, n_lat, n_samples), 0, 0))],
            out_specs=pl.BlockSpec((T, D), lambda i, d: (i, 0)),
            scratch_shapes=[pltpu.VMEM((2, 2 * T, D), F32), pltpu.SemaphoreType.DMA((2,))]),
        out_shape=jax.ShapeDtypeStruct((M, D), F32),
        compiler_params=_cparams(1, 32), name="moe_combine",
    )(dest.reshape(-1), y, pw, xres, gate.reshape(G, 1, D))


def kernel(x, c, ctx, c_ctx, norm1_g, norm2_g, w_ada, b_ada, w_in, na_rpb, diff_lambda, diff_subln_g,
           mla_q_norm_g, mla_w_q_up, mla_kv_norm_g, mla_w_kv_up, hgrn_lower_bounds, hgrn_norm_g,
           w_branch, w_gate, b_gate, w_out, ffn_w1, ffn_w3, ffn_w2, moe_router, moe_w1, moe_w3, moe_w2,
           final_norm_g):
    B, N, D = x.shape
    NC = ctx.shape[1]
    L = w_ada.shape[0]
    n_lat, n_all = B * N, B * N + B * NC
    BW = w_branch.shape[2]
    n_heads = BW // HEAD_DIM
    HG0 = w_in.shape[2] - 5 * BW
    assert N % ROW_TILE == 0 and (B * NC) % ROW_TILE == 0 and NC % HGRN_CHUNK == 0

    xs = jnp.concatenate([x.reshape(n_lat, D), ctx.reshape(B * NC, D)], axis=0)

    cond8 = jnp.zeros((8, D), F32).at[:B].set(c).at[B].set(c_ctx)
    mods = adaln_all(cond8, w_ada, b_ada)[:, :B + 1].reshape(L, B + 1, 6, D)

    lb_all = jnp.cumsum(jax.nn.softmax(hgrn_lower_bounds.astype(F32), axis=0), axis=0)
    lb_all = lb_all - lb_all[0:1]

    w_in_t = jnp.swapaxes(w_in, 1, 2)

    cosd, sind = _rope_tables(N, B * NC, 2 * 64, ROW_TILE)
    cosm, sinm = _rope_tables(N, B * NC, 64, ROW_TILE)

    RQ, RKV = mla_w_q_up.shape[1], mla_w_kv_up.shape[1]
    wq4 = mla_w_q_up.reshape(L, RQ, n_heads, HEAD_DIM + MLA_ROPE)
    wq_cat = jnp.pad(wq4, ((0, 0), (0, 0), (0, 0), (0, MLA_SLOT - HEAD_DIM - MLA_ROPE))).reshape(L, RQ, -1)
    wkv4 = mla_w_kv_up.reshape(L, RKV, n_heads, 2 * HEAD_DIM)
    wkn = wkv4[..., :HEAD_DIM].reshape(L, RKV, -1)
    wvt = jnp.swapaxes(wkv4[..., HEAD_DIM:].reshape(L, RKV, -1), 1, 2)

    for l in range(L):
        need_ctx = l < L - 1
        n_rows = n_all if need_ctx else n_lat
        lambda_init = 0.8 - 0.6 * math.exp(-0.3 * l)
        sh1, sc1, g1, sh2, sc2, g2 = (mods[l, :, k] for k in range(6))

        h = norm_mod(xs, norm1_g[l], sh1, sc1, N, B)
        z1 = wt_stream_matmul(h, w_in_t, l, 0, 15 * 512, BF16, "w_in_attn")
        z2 = wt_stream_matmul(h, w_in_t, l, HG0, 5 * BW, F32, "w_in_hgrn")

        bias = _na_bias_tables(na_rpb[l])
        o_a = na_attention(z1, bias, B, N, NC, n_heads)
        dq, dk, dvt = diff_rope(z1, cosd, sind, N, n_lat)
        lam_rows = jnp.zeros((8, HEAD_DIM), F32).at[:4].set(diff_lambda[l]).at[4].set(lambda_init)
        o_d = diff_attention(dq, dk, dvt, lam_rows, diff_subln_g[l], lambda_init, B, N, NC, False)
        qcat, kcat, mvt = mla_prep(z1, cosm, sinm, mla_q_norm_g[l], wq_cat[l], mla_kv_norm_g[l], wkn[l], wvt[l],
                                   N, n_lat)
        o_m = mla_attention(qcat, kcat, mvt, B, N, NC, False)
        o_f, o_b = hgrn_scan(z2, lb_all[l], B, N, NC)
        o_r = hgrn_readout(o_f, o_b, z2, hgrn_norm_g[l], n_rows)
        if need_ctx:
            o_a = jnp.concatenate([o_a, na_ctx_attention(z1, B, N, NC, n_heads)], axis=0)
            o_d = jnp.concatenate([o_d, diff_attention(dq, dk, dvt, lam_rows, diff_subln_g[l], lambda_init,
                                                       B, N, NC, True)], axis=0)
            o_m = jnp.concatenate([o_m, mla_attention(qcat, kcat, mvt, B, N, NC, True)], axis=0)

        s = merge_branches(h, (o_a, o_d, o_m, o_r), w_gate, b_gate, w_branch, l, n_rows)
        G = B + 1
        tmo = 2 * ROW_TILE
        xs = _mm_call(s, w_out, l, n_rows=n_rows, tn=512, tm=tmo, out_dtype=F32,
                      epilogue=_residual_epilogue, extras=(xs, g1.reshape(G, 1, D)),
                      extra_specs=(pl.BlockSpec((tmo, 512), lambda j, i: (i, j)), _group_spec(tmo, 512, N, B)),
                      name="w_out")

        j = l // 2
        if l % 2 == 0:
            h2 = norm_mod(xs, norm2_g[l], sh2, sc2, N, B)
            u = ffn_in(h2, ffn_w1, ffn_w3, j, n_rows)
            xs = ffn_out(u, ffn_w2, j, xs, g2, n_rows, N, B)
        else:
            hf, comb = norm_mod(xs, norm2_g[l], sh2, sc2, N, B, w_router=moe_router[j])
            NM, E, _, FE = moe_w1.shape
            src_tok, tile_expert, n_used, dest, pw = moe_route_plan(comb, E)
            xg = moe_gather(hf, src_tok)
            u = moe_ffn_in(xg, moe_w1.reshape(NM * E, D, FE), moe_w3.reshape(NM * E, D, FE), j * E, tile_expert, n_used)
            y = moe_ffn_out(u, moe_w2.reshape(NM * E, FE, D), j * E, tile_expert, n_used)
            xs = moe_combine(y, dest, pw, xs, g2, N, B)

    return final_norm(xs[:n_lat], final_norm_g).reshape(B, N, D)
```

```python
import functools
import math

import numpy as np
import jax
import jax.numpy as jnp
from jax import lax
from jax.experimental import pallas as pl
from jax.experimental.pallas import tpu as pltpu

F32 = jnp.float32
BF16 = jnp.bfloat16

GRID_W = 64
RMS_EPS = 1e-6
ROPE_BASE = 10000.0
NA_KH, NA_KW = 8, 16
HEAD_DIM = 128
MLA_ROPE = 64
HGRN_CHUNK = 64
MOE_EXPERTS = 8
NEG_BIG = -1e30

VMEM_LIMIT_V7X = 60000 * 1024
ROW_TILE = 512


def _even_row_tile(n_rows, cap=1152):
    return next(t for t in range(cap, 15, -16) if n_rows % t == 0)


def _cparams(n_axes, vmem_mb):
    return pltpu.CompilerParams(dimension_semantics=("arbitrary",) * n_axes,
                                vmem_limit_bytes=min(int(vmem_mb * 2**20), VMEM_LIMIT_V7X))


def _silu(x):
    return x * jax.nn.sigmoid(x)


def _dot(a, b):
    return jnp.dot(a, b, preferred_element_type=F32)


def _dot_nt(a, b):
    return lax.dot_general(a, b, (((1,), (1,)), ((), ())), preferred_element_type=F32)


def _dot_tn(a, b):
    return lax.dot_general(a, b, (((0,), (0,)), ((), ())), preferred_element_type=F32)


def _adaln_kernel(c_ref, w_ref, b_ref, o_ref):
    @pl.when(pl.program_id(1) == 0)
    def _():
        o_ref[...] = jnp.broadcast_to(b_ref[...], o_ref.shape)

    x = _silu(c_ref[...]).astype(BF16)
    o_ref[...] += _dot(x, w_ref[...].astype(BF16))


def adaln_all(cond8, w_ada, b_ada):
    L, D, N6 = w_ada.shape
    tk = 128
    cond_k = cond8.reshape(8, D // tk, tk).transpose(1, 0, 2)
    return pl.pallas_call(
        _adaln_kernel,
        grid=(L, D // tk),
        in_specs=[pl.BlockSpec((None, 8, tk), lambda l, k: (k, 0, 0)),
                  pl.BlockSpec((None, tk, N6), lambda l, k: (l, k, 0)),
                  pl.BlockSpec((None, 1, N6), lambda l, k: (l, 0, 0))],
        out_specs=pl.BlockSpec((None, 8, N6), lambda l, k: (l, 0, 0)),
        out_shape=jax.ShapeDtypeStruct((L, 8, N6), F32),
        compiler_params=_cparams(2, 48),
        name="adaln",
    )(cond_k, w_ada, b_ada.reshape(L, 1, N6))


def _norm_mod_kernel(x_ref, g_ref, sh_ref, sc_ref, *rest, with_router):
    x = x_ref[...]
    y = x * lax.rsqrt(jnp.mean(x * x, axis=-1, keepdims=True) + RMS_EPS) * g_ref[...]
    h = y * (1.0 + sc_ref[...]) + sh_ref[...]
    if not with_router:
        (h_ref,) = rest
        h_ref[...] = h.astype(BF16)
        return
    wr_ref, h_ref, comb_ref = rest
    h_ref[...] = h
    logits = jnp.dot(h, wr_ref[...], precision=lax.Precision.HIGHEST, preferred_element_type=F32)
    lane = lax.broadcasted_iota(jnp.int32, logits.shape, 1).astype(F32)
    logits = jnp.where(lane < MOE_EXPERTS, logits, -jnp.inf)
    m1 = jnp.max(logits, axis=-1, keepdims=True)
    i1 = jnp.min(jnp.where(logits == m1, lane, 128.0), axis=-1, keepdims=True)
    rest_l = jnp.where(lane == i1, -jnp.inf, logits)
    m2 = jnp.max(rest_l, axis=-1, keepdims=True)
    i2 = jnp.min(jnp.where(rest_l == m2, lane, 128.0), axis=-1, keepdims=True)
    e2 = jnp.exp(m2 - m1)
    p1 = 1.0 / (1.0 + e2)
    p2 = e2 / (1.0 + e2)
    comb_ref[...] = jnp.where(lane == i1, p1, 0.0) + jnp.where(lane == i2, p2, 0.0)


def _grp_of_tile(i, tm, lat_rows_per_sample, n_samples):
    return jnp.minimum((i * tm) // lat_rows_per_sample, n_samples)


def norm_mod(x, g, shift, scale, n_lat, n_samples, w_router=None):
    M, D = x.shape
    tm = 256
    G = shift.shape[0]
    grp = lambda i: (_grp_of_tile(i, tm, n_lat, n_samples), 0, 0)
    in_specs = [pl.BlockSpec((tm, D), lambda i: (i, 0)),
                pl.BlockSpec((1, D), lambda i: (0, 0)),
                pl.BlockSpec((None, 1, D), grp),
                pl.BlockSpec((None, 1, D), grp)]
    args = [x, g.reshape(1, D), shift.reshape(G, 1, D), scale.reshape(G, 1, D)]
    out_specs = [pl.BlockSpec((tm, D), lambda i: (i, 0))]
    out_shape = [jax.ShapeDtypeStruct((M, D), BF16 if w_router is None else F32)]
    if w_router is not None:
        wr = jnp.zeros((D, 128), F32).at[:, :w_router.shape[1]].set(w_router)
        in_specs.append(pl.BlockSpec((D, 128), lambda i: (0, 0)))
        args.append(wr)
        out_specs.append(pl.BlockSpec((tm, 128), lambda i: (i, 0)))
        out_shape.append(jax.ShapeDtypeStruct((M, 128), F32))
    outs = pl.pallas_call(
        functools.partial(_norm_mod_kernel, with_router=w_router is not None),
        grid=(M // tm,),
        in_specs=in_specs, out_specs=out_specs, out_shape=out_shape,
        compiler_params=_cparams(1, 40),
        name="norm_mod_router" if w_router is not None else "norm_mod",
    )(*args)
    return outs if w_router is not None else outs[0]


def _final_norm_kernel(x_ref, g_ref, o_ref):
    x = x_ref[...]
    o_ref[...] = x * lax.rsqrt(jnp.mean(x * x, axis=-1, keepdims=True) + RMS_EPS) * g_ref[...]


def final_norm(x, g):
    M, D = x.shape
    tm = 256
    return pl.pallas_call(
        _final_norm_kernel, grid=(M // tm,),
        in_specs=[pl.BlockSpec((tm, D), lambda i: (i, 0)), pl.BlockSpec((1, D), lambda i: (0, 0))],
        out_specs=pl.BlockSpec((tm, D), lambda i: (i, 0)),
        out_shape=jax.ShapeDtypeStruct((M, D), F32),
        compiler_params=_cparams(1, 40), name="final_norm",
    )(x, g.reshape(1, D))


def _cast_rows(src_ref, dst_ref, rows=512):
    n = src_ref.shape[0]
    for r0 in range(0, n, rows):
        r1 = min(r0 + rows, n)
        dst_ref[r0:r1, :] = src_ref[r0:r1, :].astype(BF16)


def _stream_mm_kernel(x_ref, w_hbm, *rest, layer, n_extra, epilogue):
    extras = rest[:n_extra]
    o_ref, stage, wbf, sem = rest[n_extra:]
    j, i, nj = pl.program_id(0), pl.program_id(1), pl.num_programs(0)
    tn = stage.shape[1]

    def block_copy(jj):
        return pltpu.make_async_copy(w_hbm.at[layer, :, pl.ds(pl.multiple_of(jj * tn, tn), tn)], stage, sem)

    @pl.when(i == 0)
    def _():
        @pl.when(j == 0)
        def _():
            block_copy(0).start()

        block_copy(j).wait()
        _cast_rows(stage, wbf)

        @pl.when(j + 1 < nj)
        def _():
            block_copy(j + 1).start()

    acc = _dot(x_ref[...], wbf[...])
    if epilogue is not None:
        acc = epilogue(acc, *[e[...] for e in extras])
    o_ref[...] = acc.astype(o_ref.dtype)


def _mm_call(x, w, layer, *, n_rows, tn, tm, out_dtype, epilogue=None, extras=(), extra_specs=(), name="mm"):
    K, N = w.shape[1:]
    in_specs = [pl.BlockSpec((tm, K), lambda j, i: (i, 0)), pl.BlockSpec(memory_space=pl.ANY)]
    in_specs += list(extra_specs)
    out_bytes = jnp.dtype(out_dtype).itemsize
    vmem = (2 * tm * K * 2 + K * tn * 4 + K * tn * 2 + 2 * tm * tn * out_bytes + 3 * tm * tn * 4
            + len(extras) * 2 * tm * tn * 4) / 2**20 + 6
    return pl.pallas_call(
        functools.partial(_stream_mm_kernel, layer=layer, n_extra=len(extras), epilogue=epilogue),
        grid=(N // tn, pl.cdiv(n_rows, tm)),
        in_specs=in_specs,
        out_specs=pl.BlockSpec((tm, tn), lambda j, i: (i, j)),
        out_shape=jax.ShapeDtypeStruct((n_rows, N), out_dtype),
        scratch_shapes=[pltpu.VMEM((K, tn), F32), pltpu.VMEM((K, tn), BF16), pltpu.SemaphoreType.DMA(())],
        compiler_params=_cparams(2, vmem),
        name=name,
    )(x, w, *extras)


def _residual_epilogue(acc, xres, gate):
    return xres + gate * acc


def _wt_stream_mm_kernel(x_ref, wt_hbm, o_ref, stage, wbf, sem, *, layer, row0):
    j, i, nj = pl.program_id(0), pl.program_id(1), pl.num_programs(0)
    tn, K = stage.shape

    def block_copy(jj):
        return pltpu.make_async_copy(wt_hbm.at[layer, pl.ds(row0 + jj * tn, tn), :], stage, sem)

    @pl.when(i == 0)
    def _():
        @pl.when(j == 0)
        def _():
            block_copy(0).start()

        block_copy(j).wait()
        for c in range(0, K, tn):
            wbf[c:c + tn, :] = jnp.transpose(stage[:, c:c + tn]).astype(BF16)

        @pl.when(j + 1 < nj)
        def _():
            block_copy(j + 1).start()

    o_ref[...] = _dot(x_ref[...], wbf[...]).astype(o_ref.dtype)


def wt_stream_matmul(x, wt, layer, row0, n_cols, out_dtype, name):
    M, K = x.shape
    tm, tn = _even_row_tile(M), 512
    assert n_cols % tn == 0 and K % tn == 0 and row0 % 8 == 0
    return pl.pallas_call(
        functools.partial(_wt_stream_mm_kernel, layer=layer, row0=row0),
        grid=(n_cols // tn, pl.cdiv(M, tm)),
        in_specs=[pl.BlockSpec((tm, K), lambda j, i: (i, 0)), pl.BlockSpec(memory_space=pl.ANY)],
        out_specs=pl.BlockSpec((tm, tn), lambda j, i: (i, j)),
        out_shape=jax.ShapeDtypeStruct((M, n_cols), out_dtype),
        scratch_shapes=[pltpu.VMEM((tn, K), F32), pltpu.VMEM((K, tn), BF16), pltpu.SemaphoreType.DMA(())],
        compiler_params=_cparams(2, 48), name=name,
    )(x, wt)


def _rope_tables(n_tokens, n_ident, group, tile_rows):
    half = group // 2
    q = half // 2
    pos = np.arange(n_tokens)
    row, col = pos // GRID_W, pos % GRID_W
    freqs = ROPE_BASE ** (-(np.arange(q, dtype=np.float64) / q))
    lane = np.arange(128)
    in_group = lane % group
    axis_pos = np.where((in_group < half)[None, :], row[:, None], col[:, None]).astype(np.float64)
    ang = axis_pos * freqs[(in_group % half) % q][None, :]
    sign = np.where((in_group % half) < q, -1.0, 1.0)[None, :]
    cos, sin = np.cos(ang), np.sin(ang) * sign
    n_pad = -(-n_ident // tile_rows) * tile_rows
    cos = np.concatenate([cos, np.ones((n_pad, 128))], axis=0)
    sin = np.concatenate([sin, np.zeros((n_pad, 128))], axis=0)
    return jnp.asarray(cos, F32), jnp.asarray(sin, F32)


def _rope_apply(x, cos, sin, q):
    lane = lax.broadcasted_iota(jnp.int32, x.shape, 1)
    partner = jnp.where((lane % (2 * q)) < q, pltpu.roll(x, 128 - q, 1), pltpu.roll(x, q, 1))
    return x * cos + partner * sin


LOG2E = math.log2(math.e)


def _diff_rope_kernel(q_ref, k_ref, v_ref, cos_ref, sin_ref, qo_ref, ko_ref, vt_ref, *, q_scale):
    vt_ref[...] = jnp.transpose(v_ref[...].astype(F32)).astype(BF16)
    cos, sin = cos_ref[...], sin_ref[...]
    for src, dst, mult in ((q_ref, qo_ref, q_scale), (k_ref, ko_ref, None)):
        for c in range(src.shape[1] // 128):
            sl = slice(c * 128, (c + 1) * 128)
            y = _rope_apply(src[:, sl].astype(F32), cos, sin, 32)
            dst[:, sl] = (y if mult is None else y * mult).astype(BF16)


def _table_block(i, tm, n_tok, n_lat_total):
    return jnp.where(i * tm < n_lat_total, (i * tm % n_tok) // tm, n_tok // tm)


def diff_rope(z1, cosd, sind, n_tok, n_lat_total):
    M = z1.shape[0]
    tm = ROW_TILE
    W = 1024
    tb = lambda i: (_table_block(i, tm, n_tok, n_lat_total), 0)
    return pl.pallas_call(
        functools.partial(_diff_rope_kernel, q_scale=HEAD_DIM ** -0.5 * LOG2E), grid=(M // tm,),
        in_specs=[pl.BlockSpec((tm, W), lambda i: (i, 3)), pl.BlockSpec((tm, W), lambda i: (i, 4)),
                  pl.BlockSpec((tm, W), lambda i: (i, 5)),
                  pl.BlockSpec((tm, 128), tb), pl.BlockSpec((tm, 128), tb)],
        out_specs=[pl.BlockSpec((tm, W), lambda i: (i, 0))] * 2 + [pl.BlockSpec((W, tm), lambda i: (0, i))],
        out_shape=[jax.ShapeDtypeStruct((M, W), BF16)] * 2 + [jax.ShapeDtypeStruct((W, M), BF16)],
        compiler_params=_cparams(1, 40), name="diff_rope",
    )(z1, z1, z1, cosd, sind)


MLA_SLOT = 2 * HEAD_DIM


def _mla_prep_kernel(cq_ref, ckva_ref, ckvb_ref, kr_ref, cos_ref, sin_ref, gq_ref, gkv_ref, wq_ref, wkn_ref,
                     wvt_ref, qcat_ref, kcat_ref, vt_ref, wq_bf, wkn_bf, wvt_bf, *, q_scale):
    @pl.when(pl.program_id(0) == 0)
    def _():
        wq_bf[...] = wq_ref[...].astype(BF16)
        wkn_bf[...] = wkn_ref[...].astype(BF16)
        wvt_bf[...] = wvt_ref[...].astype(BF16)

    cos, sin = cos_ref[...], sin_ref[...]
    d = HEAD_DIM
    n_heads = qcat_ref.shape[1] // MLA_SLOT
    cq = cq_ref[...].astype(F32)
    cqn = (cq * lax.rsqrt(jnp.mean(cq * cq, axis=-1, keepdims=True) + RMS_EPS) * gq_ref[...]).astype(BF16)
    q = _dot(cqn, wq_bf[...])
    a = ckva_ref[...].astype(F32)
    b = ckvb_ref[...].astype(F32)
    ha = a.shape[1]
    ms = (jnp.sum(a * a, axis=-1, keepdims=True) + jnp.sum(b * b, axis=-1, keepdims=True)) / (2 * ha)
    r = lax.rsqrt(ms + RMS_EPS)
    g = gkv_ref[...]
    ckvn = jnp.concatenate([(a * r * g[:, :ha]).astype(BF16), (b * r * g[:, ha:]).astype(BF16)], axis=1)
    kn = _dot(ckvn, wkn_bf[...])
    lane = lax.broadcasted_iota(jnp.int32, cos.shape, 1)
    kr = jnp.where(lane < MLA_ROPE, _rope_apply(kr_ref[...].astype(F32), cos, sin, 16), 0.0).astype(BF16)
    for h in range(n_heads):
        c0 = h * MLA_SLOT
        qcat_ref[:, c0:c0 + d] = (q[:, c0:c0 + d] * q_scale).astype(BF16)
        qcat_ref[:, c0 + d:c0 + 2 * d] = (_rope_apply(q[:, c0 + d:c0 + 2 * d], cos, sin, 16) * q_scale).astype(BF16)
        kcat_ref[:, c0:c0 + d] = kn[:, h * d:(h + 1) * d].astype(BF16)
        kcat_ref[:, c0 + d:c0 + 2 * d] = kr
    vt_ref[...] = _dot_nt(wvt_bf[...], ckvn).astype(BF16)


def mla_prep(z1, cosm, sinm, gq, wq_cat, gkv, wkn, wvt, n_tok, n_lat_total):
    M = z1.shape[0]
    tm = ROW_TILE
    RQ, NQ = wq_cat.shape
    RKV, NKN = wkn.shape
    tb = lambda i: (_table_block(i, tm, n_tok, n_lat_total), 0)
    const = lambda i: (0, 0)
    return pl.pallas_call(
        functools.partial(_mla_prep_kernel, q_scale=(HEAD_DIM + MLA_ROPE) ** -0.5 * LOG2E), grid=(M // tm,),
        in_specs=[pl.BlockSpec((tm, RQ), lambda i: (i, 6144 // RQ)),
                  pl.BlockSpec((tm, RKV // 2), lambda i: (i, 6912 // (RKV // 2))),
                  pl.BlockSpec((tm, RKV // 2), lambda i: (i, 6912 // (RKV // 2) + 1)),
                  pl.BlockSpec((tm, 128), lambda i: (i, 7424 // 128)),
                  pl.BlockSpec((tm, 128), tb), pl.BlockSpec((tm, 128), tb),
                  pl.BlockSpec((1, RQ), const), pl.BlockSpec((1, RKV), const),
                  pl.BlockSpec((RQ, NQ), const), pl.BlockSpec((RKV, NKN), const), pl.BlockSpec((NKN, RKV), const)],
        out_specs=[pl.BlockSpec((tm, NQ), lambda i: (i, 0)),
                   pl.BlockSpec((tm, NQ), lambda i: (i, 0)),
                   pl.BlockSpec((NKN, tm), lambda i: (0, i))],
        out_shape=[jax.ShapeDtypeStruct((M, NQ), BF16), jax.ShapeDtypeStruct((M, NQ), BF16),
                   jax.ShapeDtypeStruct((NKN, M), BF16)],
        scratch_shapes=[pltpu.VMEM((RQ, NQ), BF16), pltpu.VMEM((RKV, NKN), BF16), pltpu.VMEM((NKN, RKV), BF16)],
        compiler_params=_cparams(1, 56), name="mla_prep",
    )(z1, z1, z1, z1, cosm, sinm, gq.reshape(1, RQ), gkv.reshape(1, RKV), wq_cat, wkn, wvt)


def _softmax_parts(s_list):
    m = functools.reduce(jnp.maximum, [jnp.max(s, axis=-1, keepdims=True) for s in s_list])
    p_list = [jnp.exp(s - m) for s in s_list]
    l = functools.reduce(lambda a, b: a + b, [jnp.sum(p, axis=-1, keepdims=True) for p in p_list])
    return p_list, l


def _softmax_parts_t(st_list):
    m = functools.reduce(jnp.maximum, [jnp.max(s, axis=0, keepdims=True) for s in st_list])
    p_list = [jnp.exp2(s - m) for s in st_list]
    l = functools.reduce(lambda a, b: a + b, [jnp.sum(p, axis=0, keepdims=True) for p in p_list])
    return p_list, l


def _query_rows(n_samples, n_tok, n_ctx, tq, ctx_queries):
    if ctx_queries:
        nq = n_ctx // tq
        first = n_samples * n_tok // tq
        return nq, (lambda b, t: first + b * nq + t), n_samples * n_ctx, (lambda b, t: b * nq + t)
    nq = n_tok // tq
    qrow = lambda b, t: b * nq + t
    return nq, qrow, n_samples * n_tok, qrow


def _mla_attn_kernel(q_ref, *rest, with_latent):
    if with_latent:
        kl_ref, vtl_ref, kc_ref, vtc_ref, o_ref = rest
    else:
        kc_ref, vtc_ref, o_ref = rest
    for hh in range(q_ref.shape[1] // MLA_SLOT):
        ds = slice(hh * MLA_SLOT, (hh + 1) * MLA_SLOT)
        dv = slice(hh * HEAD_DIM, (hh + 1) * HEAD_DIM)
        q = q_ref[:, ds]
        k_list = [kc_ref[:, ds]] + ([kl_ref[:, ds]] if with_latent else [])
        vt_list = [vtc_ref[dv, :]] + ([vtl_ref[dv, :]] if with_latent else [])
        p_list, l = _softmax_parts_t([_dot_nt(k, q) for k in k_list])
        ot = functools.reduce(lambda a, b: a + b, [_dot(vt, p.astype(BF16)) for vt, p in zip(vt_list, p_list)])
        o_ref[:, dv] = jnp.transpose(ot * (1.0 / l)).astype(BF16)


def mla_attention(qcat, kcat, vt, n_samples, n_tok, n_ctx, ctx_queries):
    tq = min(512, n_ctx) if ctx_queries else 512
    HP = 1
    nq, qrow, out_rows, orow = _query_rows(n_samples, n_tok, n_ctx, tq, ctx_queries)
    cblk = n_samples * n_tok // n_ctx
    WS, WV = HP * MLA_SLOT, HP * HEAD_DIM
    in_specs = [pl.BlockSpec((tq, WS), lambda b, hp, t: (qrow(b, t), hp))]
    args = [qcat]
    if not ctx_queries:
        in_specs += [pl.BlockSpec((n_tok, WS), lambda b, hp, t: (b, hp)),
                     pl.BlockSpec((WV, n_tok), lambda b, hp, t: (hp, b))]
        args += [kcat, vt]
    in_specs += [pl.BlockSpec((n_ctx, WS), lambda b, hp, t: (cblk + b, hp)),
                 pl.BlockSpec((WV, n_ctx), lambda b, hp, t: (hp, cblk + b))]
    args += [kcat, vt]
    return pl.pallas_call(
        functools.partial(_mla_attn_kernel, with_latent=not ctx_queries),
        grid=(n_samples, vt.shape[0] // WV, nq),
        in_specs=in_specs,
        out_specs=pl.BlockSpec((tq, WV), lambda b, hp, t: (orow(b, t), hp)),
        out_shape=jax.ShapeDtypeStruct((out_rows, vt.shape[0]), BF16),
        compiler_params=_cparams(3, 56), name="mla_attn_ctx" if ctx_queries else "mla_attn",
    )(*args)


def _diff_attn_kernel(q_ref, *rest, with_latent, out_scale):
    if with_latent:
        kl_ref, vtl_ref, kc_ref, vtc_ref, lam_ref, g_ref, o_ref = rest
    else:
        kc_ref, vtc_ref, lam_ref, g_ref, o_ref = rest
    lp = lam_ref[...]
    lam = (jnp.exp(jnp.sum(lp[0:1] * lp[1:2], axis=-1, keepdims=True))
           - jnp.exp(jnp.sum(lp[2:3] * lp[3:4], axis=-1, keepdims=True)) + lam_ref[4:5, 0:1])
    d = HEAD_DIM
    k_refs = [kc_ref] + ([kl_ref] if with_latent else [])
    vt_refs = [vtc_ref] + ([vtl_ref] if with_latent else [])
    parts = []
    for half in range(2):
        sl = slice(half * d, (half + 1) * d)
        q = q_ref[:, sl]
        p_list, l = _softmax_parts_t([_dot_nt(k[:, sl], q) for k in k_refs])
        parts.append((p_list, 1.0 / l))
    (p1_list, inv1), (p2_list, inv2) = parts
    c2 = lam * inv2
    ot = functools.reduce(lambda a, b: a + b,
                          [_dot(vt[...], (p1 * inv1 - p2 * c2).astype(BF16))
                           for p1, p2, vt in zip(p1_list, p2_list, vt_refs)])
    o = jnp.transpose(ot)
    y = o * lax.rsqrt(jnp.mean(o * o, axis=-1, keepdims=True) + RMS_EPS) * g_ref[...]
    o_ref[...] = (y * out_scale).astype(BF16)


def diff_attention(dq, dk, vt, lam_rows, subln_g, lambda_init, n_samples, n_tok, n_ctx, ctx_queries):
    tq = min(512, n_ctx) if ctx_queries else 512
    W = 2 * HEAD_DIM
    nq, qrow, out_rows, orow = _query_rows(n_samples, n_tok, n_ctx, tq, ctx_queries)
    cblk = n_samples * n_tok // n_ctx
    in_specs = [pl.BlockSpec((tq, W), lambda b, h, t: (qrow(b, t), h))]
    args = [dq]
    if not ctx_queries:
        in_specs += [pl.BlockSpec((n_tok, W), lambda b, h, t: (b, h)),
                     pl.BlockSpec((W, n_tok), lambda b, h, t: (h, b))]
        args += [dk, vt]
    in_specs += [pl.BlockSpec((n_ctx, W), lambda b, h, t: (cblk + b, h)),
                 pl.BlockSpec((W, n_ctx), lambda b, h, t: (h, cblk + b)),
                 pl.BlockSpec((8, HEAD_DIM), lambda b, h, t: (0, 0)),
                 pl.BlockSpec((1, W), lambda b, h, t: (0, 0))]
    args += [dk, vt, lam_rows, subln_g.reshape(1, W)]
    n_heads = dk.shape[1] // W
    return pl.pallas_call(
        functools.partial(_diff_attn_kernel, with_latent=not ctx_queries, out_scale=1.0 - lambda_init),
        grid=(n_samples, n_heads, nq),
        in_specs=in_specs,
        out_specs=pl.BlockSpec((tq, W), lambda b, h, t: (orow(b, t), h)),
        out_shape=jax.ShapeDtypeStruct((out_rows, dk.shape[1]), BF16),
        compiler_params=_cparams(3, 56), name="diff_attn_ctx" if ctx_queries else "diff_attn",
    )(*args)


NA_DY = 2 * NA_KH - 1


def _na_bias_tables(rpb):
    W = GRID_W
    H = rpb.shape[0]
    qc = np.arange(W)[:, None]
    kc = np.arange(W)[None, :]
    cs = np.clip(qc - NA_KW // 2, 0, W - NA_KW)
    col_ok = (kc >= cs) & (kc < cs + NA_KW)
    dx = np.clip(kc - qc + NA_KW - 1, 0, 2 * NA_KW - 2)
    by_dx = jnp.take(rpb.astype(F32), jnp.asarray(dx.reshape(W * W)), axis=2).reshape(H, NA_DY, W, W)
    by_dx = jnp.where(jnp.asarray(col_ok)[None, None], by_dx, NEG_BIG)
    neg = jnp.full((H, 1, W, W), NEG_BIG, F32)
    padded = jnp.concatenate([neg, by_dx, neg, neg], axis=1)
    return jnp.concatenate([padded[:, :-1], padded[:, 1:]], axis=-1)


def _na_attn_kernel(q_ref, kl_ref, vl_ref, kc_ref, vc_ref, bias_ref, o_ref, *, scale, n_rows):
    rb = pl.program_id(2)
    r0 = jnp.clip(rb * 8 - 4, 0, n_rows - 16)
    start = pl.multiple_of(r0 * GRID_W, 256)
    kw = kl_ref[pl.ds(start, 16 * GRID_W), :]
    vw = vl_ref[pl.ds(start, 16 * GRID_W), :]
    q = q_ref[...]
    first, last_blk = rb == 0, rb == n_rows // 8 - 1
    dy0 = jnp.where(first, NA_KH - 1, jnp.where(last_blk, -1, NA_KH // 2 - 1))
    lane = lax.broadcasted_iota(jnp.int32, (GRID_W, 2 * GRID_W), 1)
    bias_rows = []
    for qa in range(8):
        win0 = jnp.where(first, max(qa - 4, 0), jnp.where(last_blk, min(qa + 4, 8), qa))
        tiles = []
        for kp in range(8):
            a1 = jnp.clip(2 * kp - qa + dy0 + 1, 0, NA_DY + 1)
            ok_even = jnp.logical_and(2 * kp >= win0, 2 * kp < win0 + NA_KH).astype(jnp.int32)
            ok_odd = jnp.logical_and(2 * kp + 1 >= win0, 2 * kp + 1 < win0 + NA_KH).astype(jnp.int32)
            ok = jnp.where(lane < GRID_W, ok_even, ok_odd)
            tiles.append(jnp.where(ok > 0, bias_ref[a1], NEG_BIG))
        bias_rows.append(jnp.concatenate(tiles, axis=1))
    s_lat = _dot_nt(q, kw) * scale + jnp.concatenate(bias_rows, axis=0)
    s_ctx = _dot_nt(q, kc_ref[...]) * scale
    (p_lat, p_ctx), l = _softmax_parts([s_lat, s_ctx])
    inv = 1.0 / l
    o = _dot((p_lat * inv).astype(BF16), vw) + _dot((p_ctx * inv).astype(BF16), vc_ref[...])
    o_ref[...] = o.astype(BF16)


def na_attention(z1, bias, n_samples, n_tok, n_ctx, n_heads):
    n_rows = n_tok // GRID_W
    assert n_rows >= 16 and n_rows % 8 == 0
    nrb = n_rows // 8
    tq = 8 * GRID_W
    d = HEAD_DIM
    cblk = n_samples * n_tok // n_ctx
    return pl.pallas_call(
        functools.partial(_na_attn_kernel, scale=d ** -0.5, n_rows=n_rows),
        grid=(n_samples, n_heads, nrb),
        in_specs=[pl.BlockSpec((tq, d), lambda b, h, rb: (b * nrb + rb, h)),
                  pl.BlockSpec((n_tok, d), lambda b, h, rb: (b, n_heads + h)),
                  pl.BlockSpec((n_tok, d), lambda b, h, rb: (b, 2 * n_heads + h)),
                  pl.BlockSpec((n_ctx, d), lambda b, h, rb: (cblk + b, n_heads + h)),
                  pl.BlockSpec((n_ctx, d), lambda b, h, rb: (cblk + b, 2 * n_heads + h)),
                  pl.BlockSpec((None, NA_DY + 2, GRID_W, 2 * GRID_W), lambda b, h, rb: (h, 0, 0, 0))],
        out_specs=pl.BlockSpec((tq, d), lambda b, h, rb: (b * nrb + rb, h)),
        out_shape=jax.ShapeDtypeStruct((n_samples * n_tok, n_heads * d), BF16),
        compiler_params=_cparams(3, 40), name="na_attn",
    )(z1, z1, z1, z1, z1, bias)


def _ctx_attn_kernel(q_ref, k_ref, v_ref, o_ref, *, scale):
    (p,), l = _softmax_parts([_dot_nt(q_ref[...], k_ref[...]) * scale])
    o_ref[...] = _dot((p * (1.0 / l)).astype(BF16), v_ref[...]).astype(BF16)


def na_ctx_attention(z1, n_samples, n_tok, n_ctx, n_heads):
    d = HEAD_DIM
    cblk = n_samples * n_tok // n_ctx
    return pl.pallas_call(
        functools.partial(_ctx_attn_kernel, scale=d ** -0.5),
        grid=(n_samples, n_heads),
        in_specs=[pl.BlockSpec((n_ctx, d), lambda b, h: (cblk + b, h)),
                  pl.BlockSpec((n_ctx, d), lambda b, h: (cblk + b, n_heads + h)),
                  pl.BlockSpec((n_ctx, d), lambda b, h: (cblk + b, 2 * n_heads + h))],
        out_specs=pl.BlockSpec((n_ctx, d), lambda b, h: (b, h)),
        out_shape=jax.ShapeDtypeStruct((n_samples * n_ctx, n_heads * d), BF16),
        compiler_params=_cparams(2, 16), name="na_attn_ctx",
    )(z1, z1, z1)


def _hgrn_chunk(q_raw, f_raw, v, lb, st_ref, reverse):
    C, HW = q_raw.shape
    dk = HEAD_DIM
    n_heads = HW // dk
    q = _silu(q_raw) * (dk ** -0.5)
    f = lb + (1.0 - lb) * jax.nn.sigmoid(f_raw)
    g = jnp.maximum(jnp.log(f), -200.0)
    k = 1.0 - f
    t_idx = lax.broadcasted_iota(jnp.int32, (C, 1), 0)
    p = (C - 1 - t_idx) if reverse else t_idx

    def prev(x, s):
        return pltpu.roll(x, (C - s) if reverse else s, 0)

    def nxt(x, s):
        return pltpu.roll(x, s if reverse else (C - s), 0)

    def incl_scan(b):
        x = g
        pb = jnp.bitwise_and(p, b - 1)
        s = 1
        while s < b:
            x = x + jnp.where(pb >= s, prev(x, s), 0.0)
            s *= 2
        return x

    def excl_rscan(b):
        if b == 1:
            return jnp.zeros_like(g)
        pb = jnp.bitwise_and(p, b - 1)
        x = jnp.where(pb <= b - 2, nxt(g, 1), 0.0)
        s = 1
        while s < b:
            x = x + jnp.where(pb + s <= b - 1, nxt(x, s), 0.0)
            s *= 2
        return x

    cum = incl_scan(C)
    last = cum[0:1] if reverse else cum[C - 1:C]
    q_in = (q * jnp.exp(cum)).astype(BF16)
    k_out = (k * jnp.exp(last - cum)).astype(BF16)
    total = jnp.exp(last)
    qb, kb, vb = q.astype(BF16), k.astype(BF16), v.astype(BF16)

    def boundary_rows(b):
        rows = []
        for j in range(C // (2 * b)):
            r = 2 * b * j + (b if reverse else b - 1)
            rows.append(jnp.broadcast_to(cum[r:r + 1], (2 * b, HW)))
        return jnp.concatenate(rows, axis=0)

    levels = []
    b = C // 2
    while b >= 1:
        upper = jnp.bitwise_and(p, b) != 0
        if b >= 8:
            x = cum - boundary_rows(b)
            e_up, e_lo = jnp.minimum(x, 0.0), jnp.minimum(-x, 0.0)
        else:
            e_up, e_lo = incl_scan(b), excl_rscan(b)
        ql = jnp.where(upper, q * jnp.exp(e_up), 0.0).astype(BF16)
        kl = jnp.where(upper, 0.0, k * jnp.exp(e_lo)).astype(BF16)
        levels.append((b, ql, kl))
        b //= 2

    s_idx = lax.broadcasted_iota(jnp.int32, (1, C), 1)
    ps = (C - 1 - s_idx) if reverse else s_idx
    outs = []
    for h in range(n_heads):
        sl = slice(h * dk, (h + 1) * dk)
        st_old = st_ref[h]
        att = jnp.where(p == ps, _dot_nt(qb[:, sl], kb[:, sl]), 0.0)
        for b, ql, kl in levels:
            same_pair = jnp.bitwise_and(p, -2 * b) == jnp.bitwise_and(ps, -2 * b)
            att = att + jnp.where(same_pair, _dot_nt(ql[:, sl], kl[:, sl]), 0.0)
        outs.append(_dot_nt(q_in[:, sl], st_old.astype(BF16)) + _dot(att.astype(BF16), vb[:, sl]))
        st_ref[h] = total[:, sl] * st_old + _dot_tn(vb[:, sl], k_out[:, sl])
    return jnp.concatenate(outs, axis=1)


def _hgrn_kernel(qf_ref, ff_ref, if_ref, qb_ref, fb_ref, ib_ref, lb_ref, of_ref, ob_ref, sf_ref, sb_ref):
    @pl.when(pl.program_id(1) == 0)
    def _():
        sf_ref[...] = jnp.zeros_like(sf_ref)
        sb_ref[...] = jnp.zeros_like(sb_ref)

    lb = lb_ref[...]
    of_ref[...] = _hgrn_chunk(qf_ref[...], ff_ref[...], if_ref[...], lb, sf_ref, reverse=False)
    ob_ref[...] = _hgrn_chunk(qb_ref[...], fb_ref[...], ib_ref[...], lb, sb_ref, reverse=True)


def hgrn_scan(z2, lb, n_samples, n_tok, n_ctx):
    M = z2.shape[0]
    HW = z2.shape[1] // 5
    C = HGRN_CHUNK
    ncc, ncl = n_ctx // C, n_tok // C
    lat_blocks = n_samples * ncl

    def fwd_row(b, c):
        return jnp.where(c < ncc, lat_blocks + b * ncc + c, b * ncl + (c - ncc))

    def bwd_row(b, c):
        return jnp.where(c < ncc, lat_blocks + b * ncc + (ncc - 1 - c), b * ncl + (ncl - 1 - (c - ncc)))

    spec = lambda rowf, col: pl.BlockSpec((C, HW), lambda b, c: (rowf(b, c), col))
    return pl.pallas_call(
        _hgrn_kernel, grid=(n_samples, ncc + ncl),
        in_specs=[spec(fwd_row, 0), spec(fwd_row, 1), spec(fwd_row, 3),
                  spec(bwd_row, 0), spec(bwd_row, 2), spec(bwd_row, 3),
                  pl.BlockSpec((1, HW), lambda b, c: (0, 0))],
        out_specs=[spec(fwd_row, 0), spec(bwd_row, 0)],
        out_shape=[jax.ShapeDtypeStruct((M, HW), F32)] * 2,
        scratch_shapes=[pltpu.VMEM((HW // HEAD_DIM, HEAD_DIM, HEAD_DIM), F32)] * 2,
        compiler_params=_cparams(2, 48), name="hgrn_scan",
    )(z2, z2, z2, z2, z2, z2, lb.reshape(1, HW))


def _hgrn_readout_kernel(of_ref, ob_ref, gz_ref, g_ref, o_ref):
    for h in range(of_ref.shape[1] // HEAD_DIM):
        sl = slice(h * HEAD_DIM, (h + 1) * HEAD_DIM)
        x = of_ref[:, sl] + ob_ref[:, sl]
        y = x * lax.rsqrt(jnp.mean(x * x, axis=-1, keepdims=True) + RMS_EPS) * g_ref[...]
        o_ref[:, sl] = (y * _silu(gz_ref[:, sl])).astype(BF16)


def hgrn_readout(o_f, o_b, z2, norm_g, n_rows):
    HW = o_f.shape[1]
    tm = ROW_TILE
    return pl.pallas_call(
        _hgrn_readout_kernel, grid=(n_rows // tm,),
        in_specs=[pl.BlockSpec((tm, HW), lambda i: (i, 0)), pl.BlockSpec((tm, HW), lambda i: (i, 0)),
                  pl.BlockSpec((tm, HW), lambda i: (i, 4)), pl.BlockSpec((1, HEAD_DIM), lambda i: (0, 0))],
        out_specs=pl.BlockSpec((tm, HW), lambda i: (i, 0)),
        out_shape=jax.ShapeDtypeStruct((n_rows, HW), BF16),
        compiler_params=_cparams(1, 32), name="hgrn_readout",
    )(o_f, o_b, z2, norm_g.reshape(1, HEAD_DIM))


def _merge_kernel(h_ref, oa_ref, od_ref, om_ref, or_ref, wg_ref, bg_ref, wb_ref, o_ref, wg_bf, wb_bf):
    @pl.when(pl.program_id(1) == 0)
    def _():
        wg_bf[...] = wg_ref[...].astype(BF16)
        wb_bf[...] = wb_ref[...].astype(BF16)

    h = h_ref[...]
    acc = None
    for j, o_j in enumerate((oa_ref, od_ref, om_ref, or_ref)):
        gate = jax.nn.sigmoid(_dot(h, wg_bf[j]) + bg_ref[j])
        term = gate * _dot(o_j[...], wb_bf[j])
        acc = term if acc is None else acc + term
    o_ref[...] = acc.astype(BF16)


def merge_branches(h, outs, w_gate, b_gate, w_branch, layer, n_rows):
    D = h.shape[1]
    L, nb, BW = w_branch.shape[:3]
    tm, tn = ROW_TILE, 256
    single = dict(pipeline_mode=pl.Buffered(1))
    return pl.pallas_call(
        _merge_kernel, grid=(D // tn, n_rows // tm),
        in_specs=[pl.BlockSpec((tm, D), lambda j, i: (i, 0))]
                 + [pl.BlockSpec((tm, BW), lambda j, i: (i, 0))] * nb
                 + [pl.BlockSpec((None, nb, D, tn), lambda j, i: (layer, 0, 0, j), **single),
                    pl.BlockSpec((None, nb, 1, tn), lambda j, i: (layer, 0, 0, j)),
                    pl.BlockSpec((None, nb, BW, tn), lambda j, i: (layer, 0, 0, j), **single)],
        out_specs=pl.BlockSpec((tm, tn), lambda j, i: (i, j)),
        out_shape=jax.ShapeDtypeStruct((n_rows, D), BF16),
        scratch_shapes=[pltpu.VMEM((nb, D, tn), BF16), pltpu.VMEM((nb, BW, tn), BF16)],
        compiler_params=_cparams(2, 58), name="merge",
    )(h, *outs, w_gate, b_gate.reshape(L, nb, 1, D), w_branch)


def _ffn_in_kernel(x_ref, w1_ref, w3_ref, o_ref, w1_bf, w3_bf):
    @pl.when(pl.program_id(1) == 0)
    def _():
        w1_bf[...] = w1_ref[...].astype(BF16)
        w3_bf[...] = w3_ref[...].astype(BF16)

    x = x_ref[...]
    o_ref[...] = (_silu(_dot(x, w1_bf[...])) * _dot(x, w3_bf[...])).astype(BF16)


def ffn_in(h, w1, w3, layer, n_rows):
    _, D, Fd = w1.shape
    tm, tn = _even_row_tile(n_rows), 256
    w_spec = pl.BlockSpec((None, D, tn), lambda j, i: (layer, 0, j))
    return pl.pallas_call(
        _ffn_in_kernel,
        grid=(Fd // tn, pl.cdiv(n_rows, tm)),
        in_specs=[pl.BlockSpec((tm, D), lambda j, i: (i, 0)), w_spec, w_spec],
        out_specs=pl.BlockSpec((tm, tn), lambda j, i: (i, j)),
        out_shape=jax.ShapeDtypeStruct((n_rows, Fd), BF16),
        scratch_shapes=[pltpu.VMEM((D, tn), BF16)] * 2,
        compiler_params=_cparams(2, 50), name="ffn_in",
    )(h, w1, w3)


def _group_spec(tm, tn, n_lat, n_samples):
    return pl.BlockSpec((None, 1, tn), lambda j, i: (_grp_of_tile(i, tm, n_lat, n_samples), 0, j))


def ffn_out(u, w2, layer, xres, gate, n_rows, n_lat, n_samples):
    D = w2.shape[2]
    G = gate.shape[0]
    tm, tn = 256, 512
    return _mm_call(u, w2, layer, n_rows=n_rows, tn=tn, tm=tm, out_dtype=F32,
                    epilogue=_residual_epilogue, extras=(xres, gate.reshape(G, 1, D)),
                    extra_specs=(pl.BlockSpec((tm, tn), lambda j, i: (i, j)), _group_spec(tm, tn, n_lat, n_samples)),
                    name="ffn_out")


MOE_TILE = 256


def moe_route_plan(comb, n_experts):
    M = comb.shape[0]
    T = MOE_TILE
    w = comb[:, :n_experts]
    sel = w > 0
    n_tiles = 2 * M // T + n_experts
    R = n_tiles * T
    cnt = jnp.sum(sel, axis=0, dtype=jnp.int32)
    rank = jnp.cumsum(sel, axis=0, dtype=jnp.int32) - 1
    gsz = (cnt + T - 1) // T * T
    gend = jnp.cumsum(gsz)
    off = gend - gsz
    dest_all = off[None, :] + rank
    tile_start = jnp.arange(n_tiles, dtype=jnp.int32) * T
    tile_expert = jnp.minimum(jnp.searchsorted(gend, tile_start, side="right"), n_experts - 1).astype(jnp.int32)
    n_used = (gend[-1] // T).reshape(1).astype(jnp.int32)
    k_next = jnp.searchsorted(tile_expert, tile_expert, side="right")
    next_expert = jnp.where(k_next < n_tiles, tile_expert[jnp.minimum(k_next, n_tiles - 1)], -1).astype(jnp.int32)
    e_lo = jnp.argmax(sel, axis=1)
    e_hi = n_experts - 1 - jnp.argmax(sel[:, ::-1], axis=1)
    take = lambda a, e: jnp.take_along_axis(a, e[:, None], axis=1)[:, 0]
    two = e_hi != e_lo
    dest = jnp.stack([take(dest_all, e_lo), take(dest_all, e_hi)], axis=1).astype(jnp.int32)
    tok = lax.broadcasted_iota(jnp.int32, dest.shape, 0)
    src_tok = jnp.zeros((R,), jnp.int32).at[dest.reshape(-1)].set(tok.reshape(-1))
    pw = jnp.zeros((M, 128), F32).at[:, 0].set(take(w, e_lo)).at[:, 1].set(jnp.where(two, take(w, e_hi), 0.0))
    return src_tok, (tile_expert, next_expert, n_used), dest, pw


def _row_copy(src_hbm, row, dst_buf, slot, r, sem):
    return pltpu.make_async_copy(src_hbm.at[pl.ds(row, 1)], dst_buf.at[slot, pl.ds(r, 1)], sem.at[slot])


def _gather_tile(idx_of_row, src_hbm, buf, sem, n_rows_tile):
    i = pl.program_id(0)
    n = pl.num_programs(0)

    def start_tile(t, slot):
        def body(r2, carry):
            for pri in range(2):
                r = 2 * r2 + pri
                _row_copy(src_hbm, idx_of_row(t * n_rows_tile + r), buf, slot, r, sem).start(priority=pri)
            return carry
        lax.fori_loop(0, n_rows_tile // 2, body, 0, unroll=4)

    @pl.when(i == 0)
    def _():
        start_tile(0, 0)

    @pl.when(i + 1 < n)
    def _():
        start_tile(i + 1, (i + 1) % 2)

    slot = i % 2

    def wait_body(r, carry):
        _row_copy(src_hbm, 0, buf, slot, r, sem).wait()
        return carry
    lax.fori_loop(0, n_rows_tile, wait_body, 0, unroll=8)
    return slot


def _moe_gather_kernel(src_ref, h_hbm, o_ref, buf, sem):
    slot = _gather_tile(lambda r: src_ref[r], h_hbm, buf, sem, o_ref.shape[0])
    o_ref[...] = buf[slot].astype(BF16)


def moe_gather(hf, src_tok):
    R = src_tok.shape[0]
    D = hf.shape[1]
    T = MOE_TILE
    return pl.pallas_call(
        _moe_gather_kernel,
        grid_spec=pltpu.PrefetchScalarGridSpec(
            num_scalar_prefetch=1, grid=(R // T,),
            in_specs=[pl.BlockSpec(memory_space=pl.ANY)],
            out_specs=pl.BlockSpec((T, D), lambda i, src: (i, 0)),
            scratch_shapes=[pltpu.VMEM((2, T, D), F32), pltpu.SemaphoreType.DMA((2,))]),
        out_shape=jax.ShapeDtypeStruct((R, D), BF16),
        compiler_params=_cparams(1, 24), name="moe_gather",
    )(src_tok, hf)


def _moe_mm_kernel(te_ref, nx_ref, nu_ref, x_ref, *rest, e0, n_w):
    w_hbms = rest[:n_w]
    o_ref = rest[n_w]
    stages, wbfs, sem = rest[n_w + 1:2 * n_w + 1], rest[2 * n_w + 1:3 * n_w + 1], rest[3 * n_w + 1]
    j, i, nj = pl.program_id(0), pl.program_id(1), pl.num_programs(0)
    tn = stages[0].shape[1]

    def block_copies(e, jj):
        cols = pl.ds(pl.multiple_of(jj * tn, tn), tn)
        return [pltpu.make_async_copy(w.at[e0 + e, :, cols], st, sem.at[k])
                for k, (w, st) in enumerate(zip(w_hbms, stages))]

    fresh = jnp.logical_or(i == 0, te_ref[i] != te_ref[jnp.maximum(i - 1, 0)])

    @pl.when(fresh)
    def _():
        @pl.when(jnp.logical_and(i == 0, j == 0))
        def _():
            for cp in block_copies(te_ref[0], 0):
                cp.start()

        for cp in block_copies(te_ref[i], j):
            cp.wait()
        for st, wbf in zip(stages, wbfs):
            _cast_rows(st, wbf)
        nxt = nx_ref[i]

        @pl.when(nxt >= 0)
        def _():
            for cp in block_copies(nxt, j):
                cp.start()

        @pl.when(jnp.logical_and(nxt < 0, j + 1 < nj))
        def _():
            for cp in block_copies(te_ref[0], j + 1):
                cp.start()

    @pl.when(i < nu_ref[0])
    def _():
        x = x_ref[...]
        if n_w == 2:
            o_ref[...] = (_silu(_dot(x, wbfs[0][...])) * _dot(x, wbfs[1][...])).astype(o_ref.dtype)
        else:
            o_ref[...] = _dot(x, wbfs[0][...]).astype(o_ref.dtype)

    @pl.when(i >= nu_ref[0])
    def _():
        o_ref[...] = jnp.zeros_like(o_ref)


def _moe_mm(x, ws, e0, plan, tn, out_dtype, name):
    tile_expert, next_expert, n_used = plan
    R, K = x.shape
    N = ws[0].shape[2]
    T = MOE_TILE
    n_w = len(ws)
    return pl.pallas_call(
        functools.partial(_moe_mm_kernel, e0=e0, n_w=n_w),
        grid_spec=pltpu.PrefetchScalarGridSpec(
            num_scalar_prefetch=3, grid=(N // tn, R // T),
            in_specs=[pl.BlockSpec((T, K), lambda j, i, te, nx, nu: (i, 0))]
                     + [pl.BlockSpec(memory_space=pl.ANY)] * n_w,
            out_specs=pl.BlockSpec((T, tn), lambda j, i, te, nx, nu: (i, j)),
            scratch_shapes=[pltpu.VMEM((K, tn), F32)] * n_w + [pltpu.VMEM((K, tn), BF16)] * n_w
                           + [pltpu.SemaphoreType.DMA((n_w,))]),
        out_shape=jax.ShapeDtypeStruct((R, N), out_dtype),
        compiler_params=_cparams(2, 48), name=name,
    )(tile_expert, next_expert, n_used, x, *ws)


def moe_ffn_in(xs, w1, w3, e0, plan):
    return _moe_mm(xs, (w1, w3), e0, plan, 512, BF16, "moe_ffn_in")


def moe_ffn_out(u, w2, e0, plan):
    return _moe_mm(u, (w2,), e0, plan, 2048, F32, "moe_ffn_out")


def _moe_combine_kernel(dest_ref, y_hbm, pw_ref, x_ref, g_ref, *rest, with_norm):
    if with_norm:
        ng_ref, o_ref, buf, sem = rest
    else:
        o_ref, buf, sem = rest
    T = o_ref.shape[0]
    slot = _gather_tile(lambda r: dest_ref[r], y_hbm, buf, sem, 2 * T)
    pw = pw_ref[...]
    mix = pw[:, 0:1] * buf[slot, 0:T, :] + pw[:, 1:2] * buf[slot, T:2 * T, :]
    x = x_ref[...] + g_ref[...] * mix
    if with_norm:
        x = x * lax.rsqrt(jnp.mean(x * x, axis=-1, keepdims=True) + RMS_EPS) * ng_ref[...]
    o_ref[...] = x


def moe_combine(y, dest, pw, xres, gate, n_lat, n_samples, norm_g=None):
    M, D = xres.shape
    G = gate.shape[0]
    T = 128
    dest = dest.reshape(M // T, T, 2).transpose(0, 2, 1)
    in_specs = [pl.BlockSpec(memory_space=pl.ANY),
                pl.BlockSpec((T, 128), lambda i, d: (i, 0)),
                pl.BlockSpec((T, D), lambda i, d: (i, 0)),
                pl.BlockSpec((None, 1, D), lambda i, d: (_grp_of_tile(i, T, n_lat, n_samples), 0, 0))]
    args = [dest.reshape(-1), y, pw, xres, gate.reshape(G, 1, D)]
    if norm_g is not None:
        in_specs.append(pl.BlockSpec((1, D), lambda i, d: (0, 0)))
        args.append(norm_g.reshape(1, D))
    return pl.pallas_call(
        functools.partial(_moe_combine_kernel, with_norm=norm_g is not None),
        grid_spec=pltpu.PrefetchScalarGridSpec(
            num_scalar_prefetch=1, grid=(M // T,),
            in_specs=in_specs,
            out_specs=pl.BlockSpec((T, D), lambda i, d: (i, 0)),
            scratch_shapes=[pltpu.VMEM((2, 2 * T, D), F32), pltpu.SemaphoreType.DMA((2,))]),
        out_shape=jax.ShapeDtypeStruct((M, D), F32),
        compiler_params=_cparams(1, 32), name="moe_combine",
    )(*args)


def kernel(x, c, ctx, c_ctx, norm1_g, norm2_g, w_ada, b_ada, w_in, na_rpb, diff_lambda, diff_subln_g,
           mla_q_norm_g, mla_w_q_up, mla_kv_norm_g, mla_w_kv_up, hgrn_lower_bounds, hgrn_norm_g,
           w_branch, w_gate, b_gate, w_out, ffn_w1, ffn_w3, ffn_w2, moe_router, moe_w1, moe_w3, moe_w2,
           final_norm_g):
    B, N, D = x.shape
    NC = ctx.shape[1]
    L = w_ada.shape[0]
    n_lat, n_all = B * N, B * N + B * NC
    BW = w_branch.shape[2]
    n_heads = BW // HEAD_DIM
    HG0 = w_in.shape[2] - 5 * BW
    assert N % ROW_TILE == 0 and (B * NC) % ROW_TILE == 0 and NC % HGRN_CHUNK == 0

    xs = jnp.concatenate([x.reshape(n_lat, D), ctx.reshape(B * NC, D)], axis=0)

    cond8 = jnp.zeros((8, D), F32).at[:B].set(c).at[B].set(c_ctx)
    mods = adaln_all(cond8, w_ada, b_ada)[:, :B + 1].reshape(L, B + 1, 6, D)

    lb_all = jnp.cumsum(jax.nn.softmax(hgrn_lower_bounds.astype(F32), axis=0), axis=0)
    lb_all = lb_all - lb_all[0:1]

    w_in_t = jnp.swapaxes(w_in, 1, 2)

    cosd, sind = _rope_tables(N, B * NC, 2 * 64, ROW_TILE)
    cosm, sinm = _rope_tables(N, B * NC, 64, ROW_TILE)

    RQ, RKV = mla_w_q_up.shape[1], mla_w_kv_up.shape[1]
    wq4 = mla_w_q_up.reshape(L, RQ, n_heads, HEAD_DIM + MLA_ROPE)
    wq_cat = jnp.pad(wq4, ((0, 0), (0, 0), (0, 0), (0, MLA_SLOT - HEAD_DIM - MLA_ROPE))).reshape(L, RQ, -1)
    wkv4 = mla_w_kv_up.reshape(L, RKV, n_heads, 2 * HEAD_DIM)
    wkn = wkv4[..., :HEAD_DIM].reshape(L, RKV, -1)
    wvt = jnp.swapaxes(wkv4[..., HEAD_DIM:].reshape(L, RKV, -1), 1, 2)

    for l in range(L):
        need_ctx = l < L - 1
        n_rows = n_all if need_ctx else n_lat
        lambda_init = 0.8 - 0.6 * math.exp(-0.3 * l)
        sh1, sc1, g1, sh2, sc2, g2 = (mods[l, :, k] for k in range(6))

        h = norm_mod(xs, norm1_g[l], sh1, sc1, N, B)
        z1 = wt_stream_matmul(h, w_in_t, l, 0, 15 * 512, BF16, "w_in_attn")
        z2 = wt_stream_matmul(h, w_in_t, l, HG0, 5 * BW, F32, "w_in_hgrn")

        bias = _na_bias_tables(na_rpb[l])
        o_a = na_attention(z1, bias, B, N, NC, n_heads)
        dq, dk, dvt = diff_rope(z1, cosd, sind, N, n_lat)
        lam_rows = jnp.zeros((8, HEAD_DIM), F32).at[:4].set(diff_lambda[l]).at[4].set(lambda_init)
        o_d = diff_attention(dq, dk, dvt, lam_rows, diff_subln_g[l], lambda_init, B, N, NC, False)
        qcat, kcat, mvt = mla_prep(z1, cosm, sinm, mla_q_norm_g[l], wq_cat[l], mla_kv_norm_g[l], wkn[l], wvt[l],
                                   N, n_lat)
        o_m = mla_attention(qcat, kcat, mvt, B, N, NC, False)
        o_f, o_b = hgrn_scan(z2, lb_all[l], B, N, NC)
        o_r = hgrn_readout(o_f, o_b, z2, hgrn_norm_g[l], n_rows)
        if need_ctx:
            o_a = jnp.concatenate([o_a, na_ctx_attention(z1, B, N, NC, n_heads)], axis=0)
            o_d = jnp.concatenate([o_d, diff_attention(dq, dk, dvt, lam_rows, diff_subln_g[l], lambda_init,
                                                       B, N, NC, True)], axis=0)
            o_m = jnp.concatenate([o_m, mla_attention(qcat, kcat, mvt, B, N, NC, True)], axis=0)

        s = merge_branches(h, (o_a, o_d, o_m, o_r), w_gate, b_gate, w_branch, l, n_rows)
        G = B + 1
        tmo = 2 * ROW_TILE
        xs = _mm_call(s, w_out, l, n_rows=n_rows, tn=512, tm=tmo, out_dtype=F32,
                      epilogue=_residual_epilogue, extras=(xs, g1.reshape(G, 1, D)),
                      extra_specs=(pl.BlockSpec((tmo, 512), lambda j, i: (i, j)), _group_spec(tmo, 512, N, B)),
                      name="w_out")

        j = l // 2
        if l % 2 == 0:
            h2 = norm_mod(xs, norm2_g[l], sh2, sc2, N, B)
            u = ffn_in(h2, ffn_w1, ffn_w3, j, n_rows)
            xs = ffn_out(u, ffn_w2, j, xs, g2, n_rows, N, B)
        else:
            hf, comb = norm_mod(xs, norm2_g[l], sh2, sc2, N, B, w_router=moe_router[j])
            NM, E, _, FE = moe_w1.shape
            src_tok, plan, dest, pw = moe_route_plan(comb, E)
            xg = moe_gather(hf, src_tok)
            u = moe_ffn_in(xg, moe_w1.reshape(NM * E, D, FE), moe_w3.reshape(NM * E, D, FE), j * E, plan)
            y = moe_ffn_out(u, moe_w2.reshape(NM * E, FE, D), j * E, plan)
            xs = moe_combine(y, dest, pw, xs, g2, N, B, norm_g=final_norm_g if l == L - 1 else None)

    if L % 2 == 1:
        xs = final_norm(xs[:n_lat], final_norm_g)
    return xs.reshape(B, N, D)
```

```python
import functools
import math

import numpy as np
import jax
import jax.numpy as jnp
from jax import lax
from jax.experimental import pallas as pl
from jax.experimental.pallas import tpu as pltpu

F32 = jnp.float32
BF16 = jnp.bfloat16

GRID_W = 64
RMS_EPS = 1e-6
ROPE_BASE = 10000.0
NA_KH, NA_KW = 8, 16
HEAD_DIM = 128
MLA_ROPE = 64
HGRN_CHUNK = 64
MOE_EXPERTS = 8
NEG_BIG = -1e30

VMEM_LIMIT_V7X = 60000 * 1024
ROW_TILE = 512


def _even_row_tile(n_rows, cap=1152):
    return next(t for t in range(cap, 15, -16) if n_rows % t == 0)


def _cparams(n_axes, vmem_mb):
    return pltpu.CompilerParams(dimension_semantics=("arbitrary",) * n_axes,
                                vmem_limit_bytes=min(int(vmem_mb * 2**20), VMEM_LIMIT_V7X))


def _silu(x):
    return x * jax.nn.sigmoid(x)


def _dot(a, b):
    return jnp.dot(a, b, preferred_element_type=F32)


def _dot_nt(a, b):
    return lax.dot_general(a, b, (((1,), (1,)), ((), ())), preferred_element_type=F32)


def _dot_tn(a, b):
    return lax.dot_general(a, b, (((0,), (0,)), ((), ())), preferred_element_type=F32)


def _adaln_kernel(c_ref, w_ref, b_ref, o_ref):
    @pl.when(pl.program_id(1) == 0)
    def _():
        o_ref[...] = jnp.broadcast_to(b_ref[...], o_ref.shape)

    x = _silu(c_ref[...]).astype(BF16)
    o_ref[...] += _dot(x, w_ref[...].astype(BF16))


def adaln_all(cond8, w_ada, b_ada):
    L, D, N6 = w_ada.shape
    tk = 128
    cond_k = cond8.reshape(8, D // tk, tk).transpose(1, 0, 2)
    return pl.pallas_call(
        _adaln_kernel,
        grid=(L, D // tk),
        in_specs=[pl.BlockSpec((None, 8, tk), lambda l, k: (k, 0, 0)),
                  pl.BlockSpec((None, tk, N6), lambda l, k: (l, k, 0)),
                  pl.BlockSpec((None, 1, N6), lambda l, k: (l, 0, 0))],
        out_specs=pl.BlockSpec((None, 8, N6), lambda l, k: (l, 0, 0)),
        out_shape=jax.ShapeDtypeStruct((L, 8, N6), F32),
        compiler_params=_cparams(2, 48),
        name="adaln",
    )(cond_k, w_ada, b_ada.reshape(L, 1, N6))


def _norm_mod_kernel(x_ref, g_ref, sh_ref, sc_ref, *rest, with_router):
    x = x_ref[...]
    y = x * lax.rsqrt(jnp.mean(x * x, axis=-1, keepdims=True) + RMS_EPS) * g_ref[...]
    h = y * (1.0 + sc_ref[...]) + sh_ref[...]
    if not with_router:
        (h_ref,) = rest
        h_ref[...] = h.astype(BF16)
        return
    wr_ref, h_ref, comb_ref = rest
    h_ref[...] = h
    logits = jnp.dot(h, wr_ref[...], precision=lax.Precision.HIGHEST, preferred_element_type=F32)
    lane = lax.broadcasted_iota(jnp.int32, logits.shape, 1).astype(F32)
    logits = jnp.where(lane < MOE_EXPERTS, logits, -jnp.inf)
    m1 = jnp.max(logits, axis=-1, keepdims=True)
    i1 = jnp.min(jnp.where(logits == m1, lane, 128.0), axis=-1, keepdims=True)
    rest_l = jnp.where(lane == i1, -jnp.inf, logits)
    m2 = jnp.max(rest_l, axis=-1, keepdims=True)
    i2 = jnp.min(jnp.where(rest_l == m2, lane, 128.0), axis=-1, keepdims=True)
    e2 = jnp.exp(m2 - m1)
    p1 = 1.0 / (1.0 + e2)
    p2 = e2 / (1.0 + e2)
    comb_ref[...] = jnp.where(lane == i1, p1, 0.0) + jnp.where(lane == i2, p2, 0.0)


def _grp_of_tile(i, tm, lat_rows_per_sample, n_samples):
    return jnp.minimum((i * tm) // lat_rows_per_sample, n_samples)


def norm_mod(x, g, shift, scale, n_lat, n_samples, w_router=None):
    M, D = x.shape
    tm = 256
    G = shift.shape[0]
    grp = lambda i: (_grp_of_tile(i, tm, n_lat, n_samples), 0, 0)
    in_specs = [pl.BlockSpec((tm, D), lambda i: (i, 0)),
                pl.BlockSpec((1, D), lambda i: (0, 0)),
                pl.BlockSpec((None, 1, D), grp),
                pl.BlockSpec((None, 1, D), grp)]
    args = [x, g.reshape(1, D), shift.reshape(G, 1, D), scale.reshape(G, 1, D)]
    out_specs = [pl.BlockSpec((tm, D), lambda i: (i, 0))]
    out_shape = [jax.ShapeDtypeStruct((M, D), BF16 if w_router is None else F32)]
    if w_router is not None:
        wr = jnp.zeros((D, 128), F32).at[:, :w_router.shape[1]].set(w_router)
        in_specs.append(pl.BlockSpec((D, 128), lambda i: (0, 0)))
        args.append(wr)
        out_specs.append(pl.BlockSpec((tm, 128), lambda i: (i, 0)))
        out_shape.append(jax.ShapeDtypeStruct((M, 128), F32))
    outs = pl.pallas_call(
        functools.partial(_norm_mod_kernel, with_router=w_router is not None),
        grid=(M // tm,),
        in_specs=in_specs, out_specs=out_specs, out_shape=out_shape,
        compiler_params=_cparams(1, 40),
        name="norm_mod_router" if w_router is not None else "norm_mod",
    )(*args)
    return outs if w_router is not None else outs[0]


def _final_norm_kernel(x_ref, g_ref, o_ref):
    x = x_ref[...]
    o_ref[...] = x * lax.rsqrt(jnp.mean(x * x, axis=-1, keepdims=True) + RMS_EPS) * g_ref[...]


def final_norm(x, g):
    M, D = x.shape
    tm = 256
    return pl.pallas_call(
        _final_norm_kernel, grid=(M // tm,),
        in_specs=[pl.BlockSpec((tm, D), lambda i: (i, 0)), pl.BlockSpec((1, D), lambda i: (0, 0))],
        out_specs=pl.BlockSpec((tm, D), lambda i: (i, 0)),
        out_shape=jax.ShapeDtypeStruct((M, D), F32),
        compiler_params=_cparams(1, 40), name="final_norm",
    )(x, g.reshape(1, D))


def _cast_rows(src_ref, dst_ref, rows=512):
    n = src_ref.shape[0]
    for r0 in range(0, n, rows):
        r1 = min(r0 + rows, n)
        dst_ref[r0:r1, :] = src_ref[r0:r1, :].astype(BF16)


def _stream_mm_kernel(x_ref, w_hbm, *rest, layer, n_extra, epilogue):
    extras = rest[:n_extra]
    o_ref, stage, wbf, sem = rest[n_extra:]
    j, i, nj = pl.program_id(0), pl.program_id(1), pl.num_programs(0)
    tn = stage.shape[1]

    def block_copy(jj):
        return pltpu.make_async_copy(w_hbm.at[layer, :, pl.ds(pl.multiple_of(jj * tn, tn), tn)], stage, sem)

    @pl.when(i == 0)
    def _():
        @pl.when(j == 0)
        def _():
            block_copy(0).start()

        block_copy(j).wait()
        _cast_rows(stage, wbf)

        @pl.when(j + 1 < nj)
        def _():
            block_copy(j + 1).start()

    acc = _dot(x_ref[...], wbf[...])
    if epilogue is not None:
        acc = epilogue(acc, *[e[...] for e in extras])
    o_ref[...] = acc.astype(o_ref.dtype)


def _mm_call(x, w, layer, *, n_rows, tn, tm, out_dtype, epilogue=None, extras=(), extra_specs=(), name="mm"):
    K, N = w.shape[1:]
    in_specs = [pl.BlockSpec((tm, K), lambda j, i: (i, 0)), pl.BlockSpec(memory_space=pl.ANY)]
    in_specs += list(extra_specs)
    out_bytes = jnp.dtype(out_dtype).itemsize
    vmem = (2 * tm * K * 2 + K * tn * 4 + K * tn * 2 + 2 * tm * tn * out_bytes + 3 * tm * tn * 4
            + len(extras) * 2 * tm * tn * 4) / 2**20 + 6
    return pl.pallas_call(
        functools.partial(_stream_mm_kernel, layer=layer, n_extra=len(extras), epilogue=epilogue),
        grid=(N // tn, pl.cdiv(n_rows, tm)),
        in_specs=in_specs,
        out_specs=pl.BlockSpec((tm, tn), lambda j, i: (i, j)),
        out_shape=jax.ShapeDtypeStruct((n_rows, N), out_dtype),
        scratch_shapes=[pltpu.VMEM((K, tn), F32), pltpu.VMEM((K, tn), BF16), pltpu.SemaphoreType.DMA(())],
        compiler_params=_cparams(2, vmem),
        name=name,
    )(x, w, *extras)


def _residual_epilogue(acc, xres, gate):
    return xres + gate * acc


def _wt_stream_mm_kernel(x_ref, wt_hbm, o_ref, stage, wbf, sem, *, layer, row0):
    j, i, nj = pl.program_id(0), pl.program_id(1), pl.num_programs(0)
    tn, K = stage.shape

    def block_copy(jj):
        return pltpu.make_async_copy(wt_hbm.at[layer, pl.ds(row0 + jj * tn, tn), :], stage, sem)

    @pl.when(i == 0)
    def _():
        @pl.when(j == 0)
        def _():
            block_copy(0).start()

        block_copy(j).wait()
        for c in range(0, K, tn):
            wbf[c:c + tn, :] = jnp.transpose(stage[:, c:c + tn]).astype(BF16)

        @pl.when(j + 1 < nj)
        def _():
            block_copy(j + 1).start()

    o_ref[...] = _dot(x_ref[...], wbf[...]).astype(o_ref.dtype)


def wt_stream_matmul(x, wt, layer, row0, n_cols, out_dtype, name):
    M, K = x.shape
    tm, tn = _even_row_tile(M), 512
    assert n_cols % tn == 0 and K % tn == 0 and row0 % 8 == 0
    return pl.pallas_call(
        functools.partial(_wt_stream_mm_kernel, layer=layer, row0=row0),
        grid=(n_cols // tn, pl.cdiv(M, tm)),
        in_specs=[pl.BlockSpec((tm, K), lambda j, i: (i, 0)), pl.BlockSpec(memory_space=pl.ANY)],
        out_specs=pl.BlockSpec((tm, tn), lambda j, i: (i, j)),
        out_shape=jax.ShapeDtypeStruct((M, n_cols), out_dtype),
        scratch_shapes=[pltpu.VMEM((tn, K), F32), pltpu.VMEM((K, tn), BF16), pltpu.SemaphoreType.DMA(())],
        compiler_params=_cparams(2, 48), name=name,
    )(x, wt)


def _rope_tables(n_tokens, n_ident, group, tile_rows):
    half = group // 2
    q = half // 2
    pos = np.arange(n_tokens)
    row, col = pos // GRID_W, pos % GRID_W
    freqs = ROPE_BASE ** (-(np.arange(q, dtype=np.float64) / q))
    lane = np.arange(128)
    in_group = lane % group
    axis_pos = np.where((in_group < half)[None, :], row[:, None], col[:, None]).astype(np.float64)
    ang = axis_pos * freqs[(in_group % half) % q][None, :]
    sign = np.where((in_group % half) < q, -1.0, 1.0)[None, :]
    cos, sin = np.cos(ang), np.sin(ang) * sign
    n_pad = -(-n_ident // tile_rows) * tile_rows
    cos = np.concatenate([cos, np.ones((n_pad, 128))], axis=0)
    sin = np.concatenate([sin, np.zeros((n_pad, 128))], axis=0)
    return jnp.asarray(cos, F32), jnp.asarray(sin, F32)


def _rope_apply(x, cos, sin, q):
    lane = lax.broadcasted_iota(jnp.int32, x.shape, 1)
    partner = jnp.where((lane % (2 * q)) < q, pltpu.roll(x, 128 - q, 1), pltpu.roll(x, q, 1))
    return x * cos + partner * sin


LOG2E = math.log2(math.e)


def _diff_rope_kernel(q_ref, k_ref, v_ref, cos_ref, sin_ref, qo_ref, ko_ref, vt_ref, *, q_scale):
    vt_ref[...] = jnp.transpose(v_ref[...].astype(F32)).astype(BF16)
    cos, sin = cos_ref[...], sin_ref[...]
    for src, dst, mult in ((q_ref, qo_ref, q_scale), (k_ref, ko_ref, None)):
        for c in range(src.shape[1] // 128):
            sl = slice(c * 128, (c + 1) * 128)
            y = _rope_apply(src[:, sl].astype(F32), cos, sin, 32)
            dst[:, sl] = (y if mult is None else y * mult).astype(BF16)


def _table_block(i, tm, n_tok, n_lat_total):
    return jnp.where(i * tm < n_lat_total, (i * tm % n_tok) // tm, n_tok // tm)


def diff_rope(z1, cosd, sind, n_tok, n_lat_total):
    M = z1.shape[0]
    tm = ROW_TILE
    W = 1024
    tb = lambda i: (_table_block(i, tm, n_tok, n_lat_total), 0)
    return pl.pallas_call(
        functools.partial(_diff_rope_kernel, q_scale=HEAD_DIM ** -0.5 * LOG2E), grid=(M // tm,),
        in_specs=[pl.BlockSpec((tm, W), lambda i: (i, 3)), pl.BlockSpec((tm, W), lambda i: (i, 4)),
                  pl.BlockSpec((tm, W), lambda i: (i, 5)),
                  pl.BlockSpec((tm, 128), tb), pl.BlockSpec((tm, 128), tb)],
        out_specs=[pl.BlockSpec((tm, W), lambda i: (i, 0))] * 2 + [pl.BlockSpec((W, tm), lambda i: (0, i))],
        out_shape=[jax.ShapeDtypeStruct((M, W), BF16)] * 2 + [jax.ShapeDtypeStruct((W, M), BF16)],
        compiler_params=_cparams(1, 40), name="diff_rope",
    )(z1, z1, z1, cosd, sind)


MLA_SLOT = 2 * HEAD_DIM


def _mla_prep_kernel(cq_ref, ckva_ref, ckvb_ref, kr_ref, cos_ref, sin_ref, gq_ref, gkv_ref, wq_ref, wkn_ref,
                     wvt_ref, qcat_ref, kcat_ref, vt_ref, wq_bf, wkn_bf, wvt_bf, *, q_scale):
    @pl.when(pl.program_id(0) == 0)
    def _():
        wq_bf[...] = wq_ref[...].astype(BF16)
        wkn_bf[...] = wkn_ref[...].astype(BF16)
        wvt_bf[...] = wvt_ref[...].astype(BF16)

    cos, sin = cos_ref[...], sin_ref[...]
    d = HEAD_DIM
    n_heads = qcat_ref.shape[1] // MLA_SLOT
    cq = cq_ref[...].astype(F32)
    cqn = (cq * lax.rsqrt(jnp.mean(cq * cq, axis=-1, keepdims=True) + RMS_EPS) * gq_ref[...]).astype(BF16)
    q = _dot(cqn, wq_bf[...])
    a = ckva_ref[...].astype(F32)
    b = ckvb_ref[...].astype(F32)
    ha = a.shape[1]
    ms = (jnp.sum(a * a, axis=-1, keepdims=True) + jnp.sum(b * b, axis=-1, keepdims=True)) / (2 * ha)
    r = lax.rsqrt(ms + RMS_EPS)
    g = gkv_ref[...]
    ckvn = jnp.concatenate([(a * r * g[:, :ha]).astype(BF16), (b * r * g[:, ha:]).astype(BF16)], axis=1)
    kn = _dot(ckvn, wkn_bf[...])
    lane = lax.broadcasted_iota(jnp.int32, cos.shape, 1)
    kr = jnp.where(lane < MLA_ROPE, _rope_apply(kr_ref[...].astype(F32), cos, sin, 16), 0.0).astype(BF16)
    for h in range(n_heads):
        c0 = h * MLA_SLOT
        qcat_ref[:, c0:c0 + d] = (q[:, c0:c0 + d] * q_scale).astype(BF16)
        qcat_ref[:, c0 + d:c0 + 2 * d] = (_rope_apply(q[:, c0 + d:c0 + 2 * d], cos, sin, 16) * q_scale).astype(BF16)
        kcat_ref[:, c0:c0 + d] = kn[:, h * d:(h + 1) * d].astype(BF16)
        kcat_ref[:, c0 + d:c0 + 2 * d] = kr
    vt_ref[...] = _dot_nt(wvt_bf[...], ckvn).astype(BF16)


def mla_prep(z1, cosm, sinm, gq, wq_cat, gkv, wkn, wvt, n_tok, n_lat_total):
    M = z1.shape[0]
    tm = ROW_TILE
    RQ, NQ = wq_cat.shape
    RKV, NKN = wkn.shape
    tb = lambda i: (_table_block(i, tm, n_tok, n_lat_total), 0)
    const = lambda i: (0, 0)
    return pl.pallas_call(
        functools.partial(_mla_prep_kernel, q_scale=(HEAD_DIM + MLA_ROPE) ** -0.5 * LOG2E), grid=(M // tm,),
        in_specs=[pl.BlockSpec((tm, RQ), lambda i: (i, 6144 // RQ)),
                  pl.BlockSpec((tm, RKV // 2), lambda i: (i, 6912 // (RKV // 2))),
                  pl.BlockSpec((tm, RKV // 2), lambda i: (i, 6912 // (RKV // 2) + 1)),
                  pl.BlockSpec((tm, 128), lambda i: (i, 7424 // 128)),
                  pl.BlockSpec((tm, 128), tb), pl.BlockSpec((tm, 128), tb),
                  pl.BlockSpec((1, RQ), const), pl.BlockSpec((1, RKV), const),
                  pl.BlockSpec((RQ, NQ), const), pl.BlockSpec((RKV, NKN), const), pl.BlockSpec((NKN, RKV), const)],
        out_specs=[pl.BlockSpec((tm, NQ), lambda i: (i, 0)),
                   pl.BlockSpec((tm, NQ), lambda i: (i, 0)),
                   pl.BlockSpec((NKN, tm), lambda i: (0, i))],
        out_shape=[jax.ShapeDtypeStruct((M, NQ), BF16), jax.ShapeDtypeStruct((M, NQ), BF16),
                   jax.ShapeDtypeStruct((NKN, M), BF16)],
        scratch_shapes=[pltpu.VMEM((RQ, NQ), BF16), pltpu.VMEM((RKV, NKN), BF16), pltpu.VMEM((NKN, RKV), BF16)],
        compiler_params=_cparams(1, 56), name="mla_prep",
    )(z1, z1, z1, z1, cosm, sinm, gq.reshape(1, RQ), gkv.reshape(1, RKV), wq_cat, wkn, wvt)


def _softmax_parts(s_list):
    m = functools.reduce(jnp.maximum, [jnp.max(s, axis=-1, keepdims=True) for s in s_list])
    p_list = [jnp.exp(s - m) for s in s_list]
    l = functools.reduce(lambda a, b: a + b, [jnp.sum(p, axis=-1, keepdims=True) for p in p_list])
    return p_list, l


def _softmax_parts_t(st_list):
    m = functools.reduce(jnp.maximum, [jnp.max(s, axis=0, keepdims=True) for s in st_list])
    p_list = [jnp.exp2(s - m) for s in st_list]
    l = functools.reduce(lambda a, b: a + b, [jnp.sum(p, axis=0, keepdims=True) for p in p_list])
    return p_list, l


def _query_rows(n_samples, n_tok, n_ctx, tq, ctx_queries):
    if ctx_queries:
        nq = n_ctx // tq
        first = n_samples * n_tok // tq
        return nq, (lambda b, t: first + b * nq + t), n_samples * n_ctx, (lambda b, t: b * nq + t)
    nq = n_tok // tq
    qrow = lambda b, t: b * nq + t
    return nq, qrow, n_samples * n_tok, qrow


def _mla_attn_kernel(q_ref, *rest, with_latent):
    if with_latent:
        kl_ref, vtl_ref, kc_ref, vtc_ref, o_ref = rest
    else:
        kc_ref, vtc_ref, o_ref = rest
    for hh in range(q_ref.shape[1] // MLA_SLOT):
        ds = slice(hh * MLA_SLOT, (hh + 1) * MLA_SLOT)
        dv = slice(hh * HEAD_DIM, (hh + 1) * HEAD_DIM)
        q = q_ref[:, ds]
        k_list = [kc_ref[:, ds]] + ([kl_ref[:, ds]] if with_latent else [])
        vt_list = [vtc_ref[dv, :]] + ([vtl_ref[dv, :]] if with_latent else [])
        p_list, l = _softmax_parts_t([_dot_nt(k, q) for k in k_list])
        ot = functools.reduce(lambda a, b: a + b, [_dot(vt, p.astype(BF16)) for vt, p in zip(vt_list, p_list)])
        o_ref[:, dv] = jnp.transpose(ot * (1.0 / l)).astype(BF16)


def mla_attention(qcat, kcat, vt, n_samples, n_tok, n_ctx, ctx_queries):
    tq = min(512, n_ctx) if ctx_queries else 512
    HP = 1
    nq, qrow, out_rows, orow = _query_rows(n_samples, n_tok, n_ctx, tq, ctx_queries)
    cblk = n_samples * n_tok // n_ctx
    WS, WV = HP * MLA_SLOT, HP * HEAD_DIM
    in_specs = [pl.BlockSpec((tq, WS), lambda b, hp, t: (qrow(b, t), hp))]
    args = [qcat]
    if not ctx_queries:
        in_specs += [pl.BlockSpec((n_tok, WS), lambda b, hp, t: (b, hp)),
                     pl.BlockSpec((WV, n_tok), lambda b, hp, t: (hp, b))]
        args += [kcat, vt]
    in_specs += [pl.BlockSpec((n_ctx, WS), lambda b, hp, t: (cblk + b, hp)),
                 pl.BlockSpec((WV, n_ctx), lambda b, hp, t: (hp, cblk + b))]
    args += [kcat, vt]
    return pl.pallas_call(
        functools.partial(_mla_attn_kernel, with_latent=not ctx_queries),
        grid=(n_samples, vt.shape[0] // WV, nq),
        in_specs=in_specs,
        out_specs=pl.BlockSpec((tq, WV), lambda b, hp, t: (orow(b, t), hp)),
        out_shape=jax.ShapeDtypeStruct((out_rows, vt.shape[0]), BF16),
        compiler_params=_cparams(3, 56), name="mla_attn_ctx" if ctx_queries else "mla_attn",
    )(*args)


def _diff_attn_kernel(q_ref, *rest, with_latent, out_scale):
    if with_latent:
        kl_ref, vtl_ref, kc_ref, vtc_ref, lam_ref, g_ref, o_ref = rest
    else:
        kc_ref, vtc_ref, lam_ref, g_ref, o_ref = rest
    lp = lam_ref[...]
    lam = (jnp.exp(jnp.sum(lp[0:1] * lp[1:2], axis=-1, keepdims=True))
           - jnp.exp(jnp.sum(lp[2:3] * lp[3:4], axis=-1, keepdims=True)) + lam_ref[4:5, 0:1])
    d = HEAD_DIM
    k_refs = [kc_ref] + ([kl_ref] if with_latent else [])
    vt_refs = [vtc_ref] + ([vtl_ref] if with_latent else [])
    parts = []
    for half in range(2):
        sl = slice(half * d, (half + 1) * d)
        q = q_ref[:, sl]
        p_list, l = _softmax_parts_t([_dot_nt(k[:, sl], q) for k in k_refs])
        parts.append((p_list, 1.0 / l))
    (p1_list, inv1), (p2_list, inv2) = parts
    r = lam * inv2 / inv1
    ot = functools.reduce(lambda a, b: a + b,
                          [_dot(vt[...], (p1 - p2 * r).astype(BF16))
                           for p1, p2, vt in zip(p1_list, p2_list, vt_refs)])
    o = jnp.transpose(ot * inv1)
    y = o * lax.rsqrt(jnp.mean(o * o, axis=-1, keepdims=True) + RMS_EPS) * g_ref[...]
    o_ref[...] = (y * out_scale).astype(BF16)


def diff_attention(dq, dk, vt, lam_rows, subln_g, lambda_init, n_samples, n_tok, n_ctx, ctx_queries):
    tq = min(512, n_ctx) if ctx_queries else 512
    W = 2 * HEAD_DIM
    nq, qrow, out_rows, orow = _query_rows(n_samples, n_tok, n_ctx, tq, ctx_queries)
    cblk = n_samples * n_tok // n_ctx
    in_specs = [pl.BlockSpec((tq, W), lambda b, h, t: (qrow(b, t), h))]
    args = [dq]
    if not ctx_queries:
        in_specs += [pl.BlockSpec((n_tok, W), lambda b, h, t: (b, h)),
                     pl.BlockSpec((W, n_tok), lambda b, h, t: (h, b))]
        args += [dk, vt]
    in_specs += [pl.BlockSpec((n_ctx, W), lambda b, h, t: (cblk + b, h)),
                 pl.BlockSpec((W, n_ctx), lambda b, h, t: (h, cblk + b)),
                 pl.BlockSpec((8, HEAD_DIM), lambda b, h, t: (0, 0)),
                 pl.BlockSpec((1, W), lambda b, h, t: (0, 0))]
    args += [dk, vt, lam_rows, subln_g.reshape(1, W)]
    n_heads = dk.shape[1] // W
    return pl.pallas_call(
        functools.partial(_diff_attn_kernel, with_latent=not ctx_queries, out_scale=1.0 - lambda_init),
        grid=(n_samples, n_heads, nq),
        in_specs=in_specs,
        out_specs=pl.BlockSpec((tq, W), lambda b, h, t: (orow(b, t), h)),
        out_shape=jax.ShapeDtypeStruct((out_rows, dk.shape[1]), BF16),
        compiler_params=_cparams(3, 56), name="diff_attn_ctx" if ctx_queries else "diff_attn",
    )(*args)


NA_DY = 2 * NA_KH - 1


def _na_bias_tables(rpb):
    W = GRID_W
    H = rpb.shape[0]
    qc = np.arange(W)[:, None]
    kc = np.arange(W)[None, :]
    cs = np.clip(qc - NA_KW // 2, 0, W - NA_KW)
    col_ok = (kc >= cs) & (kc < cs + NA_KW)
    dx = np.clip(kc - qc + NA_KW - 1, 0, 2 * NA_KW - 2)
    by_dx = jnp.take(rpb.astype(F32), jnp.asarray(dx.reshape(W * W)), axis=2).reshape(H, NA_DY, W, W)
    by_dx = jnp.where(jnp.asarray(col_ok)[None, None], by_dx, NEG_BIG)
    neg = jnp.full((H, 1, W, W), NEG_BIG, F32)
    padded = jnp.concatenate([neg, by_dx, neg, neg], axis=1)
    return jnp.concatenate([padded[:, :-1], padded[:, 1:]], axis=-1)


def _na_attn_kernel(q_ref, kl_ref, vl_ref, kc_ref, vc_ref, bias_ref, o_ref, *, scale, n_rows):
    rb = pl.program_id(2)
    r0 = jnp.clip(rb * 8 - 4, 0, n_rows - 16)
    start = pl.multiple_of(r0 * GRID_W, 256)
    kw = kl_ref[pl.ds(start, 16 * GRID_W), :]
    vw = vl_ref[pl.ds(start, 16 * GRID_W), :]
    q = q_ref[...]
    first, last_blk = rb == 0, rb == n_rows // 8 - 1
    dy0 = jnp.where(first, NA_KH - 1, jnp.where(last_blk, -1, NA_KH // 2 - 1))
    lane = lax.broadcasted_iota(jnp.int32, (GRID_W, 2 * GRID_W), 1)
    bias_rows = []
    for qa in range(8):
        win0 = jnp.where(first, max(qa - 4, 0), jnp.where(last_blk, min(qa + 4, 8), qa))
        tiles = []
        for kp in range(8):
            a1 = jnp.clip(2 * kp - qa + dy0 + 1, 0, NA_DY + 1)
            ok_even = jnp.logical_and(2 * kp >= win0, 2 * kp < win0 + NA_KH).astype(jnp.int32)
            ok_odd = jnp.logical_and(2 * kp + 1 >= win0, 2 * kp + 1 < win0 + NA_KH).astype(jnp.int32)
            ok = jnp.where(lane < GRID_W, ok_even, ok_odd)
            tiles.append(jnp.where(ok > 0, bias_ref[a1], NEG_BIG))
        bias_rows.append(jnp.concatenate(tiles, axis=1))
    s_lat = _dot_nt(q, kw) * scale + jnp.concatenate(bias_rows, axis=0)
    s_ctx = _dot_nt(q, kc_ref[...]) * scale
    (p_lat, p_ctx), l = _softmax_parts([s_lat, s_ctx])
    inv = 1.0 / l
    o = _dot((p_lat * inv).astype(BF16), vw) + _dot((p_ctx * inv).astype(BF16), vc_ref[...])
    o_ref[...] = o.astype(BF16)


def na_attention(z1, bias, n_samples, n_tok, n_ctx, n_heads):
    n_rows = n_tok // GRID_W
    assert n_rows >= 16 and n_rows % 8 == 0
    nrb = n_rows // 8
    tq = 8 * GRID_W
    d = HEAD_DIM
    cblk = n_samples * n_tok // n_ctx
    return pl.pallas_call(
        functools.partial(_na_attn_kernel, scale=d ** -0.5, n_rows=n_rows),
        grid=(n_samples, n_heads, nrb),
        in_specs=[pl.BlockSpec((tq, d), lambda b, h, rb: (b * nrb + rb, h)),
                  pl.BlockSpec((n_tok, d), lambda b, h, rb: (b, n_heads + h)),
                  pl.BlockSpec((n_tok, d), lambda b, h, rb: (b, 2 * n_heads + h)),
                  pl.BlockSpec((n_ctx, d), lambda b, h, rb: (cblk + b, n_heads + h)),
                  pl.BlockSpec((n_ctx, d), lambda b, h, rb: (cblk + b, 2 * n_heads + h)),
                  pl.BlockSpec((None, NA_DY + 2, GRID_W, 2 * GRID_W), lambda b, h, rb: (h, 0, 0, 0))],
        out_specs=pl.BlockSpec((tq, d), lambda b, h, rb: (b * nrb + rb, h)),
        out_shape=jax.ShapeDtypeStruct((n_samples * n_tok, n_heads * d), BF16),
        compiler_params=_cparams(3, 40), name="na_attn",
    )(z1, z1, z1, z1, z1, bias)


def _ctx_attn_kernel(q_ref, k_ref, v_ref, o_ref, *, scale):
    (p,), l = _softmax_parts([_dot_nt(q_ref[...], k_ref[...]) * scale])
    o_ref[...] = _dot((p * (1.0 / l)).astype(BF16), v_ref[...]).astype(BF16)


def na_ctx_attention(z1, n_samples, n_tok, n_ctx, n_heads):
    d = HEAD_DIM
    cblk = n_samples * n_tok // n_ctx
    return pl.pallas_call(
        functools.partial(_ctx_attn_kernel, scale=d ** -0.5),
        grid=(n_samples, n_heads),
        in_specs=[pl.BlockSpec((n_ctx, d), lambda b, h: (cblk + b, h)),
                  pl.BlockSpec((n_ctx, d), lambda b, h: (cblk + b, n_heads + h)),
                  pl.BlockSpec((n_ctx, d), lambda b, h: (cblk + b, 2 * n_heads + h))],
        out_specs=pl.BlockSpec((n_ctx, d), lambda b, h: (b, h)),
        out_shape=jax.ShapeDtypeStruct((n_samples * n_ctx, n_heads * d), BF16),
        compiler_params=_cparams(2, 16), name="na_attn_ctx",
    )(z1, z1, z1)


def _hgrn_chunk(q_raw, f_raw, v, lb, st_ref, reverse):
    C, HW = q_raw.shape
    dk = HEAD_DIM
    n_heads = HW // dk
    q = _silu(q_raw) * (dk ** -0.5)
    f = lb + (1.0 - lb) * jax.nn.sigmoid(f_raw)
    g = jnp.maximum(jnp.log(f) * LOG2E, -300.0)
    k = 1.0 - f
    t_idx = lax.broadcasted_iota(jnp.int32, (C, 1), 0)
    p = (C - 1 - t_idx) if reverse else t_idx

    def rotate_rows(x, s):
        if s % 8 == 0:
            return jnp.concatenate([x[C - s:], x[:C - s]], axis=0)
        return pltpu.roll(x, s, 0)

    def prev(x, s):
        return rotate_rows(x, (C - s) if reverse else s)

    def nxt(x, s):
        return rotate_rows(x, s if reverse else (C - s))

    def incl_scan(b):
        x = g
        pb = jnp.bitwise_and(p, b - 1)
        s = 1
        while s < b:
            x = x + jnp.where(pb >= s, prev(x, s), 0.0)
            s *= 2
        return x

    def excl_rscan(b):
        if b == 1:
            return jnp.zeros_like(g)
        pb = jnp.bitwise_and(p, b - 1)
        x = jnp.where(pb <= b - 2, nxt(g, 1), 0.0)
        s = 1
        while s < b:
            x = x + jnp.where(pb + s <= b - 1, nxt(x, s), 0.0)
            s *= 2
        return x

    cum = incl_scan(C)
    last = cum[0:1] if reverse else cum[C - 1:C]
    q_in = (q * jnp.exp2(cum)).astype(BF16)
    k_out = (k * jnp.exp2(last - cum)).astype(BF16)
    total = jnp.exp2(last)
    qb, kb, vb = q.astype(BF16), k.astype(BF16), v.astype(BF16)

    def boundary_rows(b):
        rows = []
        for j in range(C // (2 * b)):
            r = 2 * b * j + (b if reverse else b - 1)
            rows.append(jnp.broadcast_to(cum[r:r + 1], (2 * b, HW)))
        return jnp.concatenate(rows, axis=0)

    levels = []
    b = C // 2
    while b >= 1:
        upper = jnp.bitwise_and(p, b) != 0
        if b >= 8:
            x = cum - boundary_rows(b)
            e_up, e_lo = x, -x
        else:
            e_up, e_lo = incl_scan(b), excl_rscan(b)
        ql = jnp.where(upper, q * jnp.exp2(e_up), 0.0).astype(BF16)
        kl = jnp.where(upper, 0.0, k * jnp.exp2(e_lo)).astype(BF16)
        levels.append((b, ql, kl))
        b //= 2

    s_idx = lax.broadcasted_iota(jnp.int32, (1, C), 1)
    ps = (C - 1 - s_idx) if reverse else s_idx
    outs = []
    for h in range(n_heads):
        sl = slice(h * dk, (h + 1) * dk)
        st_old = st_ref[h]
        att = jnp.where(p == ps, _dot_nt(qb[:, sl], kb[:, sl]), 0.0)
        for b, ql, kl in levels:
            same_pair = jnp.bitwise_and(p, -2 * b) == jnp.bitwise_and(ps, -2 * b)
            att = att + jnp.where(same_pair, _dot_nt(ql[:, sl], kl[:, sl]), 0.0)
        outs.append(_dot_nt(q_in[:, sl], st_old.astype(BF16)) + _dot(att.astype(BF16), vb[:, sl]))
        st_ref[h] = total[:, sl] * st_old + _dot_tn(vb[:, sl], k_out[:, sl])
    return jnp.concatenate(outs, axis=1)


def _hgrn_kernel(*refs, n_samples):
    ns = n_samples
    fwd_in, bwd_in = refs[:3 * ns], refs[3 * ns:6 * ns]
    lb_ref, of_ref, ob_ref, sf_ref, sb_ref = refs[6 * ns:]

    @pl.when(pl.program_id(0) == 0)
    def _():
        sf_ref[...] = jnp.zeros_like(sf_ref)
        sb_ref[...] = jnp.zeros_like(sb_ref)

    lb = jnp.concatenate([lb_ref[...]] * ns, axis=1)
    HW = lb_ref.shape[1]
    for ins, o_ref, st_ref, rev in ((fwd_in, of_ref, sf_ref, False), (bwd_in, ob_ref, sb_ref, True)):
        q, f, v = (jnp.concatenate([ins[3 * s + k][...] for s in range(ns)], axis=1) for k in range(3))
        o = _hgrn_chunk(q, f, v, lb, st_ref, reverse=rev)
        for s in range(ns):
            o_ref[s] = o[:, s * HW:(s + 1) * HW]


def hgrn_scan(z2, lb, n_samples, n_tok, n_ctx):
    HW = z2.shape[1] // 5
    C = HGRN_CHUNK
    ncc, ncl = n_ctx // C, n_tok // C
    lat_blocks = n_samples * ncl

    def fwd_row(b, c):
        return jnp.where(c < ncc, lat_blocks + b * ncc + c, b * ncl + (c - ncc))

    def bwd_row(b, c):
        return jnp.where(c < ncc, lat_blocks + b * ncc + (ncc - 1 - c), b * ncl + (ncl - 1 - (c - ncc)))

    spec = lambda rowf, b, col: pl.BlockSpec((C, HW), lambda c: (rowf(b, c), col))
    samples = range(n_samples)
    in_specs = ([spec(fwd_row, b, col) for b in samples for col in (0, 1, 3)]
                + [spec(bwd_row, b, col) for b in samples for col in (0, 2, 3)]
                + [pl.BlockSpec((1, HW), lambda c: (0, 0))])
    n_heads_all = n_samples * HW // HEAD_DIM
    bwd_local = lambda c: jnp.where(c < ncc, ncc - 1 - c, ncc + ncl - 1 - (c - ncc))
    return pl.pallas_call(
        functools.partial(_hgrn_kernel, n_samples=n_samples), grid=(ncc + ncl,),
        in_specs=in_specs,
        out_specs=[pl.BlockSpec((n_samples, C, HW), lambda c: (0, c, 0)),
                   pl.BlockSpec((n_samples, C, HW), lambda c: (0, bwd_local(c), 0))],
        out_shape=[jax.ShapeDtypeStruct((n_samples, n_ctx + n_tok, HW), F32)] * 2,
        scratch_shapes=[pltpu.VMEM((n_heads_all, HEAD_DIM, HEAD_DIM), F32)] * 2,
        compiler_params=_cparams(1, 56), name="hgrn_scan",
    )(*([z2] * (6 * n_samples)), lb.reshape(1, HW))


def _hgrn_readout_kernel(of_ref, ob_ref, gz_ref, g_ref, o_ref):
    for h in range(of_ref.shape[1] // HEAD_DIM):
        sl = slice(h * HEAD_DIM, (h + 1) * HEAD_DIM)
        x = of_ref[:, sl] + ob_ref[:, sl]
        y = x * lax.rsqrt(jnp.mean(x * x, axis=-1, keepdims=True) + RMS_EPS) * g_ref[...]
        o_ref[:, sl] = (y * _silu(gz_ref[:, sl])).astype(BF16)


def hgrn_readout(o_f, o_b, z2, norm_g, n_rows, n_tok, n_ctx):
    n_samples, _, HW = o_f.shape
    tm = math.gcd(n_ctx, ROW_TILE)
    lat_tiles, tiles_per_sample, ctx_tiles = n_samples * n_tok // tm, n_tok // tm, n_ctx // tm

    def o_idx(i):
        is_lat = i < lat_tiles
        jc = i - lat_tiles
        return (jnp.where(is_lat, i // tiles_per_sample, jc // ctx_tiles),
                jnp.where(is_lat, ctx_tiles + i % tiles_per_sample, jc % ctx_tiles), 0)

    return pl.pallas_call(
        _hgrn_readout_kernel, grid=(n_rows // tm,),
        in_specs=[pl.BlockSpec((None, tm, HW), o_idx), pl.BlockSpec((None, tm, HW), o_idx),
                  pl.BlockSpec((tm, HW), lambda i: (i, 4)), pl.BlockSpec((1, HEAD_DIM), lambda i: (0, 0))],
        out_specs=pl.BlockSpec((tm, HW), lambda i: (i, 0)),
        out_shape=jax.ShapeDtypeStruct((n_rows, HW), BF16),
        compiler_params=_cparams(1, 32), name="hgrn_readout",
    )(o_f, o_b, z2, norm_g.reshape(1, HEAD_DIM))


def _merge_kernel(h_ref, oa_ref, od_ref, om_ref, or_ref, wg_ref, bg_ref, wb_ref, o_ref, wg_bf, wb_bf):
    @pl.when(pl.program_id(1) == 0)
    def _():
        wg_bf[...] = wg_ref[...].astype(BF16)
        wb_bf[...] = wb_ref[...].astype(BF16)

    h = h_ref[...]
    acc = None
    for j, o_j in enumerate((oa_ref, od_ref, om_ref, or_ref)):
        gate = jax.nn.sigmoid(_dot(h, wg_bf[j]) + bg_ref[j])
        term = gate * _dot(o_j[...], wb_bf[j])
        acc = term if acc is None else acc + term
    o_ref[...] = acc.astype(BF16)


def merge_branches(h, outs, w_gate, b_gate, w_branch, layer, n_rows):
    D = h.shape[1]
    L, nb, BW = w_branch.shape[:3]
    tm, tn = ROW_TILE, 256
    single = dict(pipeline_mode=pl.Buffered(1))
    return pl.pallas_call(
        _merge_kernel, grid=(D // tn, n_rows // tm),
        in_specs=[pl.BlockSpec((tm, D), lambda j, i: (i, 0))]
                 + [pl.BlockSpec((tm, BW), lambda j, i: (i, 0))] * nb
                 + [pl.BlockSpec((None, nb, D, tn), lambda j, i: (layer, 0, 0, j), **single),
                    pl.BlockSpec((None, nb, 1, tn), lambda j, i: (layer, 0, 0, j)),
                    pl.BlockSpec((None, nb, BW, tn), lambda j, i: (layer, 0, 0, j), **single)],
        out_specs=pl.BlockSpec((tm, tn), lambda j, i: (i, j)),
        out_shape=jax.ShapeDtypeStruct((n_rows, D), BF16),
        scratch_shapes=[pltpu.VMEM((nb, D, tn), BF16), pltpu.VMEM((nb, BW, tn), BF16)],
        compiler_params=_cparams(2, 58), name="merge",
    )(h, *outs, w_gate, b_gate.reshape(L, nb, 1, D), w_branch)


def _ffn_in_kernel(x_ref, w1_ref, w3_ref, o_ref, w1_bf, w3_bf):
    @pl.when(pl.program_id(1) == 0)
    def _():
        w1_bf[...] = w1_ref[...].astype(BF16)
        w3_bf[...] = w3_ref[...].astype(BF16)

    x = x_ref[...]
    o_ref[...] = (_silu(_dot(x, w1_bf[...])) * _dot(x, w3_bf[...])).astype(BF16)


def ffn_in(h, w1, w3, layer, n_rows):
    _, D, Fd = w1.shape
    tm, tn = _even_row_tile(n_rows), 256
    w_spec = pl.BlockSpec((None, D, tn), lambda j, i: (layer, 0, j))
    return pl.pallas_call(
        _ffn_in_kernel,
        grid=(Fd // tn, pl.cdiv(n_rows, tm)),
        in_specs=[pl.BlockSpec((tm, D), lambda j, i: (i, 0)), w_spec, w_spec],
        out_specs=pl.BlockSpec((tm, tn), lambda j, i: (i, j)),
        out_shape=jax.ShapeDtypeStruct((n_rows, Fd), BF16),
        scratch_shapes=[pltpu.VMEM((D, tn), BF16)] * 2,
        compiler_params=_cparams(2, 50), name="ffn_in",
    )(h, w1, w3)


def _group_spec(tm, tn, n_lat, n_samples):
    return pl.BlockSpec((None, 1, tn), lambda j, i: (_grp_of_tile(i, tm, n_lat, n_samples), 0, j))


def ffn_out(u, w2, layer, xres, gate, n_rows, n_lat, n_samples):
    D = w2.shape[2]
    G = gate.shape[0]
    tm, tn = 256, 512
    return _mm_call(u, w2, layer, n_rows=n_rows, tn=tn, tm=tm, out_dtype=F32,
                    epilogue=_residual_epilogue, extras=(xres, gate.reshape(G, 1, D)),
                    extra_specs=(pl.BlockSpec((tm, tn), lambda j, i: (i, j)), _group_spec(tm, tn, n_lat, n_samples)),
                    name="ffn_out")


MOE_TILE = 256


def moe_route_plan(comb, n_experts):
    M = comb.shape[0]
    T = MOE_TILE
    w = comb[:, :n_experts]
    sel = w > 0
    n_tiles = 2 * M // T + n_experts
    R = n_tiles * T
    cnt = jnp.sum(sel, axis=0, dtype=jnp.int32)
    rank = jnp.cumsum(sel, axis=0, dtype=jnp.int32) - 1
    gsz = (cnt + T - 1) // T * T
    gend = jnp.cumsum(gsz)
    off = gend - gsz
    dest_all = off[None, :] + rank
    tile_start = jnp.arange(n_tiles, dtype=jnp.int32) * T
    tile_expert = jnp.minimum(jnp.searchsorted(gend, tile_start, side="right"), n_experts - 1).astype(jnp.int32)
    n_used = (gend[-1] // T).reshape(1).astype(jnp.int32)
    k_next = jnp.searchsorted(tile_expert, tile_expert, side="right")
    next_expert = jnp.where(k_next < n_tiles, tile_expert[jnp.minimum(k_next, n_tiles - 1)], -1).astype(jnp.int32)
    e_lo = jnp.argmax(sel, axis=1)
    e_hi = n_experts - 1 - jnp.argmax(sel[:, ::-1], axis=1)
    take = lambda a, e: jnp.take_along_axis(a, e[:, None], axis=1)[:, 0]
    two = e_hi != e_lo
    dest = jnp.stack([take(dest_all, e_lo), take(dest_all, e_hi)], axis=1).astype(jnp.int32)
    tok = lax.broadcasted_iota(jnp.int32, dest.shape, 0)
    src_tok = jnp.zeros((R,), jnp.int32).at[dest.reshape(-1)].set(tok.reshape(-1))
    pw = jnp.zeros((M, 128), F32).at[:, 0].set(take(w, e_lo)).at[:, 1].set(jnp.where(two, take(w, e_hi), 0.0))
    return src_tok, (tile_expert, next_expert, n_used), dest, pw


def _row_copy(src_hbm, row, dst_buf, slot, r, sem):
    return pltpu.make_async_copy(src_hbm.at[pl.ds(row, 1)], dst_buf.at[slot, pl.ds(r, 1)], sem.at[slot])


def _gather_tile(idx_of_row, src_hbm, buf, sem, n_rows_tile):
    i = pl.program_id(0)
    n = pl.num_programs(0)

    def start_tile(t, slot):
        def body(r2, carry):
            for pri in range(2):
                r = 2 * r2 + pri
                _row_copy(src_hbm, idx_of_row(t * n_rows_tile + r), buf, slot, r, sem).start(priority=pri)
            return carry
        lax.fori_loop(0, n_rows_tile // 2, body, 0, unroll=4)

    @pl.when(i == 0)
    def _():
        start_tile(0, 0)

    @pl.when(i + 1 < n)
    def _():
        start_tile(i + 1, (i + 1) % 2)

    slot = i % 2

    def wait_body(r, carry):
        _row_copy(src_hbm, 0, buf, slot, r, sem).wait()
        return carry
    lax.fori_loop(0, n_rows_tile, wait_body, 0, unroll=8)
    return slot


def _moe_gather_kernel(src_ref, h_hbm, o_ref, buf, sem):
    slot = _gather_tile(lambda r: src_ref[r], h_hbm, buf, sem, o_ref.shape[0])
    o_ref[...] = buf[slot].astype(BF16)


def moe_gather(hf, src_tok):
    R = src_tok.shape[0]
    D = hf.shape[1]
    T = MOE_TILE
    return pl.pallas_call(
        _moe_gather_kernel,
        grid_spec=pltpu.PrefetchScalarGridSpec(
            num_scalar_prefetch=1, grid=(R // T,),
            in_specs=[pl.BlockSpec(memory_space=pl.ANY)],
            out_specs=pl.BlockSpec((T, D), lambda i, src: (i, 0)),
            scratch_shapes=[pltpu.VMEM((2, T, D), F32), pltpu.SemaphoreType.DMA((2,))]),
        out_shape=jax.ShapeDtypeStruct((R, D), BF16),
        compiler_params=_cparams(1, 24), name="moe_gather",
    )(src_tok, hf)


def _moe_mm_kernel(te_ref, nx_ref, nu_ref, x_ref, *rest, e0, n_w):
    w_hbms = rest[:n_w]
    o_ref = rest[n_w]
    stages, wbfs, sem = rest[n_w + 1:2 * n_w + 1], rest[2 * n_w + 1:3 * n_w + 1], rest[3 * n_w + 1]
    j, i, nj = pl.program_id(0), pl.program_id(1), pl.num_programs(0)
    tn = stages[0].shape[1]

    def block_copies(e, jj):
        cols = pl.ds(pl.multiple_of(jj * tn, tn), tn)
        return [pltpu.make_async_copy(w.at[e0 + e, :, cols], st, sem.at[k])
                for k, (w, st) in enumerate(zip(w_hbms, stages))]

    fresh = jnp.logical_or(i == 0, te_ref[i] != te_ref[jnp.maximum(i - 1, 0)])

    @pl.when(fresh)
    def _():
        @pl.when(jnp.logical_and(i == 0, j == 0))
        def _():
            for cp in block_copies(te_ref[0], 0):
                cp.start()

        for cp in block_copies(te_ref[i], j):
            cp.wait()
        for st, wbf in zip(stages, wbfs):
            _cast_rows(st, wbf)
        nxt = nx_ref[i]

        @pl.when(nxt >= 0)
        def _():
            for cp in block_copies(nxt, j):
                cp.start()

        @pl.when(jnp.logical_and(nxt < 0, j + 1 < nj))
        def _():
            for cp in block_copies(te_ref[0], j + 1):
                cp.start()

    @pl.when(i < nu_ref[0])
    def _():
        x = x_ref[...]
        if n_w == 2:
            o_ref[...] = (_silu(_dot(x, wbfs[0][...])) * _dot(x, wbfs[1][...])).astype(o_ref.dtype)
        else:
            o_ref[...] = _dot(x, wbfs[0][...]).astype(o_ref.dtype)

    @pl.when(i >= nu_ref[0])
    def _():
        o_ref[...] = jnp.zeros_like(o_ref)


def _moe_mm(x, ws, e0, plan, tn, out_dtype, name):
    tile_expert, next_expert, n_used = plan
    R, K = x.shape
    N = ws[0].shape[2]
    T = MOE_TILE
    n_w = len(ws)
    return pl.pallas_call(
        functools.partial(_moe_mm_kernel, e0=e0, n_w=n_w),
        grid_spec=pltpu.PrefetchScalarGridSpec(
            num_scalar_prefetch=3, grid=(N // tn, R // T),
            in_specs=[pl.BlockSpec((T, K), lambda j, i, te, nx, nu: (i, 0))]
                     + [pl.BlockSpec(memory_space=pl.ANY)] * n_w,
            out_specs=pl.BlockSpec((T, tn), lambda j, i, te, nx, nu: (i, j)),
            scratch_shapes=[pltpu.VMEM((K, tn), F32)] * n_w + [pltpu.VMEM((K, tn), BF16)] * n_w
                           + [pltpu.SemaphoreType.DMA((n_w,))]),
        out_shape=jax.ShapeDtypeStruct((R, N), out_dtype),
        compiler_params=_cparams(2, 48), name=name,
    )(tile_expert, next_expert, n_used, x, *ws)


def moe_ffn_in(xs, w1, w3, e0, plan):
    return _moe_mm(xs, (w1, w3), e0, plan, 512, BF16, "moe_ffn_in")


def moe_ffn_out(u, w2, e0, plan):
    return _moe_mm(u, (w2,), e0, plan, 2048, F32, "moe_ffn_out")


def _moe_combine_kernel(dest_ref, y_hbm, pw_ref, x_ref, g_ref, *rest, with_norm):
    if with_norm:
        ng_ref, o_ref, buf, sem = rest
    else:
        o_ref, buf, sem = rest
    T = o_ref.shape[0]
    slot = _gather_tile(lambda r: dest_ref[r], y_hbm, buf, sem, 2 * T)
    pw = pw_ref[...]
    mix = pw[:, 0:1] * buf[slot, 0:T, :] + pw[:, 1:2] * buf[slot, T:2 * T, :]
    x = x_ref[...] + g_ref[...] * mix
    if with_norm:
        x = x * lax.rsqrt(jnp.mean(x * x, axis=-1, keepdims=True) + RMS_EPS) * ng_ref[...]
    o_ref[...] = x


def moe_combine(y, dest, pw, xres, gate, n_lat, n_samples, norm_g=None):
    M, D = xres.shape
    G = gate.shape[0]
    T = 128
    dest = dest.reshape(M // T, T, 2).transpose(0, 2, 1)
    in_specs = [pl.BlockSpec(memory_space=pl.ANY),
                pl.BlockSpec((T, 128), lambda i, d: (i, 0)),
                pl.BlockSpec((T, D), lambda i, d: (i, 0)),
                pl.BlockSpec((None, 1, D), lambda i, d: (_grp_of_tile(i, T, n_lat, n_samples), 0, 0))]
    args = [dest.reshape(-1), y, pw, xres, gate.reshape(G, 1, D)]
    if norm_g is not None:
        in_specs.append(pl.BlockSpec((1, D), lambda i, d: (0, 0)))
        args.append(norm_g.reshape(1, D))
    return pl.pallas_call(
        functools.partial(_moe_combine_kernel, with_norm=norm_g is not None),
        grid_spec=pltpu.PrefetchScalarGridSpec(
            num_scalar_prefetch=1, grid=(M // T,),
            in_specs=in_specs,
            out_specs=pl.BlockSpec((T, D), lambda i, d: (i, 0)),
            scratch_shapes=[pltpu.VMEM((2, 2 * T, D), F32), pltpu.SemaphoreType.DMA((2,))]),
        out_shape=jax.ShapeDtypeStruct((M, D), F32),
        compiler_params=_cparams(1, 32), name="moe_combine",
    )(*args)


def kernel(x, c, ctx, c_ctx, norm1_g, norm2_g, w_ada, b_ada, w_in, na_rpb, diff_lambda, diff_subln_g,
           mla_q_norm_g, mla_w_q_up, mla_kv_norm_g, mla_w_kv_up, hgrn_lower_bounds, hgrn_norm_g,
           w_branch, w_gate, b_gate, w_out, ffn_w1, ffn_w3, ffn_w2, moe_router, moe_w1, moe_w3, moe_w2,
           final_norm_g):
    B, N, D = x.shape
    NC = ctx.shape[1]
    L = w_ada.shape[0]
    n_lat, n_all = B * N, B * N + B * NC
    BW = w_branch.shape[2]
    n_heads = BW // HEAD_DIM
    HG0 = w_in.shape[2] - 5 * BW
    assert N % ROW_TILE == 0 and (B * NC) % ROW_TILE == 0 and NC % HGRN_CHUNK == 0

    xs = jnp.concatenate([x.reshape(n_lat, D), ctx.reshape(B * NC, D)], axis=0)

    cond8 = jnp.zeros((8, D), F32).at[:B].set(c).at[B].set(c_ctx)
    mods = adaln_all(cond8, w_ada, b_ada)[:, :B + 1].reshape(L, B + 1, 6, D)

    lb_all = jnp.cumsum(jax.nn.softmax(hgrn_lower_bounds.astype(F32), axis=0), axis=0)
    lb_all = lb_all - lb_all[0:1]

    w_in_t = jnp.swapaxes(w_in, 1, 2)

    cosd, sind = _rope_tables(N, B * NC, 2 * 64, ROW_TILE)
    cosm, sinm = _rope_tables(N, B * NC, 64, ROW_TILE)

    RQ, RKV = mla_w_q_up.shape[1], mla_w_kv_up.shape[1]
    wq4 = mla_w_q_up.reshape(L, RQ, n_heads, HEAD_DIM + MLA_ROPE)
    wq_cat = jnp.pad(wq4, ((0, 0), (0, 0), (0, 0), (0, MLA_SLOT - HEAD_DIM - MLA_ROPE))).reshape(L, RQ, -1)
    wkv4 = mla_w_kv_up.reshape(L, RKV, n_heads, 2 * HEAD_DIM)
    wkn = wkv4[..., :HEAD_DIM].reshape(L, RKV, -1)
    wvt = jnp.swapaxes(wkv4[..., HEAD_DIM:].reshape(L, RKV, -1), 1, 2)

    for l in range(L):
        need_ctx = l < L - 1
        n_rows = n_all if need_ctx else n_lat
        lambda_init = 0.8 - 0.6 * math.exp(-0.3 * l)
        sh1, sc1, g1, sh2, sc2, g2 = (mods[l, :, k] for k in range(6))

        h = norm_mod(xs, norm1_g[l], sh1, sc1, N, B)
        z1 = wt_stream_matmul(h, w_in_t, l, 0, 15 * 512, BF16, "w_in_attn")
        z2 = wt_stream_matmul(h, w_in_t, l, HG0, 5 * BW, F32, "w_in_hgrn")

        bias = _na_bias_tables(na_rpb[l])
        o_a = na_attention(z1, bias, B, N, NC, n_heads)
        dq, dk, dvt = diff_rope(z1, cosd, sind, N, n_lat)
        lam_rows = jnp.zeros((8, HEAD_DIM), F32).at[:4].set(diff_lambda[l]).at[4].set(lambda_init)
        o_d = diff_attention(dq, dk, dvt, lam_rows, diff_subln_g[l], lambda_init, B, N, NC, False)
        qcat, kcat, mvt = mla_prep(z1, cosm, sinm, mla_q_norm_g[l], wq_cat[l], mla_kv_norm_g[l], wkn[l], wvt[l],
                                   N, n_lat)
        o_m = mla_attention(qcat, kcat, mvt, B, N, NC, False)
        o_f, o_b = hgrn_scan(z2, lb_all[l], B, N, NC)
        o_r = hgrn_readout(o_f, o_b, z2, hgrn_norm_g[l], n_rows, N, NC)
        if need_ctx:
            o_a = jnp.concatenate([o_a, na_ctx_attention(z1, B, N, NC, n_heads)], axis=0)
            o_d = jnp.concatenate([o_d, diff_attention(dq, dk, dvt, lam_rows, diff_subln_g[l], lambda_init,
                                                       B, N, NC, True)], axis=0)
            o_m = jnp.concatenate([o_m, mla_attention(qcat, kcat, mvt, B, N, NC, True)], axis=0)

        s = merge_branches(h, (o_a, o_d, o_m, o_r), w_gate, b_gate, w_branch, l, n_rows)
        G = B + 1
        tmo = 2 * ROW_TILE
        xs = _mm_call(s, w_out, l, n_rows=n_rows, tn=512, tm=tmo, out_dtype=F32,
                      epilogue=_residual_epilogue, extras=(xs, g1.reshape(G, 1, D)),
                      extra_specs=(pl.BlockSpec((tmo, 512), lambda j, i: (i, j)), _group_spec(tmo, 512, N, B)),
                      name="w_out")

        j = l // 2
        if l % 2 == 0:
            h2 = norm_mod(xs, norm2_g[l], sh2, sc2, N, B)
            u = ffn_in(h2, ffn_w1, ffn_w3, j, n_rows)
            xs = ffn_out(u, ffn_w2, j, xs, g2, n_rows, N, B)
        else:
            hf, comb = norm_mod(xs, norm2_g[l], sh2, sc2, N, B, w_router=moe_router[j])
            NM, E, _, FE = moe_w1.shape
            src_tok, plan, dest, pw = moe_route_plan(comb, E)
            xg = moe_gather(hf, src_tok)
            u = moe_ffn_in(xg, moe_w1.reshape(NM * E, D, FE), moe_w3.reshape(NM * E, D, FE), j * E, plan)
            y = moe_ffn_out(u, moe_w2.reshape(NM * E, FE, D), j * E, plan)
            xs = moe_combine(y, dest, pw, xs, g2, N, B, norm_g=final_norm_g if l == L - 1 else None)

    if L % 2 == 1:
        xs = final_norm(xs[:n_lat], final_norm_g)
    return xs.reshape(B, N, D)
```

```python
import functools
import math

import numpy as np
import jax
import jax.numpy as jnp
from jax import lax
from jax.experimental import pallas as pl
from jax.experimental.pallas import tpu as pltpu

F32 = jnp.float32
BF16 = jnp.bfloat16

GRID_W = 64
RMS_EPS = 1e-6
ROPE_BASE = 10000.0
NA_KH, NA_KW = 8, 16
HEAD_DIM = 128
MLA_ROPE = 64
HGRN_CHUNK = 64
NEG_BIG = -1e30

VMEM_LIMIT_V7X = 60000 * 1024
ROW_TILE = 512


def _even_row_tile(n_rows, cap=1152):
    return next(t for t in range(cap, 15, -16) if n_rows % t == 0)


def _cparams(n_axes, vmem_mb):
    return pltpu.CompilerParams(dimension_semantics=("arbitrary",) * n_axes,
                                vmem_limit_bytes=min(int(vmem_mb * 2**20), VMEM_LIMIT_V7X))


def _silu(x):
    return x * jax.nn.sigmoid(x)


def _dot(a, b):
    return jnp.dot(a, b, preferred_element_type=F32)


def _dot_nt(a, b):
    return lax.dot_general(a, b, (((1,), (1,)), ((), ())), preferred_element_type=F32)


def _dot_tn(a, b):
    return lax.dot_general(a, b, (((0,), (0,)), ((), ())), preferred_element_type=F32)


def _adaln_kernel(c_ref, w_ref, b_ref, o_ref):
    @pl.when(pl.program_id(1) == 0)
    def _():
        o_ref[...] = jnp.broadcast_to(b_ref[...], o_ref.shape)

    x = _silu(c_ref[...]).astype(BF16)
    o_ref[...] += _dot(x, w_ref[...].astype(BF16))


def adaln_all(cond8, w_ada, b_ada):
    L, D, N6 = w_ada.shape
    tk = 128
    cond_k = cond8.reshape(8, D // tk, tk).transpose(1, 0, 2)
    return pl.pallas_call(
        _adaln_kernel,
        grid=(L, D // tk),
        in_specs=[pl.BlockSpec((None, 8, tk), lambda l, k: (k, 0, 0)),
                  pl.BlockSpec((None, tk, N6), lambda l, k: (l, k, 0)),
                  pl.BlockSpec((None, 1, N6), lambda l, k: (l, 0, 0))],
        out_specs=pl.BlockSpec((None, 8, N6), lambda l, k: (l, 0, 0)),
        out_shape=jax.ShapeDtypeStruct((L, 8, N6), F32),
        compiler_params=_cparams(2, 48),
        name="adaln",
    )(cond_k, w_ada, b_ada.reshape(L, 1, N6))


def _norm_mod_kernel(x_ref, g_ref, sh_ref, sc_ref, *rest, n_experts):
    with_router = n_experts is not None
    x = x_ref[...]
    y = x * lax.rsqrt(jnp.mean(x * x, axis=-1, keepdims=True) + RMS_EPS) * g_ref[...]
    h = y * (1.0 + sc_ref[...]) + sh_ref[...]
    if not with_router:
        (h_ref,) = rest
        h_ref[...] = h.astype(BF16)
        return
    wr_ref, h_ref, comb_ref = rest
    h_ref[...] = h
    logits = jnp.dot(h, wr_ref[...], precision=lax.Precision.HIGHEST, preferred_element_type=F32)
    lane = lax.broadcasted_iota(jnp.int32, logits.shape, 1).astype(F32)
    logits = jnp.where(lane < n_experts, logits, -jnp.inf)
    m1 = jnp.max(logits, axis=-1, keepdims=True)
    i1 = jnp.min(jnp.where(logits == m1, lane, 128.0), axis=-1, keepdims=True)
    rest_l = jnp.where(lane == i1, -jnp.inf, logits)
    m2 = jnp.max(rest_l, axis=-1, keepdims=True)
    i2 = jnp.min(jnp.where(rest_l == m2, lane, 128.0), axis=-1, keepdims=True)
    e2 = jnp.exp(m2 - m1)
    p1 = 1.0 / (1.0 + e2)
    p2 = e2 / (1.0 + e2)
    comb_ref[...] = jnp.where(lane == i1, p1, 0.0) + jnp.where(lane == i2, p2, 0.0)


def _grp_of_tile(i, tm, lat_rows_per_sample, n_samples):
    return jnp.minimum((i * tm) // lat_rows_per_sample, n_samples)


def norm_mod(x, g, shift, scale, n_lat, n_samples, w_router=None):
    M, D = x.shape
    tm = 256
    G = shift.shape[0]
    grp = lambda i: (_grp_of_tile(i, tm, n_lat, n_samples), 0, 0)
    in_specs = [pl.BlockSpec((tm, D), lambda i: (i, 0)),
                pl.BlockSpec((1, D), lambda i: (0, 0)),
                pl.BlockSpec((None, 1, D), grp),
                pl.BlockSpec((None, 1, D), grp)]
    args = [x, g.reshape(1, D), shift.reshape(G, 1, D), scale.reshape(G, 1, D)]
    out_specs = [pl.BlockSpec((tm, D), lambda i: (i, 0))]
    out_shape = [jax.ShapeDtypeStruct((M, D), BF16 if w_router is None else F32)]
    if w_router is not None:
        wr = jnp.zeros((D, 128), F32).at[:, :w_router.shape[1]].set(w_router)
        in_specs.append(pl.BlockSpec((D, 128), lambda i: (0, 0)))
        args.append(wr)
        out_specs.append(pl.BlockSpec((tm, 128), lambda i: (i, 0)))
        out_shape.append(jax.ShapeDtypeStruct((M, 128), F32))
    outs = pl.pallas_call(
        functools.partial(_norm_mod_kernel, n_experts=None if w_router is None else w_router.shape[1]),
        grid=(M // tm,),
        in_specs=in_specs, out_specs=out_specs, out_shape=out_shape,
        compiler_params=_cparams(1, 40),
        name="norm_mod_router" if w_router is not None else "norm_mod",
    )(*args)
    return outs if w_router is not None else outs[0]


def _final_norm_kernel(x_ref, g_ref, o_ref):
    x = x_ref[...]
    o_ref[...] = x * lax.rsqrt(jnp.mean(x * x, axis=-1, keepdims=True) + RMS_EPS) * g_ref[...]


def final_norm(x, g):
    M, D = x.shape
    tm = 256
    return pl.pallas_call(
        _final_norm_kernel, grid=(M // tm,),
        in_specs=[pl.BlockSpec((tm, D), lambda i: (i, 0)), pl.BlockSpec((1, D), lambda i: (0, 0))],
        out_specs=pl.BlockSpec((tm, D), lambda i: (i, 0)),
        out_shape=jax.ShapeDtypeStruct((M, D), F32),
        compiler_params=_cparams(1, 40), name="final_norm",
    )(x, g.reshape(1, D))


def _cast_rows(src_ref, dst_ref, rows=512):
    n = src_ref.shape[0]
    for r0 in range(0, n, rows):
        r1 = min(r0 + rows, n)
        dst_ref[r0:r1, :] = src_ref[r0:r1, :].astype(BF16)


def _stream_mm_kernel(x_ref, w_hbm, *rest, layer, n_extra, epilogue):
    extras = rest[:n_extra]
    o_ref, stage, wbf, sem = rest[n_extra:]
    j, i, nj = pl.program_id(0), pl.program_id(1), pl.num_programs(0)
    tn = stage.shape[1]

    def block_copy(jj):
        return pltpu.make_async_copy(w_hbm.at[layer, :, pl.ds(pl.multiple_of(jj * tn, tn), tn)], stage, sem)

    @pl.when(i == 0)
    def _():
        @pl.when(j == 0)
        def _():
            block_copy(0).start()

        block_copy(j).wait()
        _cast_rows(stage, wbf)

        @pl.when(j + 1 < nj)
        def _():
            block_copy(j + 1).start()

    acc = _dot(x_ref[...], wbf[...])
    if epilogue is not None:
        acc = epilogue(acc, *[e[...] for e in extras])
    o_ref[...] = acc.astype(o_ref.dtype)


def _mm_call(x, w, layer, *, n_rows, tn, tm, out_dtype, epilogue=None, extras=(), extra_specs=(), name="mm"):
    K, N = w.shape[1:]
    in_specs = [pl.BlockSpec((tm, K), lambda j, i: (i, 0)), pl.BlockSpec(memory_space=pl.ANY)]
    in_specs += list(extra_specs)
    out_bytes = jnp.dtype(out_dtype).itemsize
    vmem = (2 * tm * K * 2 + K * tn * 4 + K * tn * 2 + 2 * tm * tn * out_bytes + 3 * tm * tn * 4
            + len(extras) * 2 * tm * tn * 4) / 2**20 + 6
    return pl.pallas_call(
        functools.partial(_stream_mm_kernel, layer=layer, n_extra=len(extras), epilogue=epilogue),
        grid=(N // tn, pl.cdiv(n_rows, tm)),
        in_specs=in_specs,
        out_specs=pl.BlockSpec((tm, tn), lambda j, i: (i, j)),
        out_shape=jax.ShapeDtypeStruct((n_rows, N), out_dtype),
        scratch_shapes=[pltpu.VMEM((K, tn), F32), pltpu.VMEM((K, tn), BF16), pltpu.SemaphoreType.DMA(())],
        compiler_params=_cparams(2, vmem),
        name=name,
    )(x, w, *extras)


def _residual_epilogue(acc, xres, gate):
    return xres + gate * acc


def _wt_stream_mm_kernel(x_ref, wt_hbm, o_ref, stage, wbf, sem, *, layer, row0):
    j, i, nj = pl.program_id(0), pl.program_id(1), pl.num_programs(0)
    tn, K = stage.shape

    def block_copy(jj):
        return pltpu.make_async_copy(wt_hbm.at[layer, pl.ds(row0 + jj * tn, tn), :], stage, sem)

    @pl.when(i == 0)
    def _():
        @pl.when(j == 0)
        def _():
            block_copy(0).start()

        block_copy(j).wait()
        for c in range(0, K, tn):
            wbf[c:c + tn, :] = jnp.transpose(stage[:, c:c + tn]).astype(BF16)

        @pl.when(j + 1 < nj)
        def _():
            block_copy(j + 1).start()

    o_ref[...] = _dot(x_ref[...], wbf[...]).astype(o_ref.dtype)


def wt_stream_matmul(x, wt, layer, row0, n_cols, out_dtype, name):
    M, K = x.shape
    tm, tn = _even_row_tile(M), 512
    assert n_cols % tn == 0 and K % tn == 0 and row0 % 8 == 0
    return pl.pallas_call(
        functools.partial(_wt_stream_mm_kernel, layer=layer, row0=row0),
        grid=(n_cols // tn, pl.cdiv(M, tm)),
        in_specs=[pl.BlockSpec((tm, K), lambda j, i: (i, 0)), pl.BlockSpec(memory_space=pl.ANY)],
        out_specs=pl.BlockSpec((tm, tn), lambda j, i: (i, j)),
        out_shape=jax.ShapeDtypeStruct((M, n_cols), out_dtype),
        scratch_shapes=[pltpu.VMEM((tn, K), F32), pltpu.VMEM((K, tn), BF16), pltpu.SemaphoreType.DMA(())],
        compiler_params=_cparams(2, 48), name=name,
    )(x, wt)


def _rope_tables(n_tokens, n_ident, group, tile_rows):
    half = group // 2
    q = half // 2
    pos = np.arange(n_tokens)
    row, col = pos // GRID_W, pos % GRID_W
    freqs = ROPE_BASE ** (-(np.arange(q, dtype=np.float64) / q))
    lane = np.arange(128)
    in_group = lane % group
    axis_pos = np.where((in_group < half)[None, :], row[:, None], col[:, None]).astype(np.float64)
    ang = axis_pos * freqs[(in_group % half) % q][None, :]
    sign = np.where((in_group % half) < q, -1.0, 1.0)[None, :]
    cos, sin = np.cos(ang), np.sin(ang) * sign
    n_pad = -(-n_ident // tile_rows) * tile_rows
    cos = np.concatenate([cos, np.ones((n_pad, 128))], axis=0)
    sin = np.concatenate([sin, np.zeros((n_pad, 128))], axis=0)
    return jnp.asarray(cos, F32), jnp.asarray(sin, F32)


def _rope_apply(x, cos, sin, q):
    lane = lax.broadcasted_iota(jnp.int32, x.shape, 1)
    partner = jnp.where((lane % (2 * q)) < q, pltpu.roll(x, 128 - q, 1), pltpu.roll(x, q, 1))
    return x * cos + partner * sin


LOG2E = math.log2(math.e)


def _diff_rope_kernel(q_ref, k_ref, v_ref, cos_ref, sin_ref, qo_ref, ko_ref, vt_ref, *, q_scale):
    vt_ref[...] = jnp.transpose(v_ref[...].astype(F32)).astype(BF16)
    cos, sin = cos_ref[...], sin_ref[...]
    for src, dst, mult in ((q_ref, qo_ref, q_scale), (k_ref, ko_ref, None)):
        for c in range(src.shape[1] // 128):
            sl = slice(c * 128, (c + 1) * 128)
            y = _rope_apply(src[:, sl].astype(F32), cos, sin, 32)
            dst[:, sl] = (y if mult is None else y * mult).astype(BF16)


def _table_block(i, tm, n_tok, n_lat_total):
    return jnp.where(i * tm < n_lat_total, (i * tm % n_tok) // tm, n_tok // tm)


def diff_rope(z1, cosd, sind, n_tok, n_lat_total, W):
    M = z1.shape[0]
    tm = ROW_TILE
    tb = lambda i: (_table_block(i, tm, n_tok, n_lat_total), 0)
    return pl.pallas_call(
        functools.partial(_diff_rope_kernel, q_scale=HEAD_DIM ** -0.5 * LOG2E), grid=(M // tm,),
        in_specs=[pl.BlockSpec((tm, W), lambda i: (i, 3)), pl.BlockSpec((tm, W), lambda i: (i, 4)),
                  pl.BlockSpec((tm, W), lambda i: (i, 5)),
                  pl.BlockSpec((tm, 128), tb), pl.BlockSpec((tm, 128), tb)],
        out_specs=[pl.BlockSpec((tm, W), lambda i: (i, 0))] * 2 + [pl.BlockSpec((W, tm), lambda i: (0, i))],
        out_shape=[jax.ShapeDtypeStruct((M, W), BF16)] * 2 + [jax.ShapeDtypeStruct((W, M), BF16)],
        compiler_params=_cparams(1, 40), name="diff_rope",
    )(z1, z1, z1, cosd, sind)


MLA_SLOT = 2 * HEAD_DIM


def _mla_prep_kernel(cq_ref, ckva_ref, ckvb_ref, kr_ref, cos_ref, sin_ref, gq_ref, gkv_ref, wq_ref, wkn_ref,
                     wvt_ref, qcat_ref, kcat_ref, vt_ref, wq_bf, wkn_bf, wvt_bf, *, q_scale):
    @pl.when(pl.program_id(0) == 0)
    def _():
        wq_bf[...] = wq_ref[...].astype(BF16)
        wkn_bf[...] = wkn_ref[...].astype(BF16)
        wvt_bf[...] = wvt_ref[...].astype(BF16)

    cos, sin = cos_ref[...], sin_ref[...]
    d = HEAD_DIM
    n_heads = qcat_ref.shape[1] // MLA_SLOT
    cq = cq_ref[...].astype(F32)
    cqn = (cq * lax.rsqrt(jnp.mean(cq * cq, axis=-1, keepdims=True) + RMS_EPS) * gq_ref[...]).astype(BF16)
    q = _dot(cqn, wq_bf[...])
    a = ckva_ref[...].astype(F32)
    b = ckvb_ref[...].astype(F32)
    ha = a.shape[1]
    ms = (jnp.sum(a * a, axis=-1, keepdims=True) + jnp.sum(b * b, axis=-1, keepdims=True)) / (2 * ha)
    r = lax.rsqrt(ms + RMS_EPS)
    g = gkv_ref[...]
    ckvn = jnp.concatenate([(a * r * g[:, :ha]).astype(BF16), (b * r * g[:, ha:]).astype(BF16)], axis=1)
    kn = _dot(ckvn, wkn_bf[...])
    lane = lax.broadcasted_iota(jnp.int32, cos.shape, 1)
    kr = jnp.where(lane < MLA_ROPE, _rope_apply(kr_ref[...].astype(F32), cos, sin, 16), 0.0).astype(BF16)
    for h in range(n_heads):
        c0 = h * MLA_SLOT
        qcat_ref[:, c0:c0 + d] = (q[:, c0:c0 + d] * q_scale).astype(BF16)
        qcat_ref[:, c0 + d:c0 + 2 * d] = (_rope_apply(q[:, c0 + d:c0 + 2 * d], cos, sin, 16) * q_scale).astype(BF16)
        kcat_ref[:, c0:c0 + d] = kn[:, h * d:(h + 1) * d].astype(BF16)
        kcat_ref[:, c0 + d:c0 + 2 * d] = kr
    vt_ref[...] = _dot_nt(wvt_bf[...], ckvn).astype(BF16)


def mla_prep(z1, cosm, sinm, gq, wq_cat, gkv, wkn, wvt, n_tok, n_lat_total, col0):
    M = z1.shape[0]
    tm = ROW_TILE
    RQ, NQ = wq_cat.shape
    RKV, NKN = wkn.shape
    c_kv, c_kr = col0 + RQ, col0 + RQ + RKV
    assert col0 % RQ == 0 and c_kv % (RKV // 2) == 0 and c_kr % 128 == 0
    tb = lambda i: (_table_block(i, tm, n_tok, n_lat_total), 0)
    const = lambda i: (0, 0)
    return pl.pallas_call(
        functools.partial(_mla_prep_kernel, q_scale=(HEAD_DIM + MLA_ROPE) ** -0.5 * LOG2E), grid=(M // tm,),
        in_specs=[pl.BlockSpec((tm, RQ), lambda i: (i, col0 // RQ)),
                  pl.BlockSpec((tm, RKV // 2), lambda i: (i, c_kv // (RKV // 2))),
                  pl.BlockSpec((tm, RKV // 2), lambda i: (i, c_kv // (RKV // 2) + 1)),
                  pl.BlockSpec((tm, 128), lambda i: (i, c_kr // 128)),
                  pl.BlockSpec((tm, 128), tb), pl.BlockSpec((tm, 128), tb),
                  pl.BlockSpec((1, RQ), const), pl.BlockSpec((1, RKV), const),
                  pl.BlockSpec((RQ, NQ), const), pl.BlockSpec((RKV, NKN), const), pl.BlockSpec((NKN, RKV), const)],
        out_specs=[pl.BlockSpec((tm, NQ), lambda i: (i, 0)),
                   pl.BlockSpec((tm, NQ), lambda i: (i, 0)),
                   pl.BlockSpec((NKN, tm), lambda i: (0, i))],
        out_shape=[jax.ShapeDtypeStruct((M, NQ), BF16), jax.ShapeDtypeStruct((M, NQ), BF16),
                   jax.ShapeDtypeStruct((NKN, M), BF16)],
        scratch_shapes=[pltpu.VMEM((RQ, NQ), BF16), pltpu.VMEM((RKV, NKN), BF16), pltpu.VMEM((NKN, RKV), BF16)],
        compiler_params=_cparams(1, 56), name="mla_prep",
    )(z1, z1, z1, z1, cosm, sinm, gq.reshape(1, RQ), gkv.reshape(1, RKV), wq_cat, wkn, wvt)


def _softmax_parts(s_list):
    m = functools.reduce(jnp.maximum, [jnp.max(s, axis=-1, keepdims=True) for s in s_list])
    p_list = [jnp.exp(s - m) for s in s_list]
    l = functools.reduce(lambda a, b: a + b, [jnp.sum(p, axis=-1, keepdims=True) for p in p_list])
    return p_list, l


def _softmax_parts_t(st_list):
    m = functools.reduce(jnp.maximum, [jnp.max(s, axis=0, keepdims=True) for s in st_list])
    p_list = [jnp.exp2(s - m) for s in st_list]
    l = functools.reduce(lambda a, b: a + b, [jnp.sum(p, axis=0, keepdims=True) for p in p_list])
    return p_list, l


def _query_rows(n_samples, n_tok, n_ctx, tq, ctx_queries):
    if ctx_queries:
        nq = n_ctx // tq
        first = n_samples * n_tok // tq
        return nq, (lambda b, t: first + b * nq + t), n_samples * n_ctx, (lambda b, t: b * nq + t)
    nq = n_tok // tq
    qrow = lambda b, t: b * nq + t
    return nq, qrow, n_samples * n_tok, qrow


def _mla_attn_kernel(q_ref, *rest, with_latent):
    if with_latent:
        kl_ref, vtl_ref, kc_ref, vtc_ref, o_ref = rest
    else:
        kc_ref, vtc_ref, o_ref = rest
    for hh in range(q_ref.shape[1] // MLA_SLOT):
        ds = slice(hh * MLA_SLOT, (hh + 1) * MLA_SLOT)
        dv = slice(hh * HEAD_DIM, (hh + 1) * HEAD_DIM)
        q = q_ref[:, ds]
        k_list = [kc_ref[:, ds]] + ([kl_ref[:, ds]] if with_latent else [])
        vt_list = [vtc_ref[dv, :]] + ([vtl_ref[dv, :]] if with_latent else [])
        p_list, l = _softmax_parts_t([_dot_nt(k, q) for k in k_list])
        ot = functools.reduce(lambda a, b: a + b, [_dot(vt, p.astype(BF16)) for vt, p in zip(vt_list, p_list)])
        o_ref[:, dv] = jnp.transpose(ot * (1.0 / l)).astype(BF16)


def mla_attention(qcat, kcat, vt, n_samples, n_tok, n_ctx, ctx_queries):
    tq = min(512, n_ctx) if ctx_queries else 512
    HP = 1
    nq, qrow, out_rows, orow = _query_rows(n_samples, n_tok, n_ctx, tq, ctx_queries)
    cblk = n_samples * n_tok // n_ctx
    WS, WV = HP * MLA_SLOT, HP * HEAD_DIM
    in_specs = [pl.BlockSpec((tq, WS), lambda b, hp, t: (qrow(b, t), hp))]
    args = [qcat]
    if not ctx_queries:
        in_specs += [pl.BlockSpec((n_tok, WS), lambda b, hp, t: (b, hp)),
                     pl.BlockSpec((WV, n_tok), lambda b, hp, t: (hp, b))]
        args += [kcat, vt]
    in_specs += [pl.BlockSpec((n_ctx, WS), lambda b, hp, t: (cblk + b, hp)),
                 pl.BlockSpec((WV, n_ctx), lambda b, hp, t: (hp, cblk + b))]
    args += [kcat, vt]
    return pl.pallas_call(
        functools.partial(_mla_attn_kernel, with_latent=not ctx_queries),
        grid=(n_samples, vt.shape[0] // WV, nq),
        in_specs=in_specs,
        out_specs=pl.BlockSpec((tq, WV), lambda b, hp, t: (orow(b, t), hp)),
        out_shape=jax.ShapeDtypeStruct((out_rows, vt.shape[0]), BF16),
        compiler_params=_cparams(3, 56), name="mla_attn_ctx" if ctx_queries else "mla_attn",
    )(*args)


def _diff_attn_kernel(q_ref, *rest, with_latent, out_scale):
    if with_latent:
        kl_ref, vtl_ref, kc_ref, vtc_ref, lam_ref, g_ref, o_ref = rest
    else:
        kc_ref, vtc_ref, lam_ref, g_ref, o_ref = rest
    lp = lam_ref[...]
    lam = (jnp.exp(jnp.sum(lp[0:1] * lp[1:2], axis=-1, keepdims=True))
           - jnp.exp(jnp.sum(lp[2:3] * lp[3:4], axis=-1, keepdims=True)) + lam_ref[4:5, 0:1])
    d = HEAD_DIM
    k_refs = [kc_ref] + ([kl_ref] if with_latent else [])
    vt_refs = [vtc_ref] + ([vtl_ref] if with_latent else [])
    parts = []
    for half in range(2):
        sl = slice(half * d, (half + 1) * d)
        q = q_ref[:, sl]
        p_list, l = _softmax_parts_t([_dot_nt(k[:, sl], q) for k in k_refs])
        parts.append((p_list, 1.0 / l))
    (p1_list, inv1), (p2_list, inv2) = parts
    r = lam * inv2 / inv1
    ot = functools.reduce(lambda a, b: a + b,
                          [_dot(vt[...], (p1 - p2 * r).astype(BF16))
                           for p1, p2, vt in zip(p1_list, p2_list, vt_refs)])
    o = jnp.transpose(ot * inv1)
    y = o * lax.rsqrt(jnp.mean(o * o, axis=-1, keepdims=True) + RMS_EPS) * g_ref[...]
    o_ref[...] = (y * out_scale).astype(BF16)


def diff_attention(dq, dk, vt, lam_rows, subln_g, lambda_init, n_samples, n_tok, n_ctx, ctx_queries):
    tq = min(512, n_ctx) if ctx_queries else 512
    W = 2 * HEAD_DIM
    nq, qrow, out_rows, orow = _query_rows(n_samples, n_tok, n_ctx, tq, ctx_queries)
    cblk = n_samples * n_tok // n_ctx
    in_specs = [pl.BlockSpec((tq, W), lambda b, h, t: (qrow(b, t), h))]
    args = [dq]
    if not ctx_queries:
        in_specs += [pl.BlockSpec((n_tok, W), lambda b, h, t: (b, h)),
                     pl.BlockSpec((W, n_tok), lambda b, h, t: (h, b))]
        args += [dk, vt]
    in_specs += [pl.BlockSpec((n_ctx, W), lambda b, h, t: (cblk + b, h)),
                 pl.BlockSpec((W, n_ctx), lambda b, h, t: (h, cblk + b)),
                 pl.BlockSpec((8, HEAD_DIM), lambda b, h, t: (0, 0)),
                 pl.BlockSpec((1, W), lambda b, h, t: (0, 0))]
    args += [dk, vt, lam_rows, subln_g.reshape(1, W)]
    n_heads = dk.shape[1] // W
    return pl.pallas_call(
        functools.partial(_diff_attn_kernel, with_latent=not ctx_queries, out_scale=1.0 - lambda_init),
        grid=(n_samples, n_heads, nq),
        in_specs=in_specs,
        out_specs=pl.BlockSpec((tq, W), lambda b, h, t: (orow(b, t), h)),
        out_shape=jax.ShapeDtypeStruct((out_rows, dk.shape[1]), BF16),
        compiler_params=_cparams(3, 56), name="diff_attn_ctx" if ctx_queries else "diff_attn",
    )(*args)


NA_DY = 2 * NA_KH - 1


def _na_bias_tables(rpb):
    W = GRID_W
    H = rpb.shape[0]
    qc = np.arange(W)[:, None]
    kc = np.arange(W)[None, :]
    cs = np.clip(qc - NA_KW // 2, 0, W - NA_KW)
    col_ok = (kc >= cs) & (kc < cs + NA_KW)
    dx = np.clip(kc - qc + NA_KW - 1, 0, 2 * NA_KW - 2)
    by_dx = jnp.take(rpb.astype(F32), jnp.asarray(dx.reshape(W * W)), axis=2).reshape(H, NA_DY, W, W)
    by_dx = jnp.where(jnp.asarray(col_ok)[None, None], by_dx, NEG_BIG)
    neg = jnp.full((H, 1, W, W), NEG_BIG, F32)
    padded = jnp.concatenate([neg, by_dx, neg, neg], axis=1)
    return jnp.concatenate([padded[:, :-1], padded[:, 1:]], axis=-1)


def _na_attn_kernel(q_ref, kl_ref, vl_ref, kc_ref, vc_ref, bias_ref, o_ref, *, scale, n_rows):
    rb = pl.program_id(2)
    r0 = jnp.clip(rb * 8 - 4, 0, n_rows - 16)
    start = pl.multiple_of(r0 * GRID_W, 256)
    kw = kl_ref[pl.ds(start, 16 * GRID_W), :]
    vw = vl_ref[pl.ds(start, 16 * GRID_W), :]
    q = q_ref[...]
    first, last_blk = rb == 0, rb == n_rows // 8 - 1
    dy0 = jnp.where(first, NA_KH - 1, jnp.where(last_blk, -1, NA_KH // 2 - 1))
    lane = lax.broadcasted_iota(jnp.int32, (GRID_W, 2 * GRID_W), 1)
    bias_rows = []
    for qa in range(8):
        win0 = jnp.where(first, max(qa - 4, 0), jnp.where(last_blk, min(qa + 4, 8), qa))
        tiles = []
        for kp in range(8):
            a1 = jnp.clip(2 * kp - qa + dy0 + 1, 0, NA_DY + 1)
            ok_even = jnp.logical_and(2 * kp >= win0, 2 * kp < win0 + NA_KH).astype(jnp.int32)
            ok_odd = jnp.logical_and(2 * kp + 1 >= win0, 2 * kp + 1 < win0 + NA_KH).astype(jnp.int32)
            ok = jnp.where(lane < GRID_W, ok_even, ok_odd)
            tiles.append(jnp.where(ok > 0, bias_ref[a1], NEG_BIG))
        bias_rows.append(jnp.concatenate(tiles, axis=1))
    s_lat = _dot_nt(q, kw) * scale + jnp.concatenate(bias_rows, axis=0)
    s_ctx = _dot_nt(q, kc_ref[...]) * scale
    (p_lat, p_ctx), l = _softmax_parts([s_lat, s_ctx])
    inv = 1.0 / l
    o = _dot((p_lat * inv).astype(BF16), vw) + _dot((p_ctx * inv).astype(BF16), vc_ref[...])
    o_ref[...] = o.astype(BF16)


def na_attention(z1, bias, n_samples, n_tok, n_ctx, n_heads):
    n_rows = n_tok // GRID_W
    assert n_rows >= 16 and n_rows % 8 == 0
    nrb = n_rows // 8
    tq = 8 * GRID_W
    d = HEAD_DIM
    cblk = n_samples * n_tok // n_ctx
    return pl.pallas_call(
        functools.partial(_na_attn_kernel, scale=d ** -0.5, n_rows=n_rows),
        grid=(n_samples, n_heads, nrb),
        in_specs=[pl.BlockSpec((tq, d), lambda b, h, rb: (b * nrb + rb, h)),
                  pl.BlockSpec((n_tok, d), lambda b, h, rb: (b, n_heads + h)),
                  pl.BlockSpec((n_tok, d), lambda b, h, rb: (b, 2 * n_heads + h)),
                  pl.BlockSpec((n_ctx, d), lambda b, h, rb: (cblk + b, n_heads + h)),
                  pl.BlockSpec((n_ctx, d), lambda b, h, rb: (cblk + b, 2 * n_heads + h)),
                  pl.BlockSpec((None, NA_DY + 2, GRID_W, 2 * GRID_W), lambda b, h, rb: (h, 0, 0, 0))],
        out_specs=pl.BlockSpec((tq, d), lambda b, h, rb: (b * nrb + rb, h)),
        out_shape=jax.ShapeDtypeStruct((n_samples * n_tok, n_heads * d), BF16),
        compiler_params=_cparams(3, 40), name="na_attn",
    )(z1, z1, z1, z1, z1, bias)


def _ctx_attn_kernel(q_ref, k_ref, v_ref, o_ref, *, scale):
    (p,), l = _softmax_parts([_dot_nt(q_ref[...], k_ref[...]) * scale])
    o_ref[...] = _dot((p * (1.0 / l)).astype(BF16), v_ref[...]).astype(BF16)


def na_ctx_attention(z1, n_samples, n_tok, n_ctx, n_heads):
    d = HEAD_DIM
    cblk = n_samples * n_tok // n_ctx
    return pl.pallas_call(
        functools.partial(_ctx_attn_kernel, scale=d ** -0.5),
        grid=(n_samples, n_heads),
        in_specs=[pl.BlockSpec((n_ctx, d), lambda b, h: (cblk + b, h)),
                  pl.BlockSpec((n_ctx, d), lambda b, h: (cblk + b, n_heads + h)),
                  pl.BlockSpec((n_ctx, d), lambda b, h: (cblk + b, 2 * n_heads + h))],
        out_specs=pl.BlockSpec((n_ctx, d), lambda b, h: (b, h)),
        out_shape=jax.ShapeDtypeStruct((n_samples * n_ctx, n_heads * d), BF16),
        compiler_params=_cparams(2, 16), name="na_attn_ctx",
    )(z1, z1, z1)


def _hgrn_chunk(q_raw, f_raw, v, lb, st_ref, reverse):
    C, HW = q_raw.shape
    dk = HEAD_DIM
    n_heads = HW // dk
    q = _silu(q_raw) * (dk ** -0.5)
    f = lb + (1.0 - lb) * jax.nn.sigmoid(f_raw)
    g = jnp.maximum(jnp.log(f) * LOG2E, -300.0)
    k = 1.0 - f
    t_idx = lax.broadcasted_iota(jnp.int32, (C, 1), 0)
    p = (C - 1 - t_idx) if reverse else t_idx

    def rotate_rows(x, s):
        if s % 8 == 0:
            return jnp.concatenate([x[C - s:], x[:C - s]], axis=0)
        return pltpu.roll(x, s, 0)

    def prev(x, s):
        return rotate_rows(x, (C - s) if reverse else s)

    def nxt(x, s):
        return rotate_rows(x, s if reverse else (C - s))

    def incl_scan(b):
        x = g
        pb = jnp.bitwise_and(p, b - 1)
        s = 1
        while s < b:
            x = x + jnp.where(pb >= s, prev(x, s), 0.0)
            s *= 2
        return x

    def excl_rscan(b):
        if b == 1:
            return jnp.zeros_like(g)
        pb = jnp.bitwise_and(p, b - 1)
        x = jnp.where(pb <= b - 2, nxt(g, 1), 0.0)
        s = 1
        while s < b:
            x = x + jnp.where(pb + s <= b - 1, nxt(x, s), 0.0)
            s *= 2
        return x

    cum = incl_scan(C)
    last = cum[0:1] if reverse else cum[C - 1:C]
    q_in = (q * jnp.exp2(cum)).astype(BF16)
    k_out = (k * jnp.exp2(last - cum)).astype(BF16)
    total = jnp.exp2(last)
    qb, kb, vb = q.astype(BF16), k.astype(BF16), v.astype(BF16)

    def boundary_rows(b):
        rows = []
        for j in range(C // (2 * b)):
            r = 2 * b * j + (b if reverse else b - 1)
            rows.append(jnp.broadcast_to(cum[r:r + 1], (2 * b, HW)))
        return jnp.concatenate(rows, axis=0)

    levels = []
    b = C // 2
    while b >= 1:
        upper = jnp.bitwise_and(p, b) != 0
        if b >= 8:
            x = cum - boundary_rows(b)
            e_up, e_lo = x, -x
        else:
            e_up, e_lo = incl_scan(b), excl_rscan(b)
        ql = jnp.where(upper, q * jnp.exp2(e_up), 0.0).astype(BF16)
        kl = jnp.where(upper, 0.0, k * jnp.exp2(e_lo)).astype(BF16)
        levels.append((b, ql, kl))
        b //= 2

    s_idx = lax.broadcasted_iota(jnp.int32, (1, C), 1)
    ps = (C - 1 - s_idx) if reverse else s_idx
    outs = []
    for h in range(n_heads):
        sl = slice(h * dk, (h + 1) * dk)
        st_old = st_ref[h]
        att = jnp.where(p == ps, _dot_nt(qb[:, sl], kb[:, sl]), 0.0)
        for b, ql, kl in levels:
            same_pair = jnp.bitwise_and(p, -2 * b) == jnp.bitwise_and(ps, -2 * b)
            att = att + jnp.where(same_pair, _dot_nt(ql[:, sl], kl[:, sl]), 0.0)
        outs.append(_dot_nt(q_in[:, sl], st_old.astype(BF16)) + _dot(att.astype(BF16), vb[:, sl]))
        st_ref[h] = total[:, sl] * st_old + _dot_tn(vb[:, sl], k_out[:, sl])
    return jnp.concatenate(outs, axis=1)


def _hgrn_kernel(*refs, n_samples):
    ns = n_samples
    fwd_in, bwd_in = refs[:3 * ns], refs[3 * ns:6 * ns]
    lb_ref, of_ref, ob_ref, sf_ref, sb_ref = refs[6 * ns:]

    @pl.when(pl.program_id(0) == 0)
    def _():
        sf_ref[...] = jnp.zeros_like(sf_ref)
        sb_ref[...] = jnp.zeros_like(sb_ref)

    lb = jnp.concatenate([lb_ref[...]] * ns, axis=1)
    HW = lb_ref.shape[1]
    for ins, o_ref, st_ref, rev in ((fwd_in, of_ref, sf_ref, False), (bwd_in, ob_ref, sb_ref, True)):
        q, f, v = (jnp.concatenate([ins[3 * s + k][...] for s in range(ns)], axis=1) for k in range(3))
        o = _hgrn_chunk(q, f, v, lb, st_ref, reverse=rev)
        for s in range(ns):
            o_ref[s] = o[:, s * HW:(s + 1) * HW]


def hgrn_scan(z2, lb, n_samples, n_tok, n_ctx):
    HW = z2.shape[1] // 5
    C = HGRN_CHUNK
    ncc, ncl = n_ctx // C, n_tok // C
    lat_blocks = n_samples * ncl

    def fwd_row(b, c):
        return jnp.where(c < ncc, lat_blocks + b * ncc + c, b * ncl + (c - ncc))

    def bwd_row(b, c):
        return jnp.where(c < ncc, lat_blocks + b * ncc + (ncc - 1 - c), b * ncl + (ncl - 1 - (c - ncc)))

    spec = lambda rowf, b, col: pl.BlockSpec((C, HW), lambda c: (rowf(b, c), col))
    samples = range(n_samples)
    in_specs = ([spec(fwd_row, b, col) for b in samples for col in (0, 1, 3)]
                + [spec(bwd_row, b, col) for b in samples for col in (0, 2, 3)]
                + [pl.BlockSpec((1, HW), lambda c: (0, 0))])
    n_heads_all = n_samples * HW // HEAD_DIM
    bwd_local = lambda c: jnp.where(c < ncc, ncc - 1 - c, ncc + ncl - 1 - (c - ncc))
    return pl.pallas_call(
        functools.partial(_hgrn_kernel, n_samples=n_samples), grid=(ncc + ncl,),
        in_specs=in_specs,
        out_specs=[pl.BlockSpec((n_samples, C, HW), lambda c: (0, c, 0)),
                   pl.BlockSpec((n_samples, C, HW), lambda c: (0, bwd_local(c), 0))],
        out_shape=[jax.ShapeDtypeStruct((n_samples, n_ctx + n_tok, HW), F32)] * 2,
        scratch_shapes=[pltpu.VMEM((n_heads_all, HEAD_DIM, HEAD_DIM), F32)] * 2,
        compiler_params=_cparams(1, 56), name="hgrn_scan",
    )(*([z2] * (6 * n_samples)), lb.reshape(1, HW))


def _hgrn_readout_kernel(of_ref, ob_ref, gz_ref, g_ref, o_ref):
    for h in range(of_ref.shape[1] // HEAD_DIM):
        sl = slice(h * HEAD_DIM, (h + 1) * HEAD_DIM)
        x = of_ref[:, sl] + ob_ref[:, sl]
        y = x * lax.rsqrt(jnp.mean(x * x, axis=-1, keepdims=True) + RMS_EPS) * g_ref[...]
        o_ref[:, sl] = (y * _silu(gz_ref[:, sl])).astype(BF16)


def hgrn_readout(o_f, o_b, z2, norm_g, n_rows, n_tok, n_ctx):
    n_samples, _, HW = o_f.shape
    tm = math.gcd(n_ctx, ROW_TILE)
    lat_tiles, tiles_per_sample, ctx_tiles = n_samples * n_tok // tm, n_tok // tm, n_ctx // tm

    def o_idx(i):
        is_lat = i < lat_tiles
        jc = i - lat_tiles
        return (jnp.where(is_lat, i // tiles_per_sample, jc // ctx_tiles),
                jnp.where(is_lat, ctx_tiles + i % tiles_per_sample, jc % ctx_tiles), 0)

    return pl.pallas_call(
        _hgrn_readout_kernel, grid=(n_rows // tm,),
        in_specs=[pl.BlockSpec((None, tm, HW), o_idx), pl.BlockSpec((None, tm, HW), o_idx),
                  pl.BlockSpec((tm, HW), lambda i: (i, 4)), pl.BlockSpec((1, HEAD_DIM), lambda i: (0, 0))],
        out_specs=pl.BlockSpec((tm, HW), lambda i: (i, 0)),
        out_shape=jax.ShapeDtypeStruct((n_rows, HW), BF16),
        compiler_params=_cparams(1, 32), name="hgrn_readout",
    )(o_f, o_b, z2, norm_g.reshape(1, HEAD_DIM))


def _merge_kernel(h_ref, oa_ref, od_ref, om_ref, or_ref, wg_ref, bg_ref, wb_ref, o_ref, wg_bf, wb_bf):
    @pl.when(pl.program_id(1) == 0)
    def _():
        wg_bf[...] = wg_ref[...].astype(BF16)
        wb_bf[...] = wb_ref[...].astype(BF16)

    h = h_ref[...]
    acc = None
    for j, o_j in enumerate((oa_ref, od_ref, om_ref, or_ref)):
        gate = jax.nn.sigmoid(_dot(h, wg_bf[j]) + bg_ref[j])
        term = gate * _dot(o_j[...], wb_bf[j])
        acc = term if acc is None else acc + term
    o_ref[...] = acc.astype(BF16)


def merge_branches(h, outs, w_gate, b_gate, w_branch, layer, n_rows):
    D = h.shape[1]
    L, nb, BW = w_branch.shape[:3]
    tm, tn = ROW_TILE, 256
    single = dict(pipeline_mode=pl.Buffered(1))
    return pl.pallas_call(
        _merge_kernel, grid=(D // tn, n_rows // tm),
        in_specs=[pl.BlockSpec((tm, D), lambda j, i: (i, 0))]
                 + [pl.BlockSpec((tm, BW), lambda j, i: (i, 0))] * nb
                 + [pl.BlockSpec((None, nb, D, tn), lambda j, i: (layer, 0, 0, j), **single),
                    pl.BlockSpec((None, nb, 1, tn), lambda j, i: (layer, 0, 0, j)),
                    pl.BlockSpec((None, nb, BW, tn), lambda j, i: (layer, 0, 0, j), **single)],
        out_specs=pl.BlockSpec((tm, tn), lambda j, i: (i, j)),
        out_shape=jax.ShapeDtypeStruct((n_rows, D), BF16),
        scratch_shapes=[pltpu.VMEM((nb, D, tn), BF16), pltpu.VMEM((nb, BW, tn), BF16)],
        compiler_params=_cparams(2, 58), name="merge",
    )(h, *outs, w_gate, b_gate.reshape(L, nb, 1, D), w_branch)


def _ffn_in_kernel(x_ref, w1_ref, w3_ref, o_ref, w1_bf, w3_bf):
    @pl.when(pl.program_id(1) == 0)
    def _():
        w1_bf[...] = w1_ref[...].astype(BF16)
        w3_bf[...] = w3_ref[...].astype(BF16)

    x = x_ref[...]
    o_ref[...] = (_silu(_dot(x, w1_bf[...])) * _dot(x, w3_bf[...])).astype(BF16)


def ffn_in(h, w1, w3, layer, n_rows):
    _, D, Fd = w1.shape
    tm, tn = _even_row_tile(n_rows), 256
    w_spec = pl.BlockSpec((None, D, tn), lambda j, i: (layer, 0, j))
    return pl.pallas_call(
        _ffn_in_kernel,
        grid=(Fd // tn, pl.cdiv(n_rows, tm)),
        in_specs=[pl.BlockSpec((tm, D), lambda j, i: (i, 0)), w_spec, w_spec],
        out_specs=pl.BlockSpec((tm, tn), lambda j, i: (i, j)),
        out_shape=jax.ShapeDtypeStruct((n_rows, Fd), BF16),
        scratch_shapes=[pltpu.VMEM((D, tn), BF16)] * 2,
        compiler_params=_cparams(2, 50), name="ffn_in",
    )(h, w1, w3)


def _group_spec(tm, tn, n_lat, n_samples):
    return pl.BlockSpec((None, 1, tn), lambda j, i: (_grp_of_tile(i, tm, n_lat, n_samples), 0, j))


def ffn_out(u, w2, layer, xres, gate, n_rows, n_lat, n_samples):
    D = w2.shape[2]
    G = gate.shape[0]
    tm, tn = 256, 512
    return _mm_call(u, w2, layer, n_rows=n_rows, tn=tn, tm=tm, out_dtype=F32,
                    epilogue=_residual_epilogue, extras=(xres, gate.reshape(G, 1, D)),
                    extra_specs=(pl.BlockSpec((tm, tn), lambda j, i: (i, j)), _group_spec(tm, tn, n_lat, n_samples)),
                    name="ffn_out")


MOE_TILE = 256


def moe_route_plan(comb, n_experts):
    M = comb.shape[0]
    T = MOE_TILE
    w = comb[:, :n_experts]
    sel = w > 0
    n_tiles = 2 * M // T + n_experts
    R = n_tiles * T
    cnt = jnp.sum(sel, axis=0, dtype=jnp.int32)
    rank = jnp.cumsum(sel, axis=0, dtype=jnp.int32) - 1
    gsz = (cnt + T - 1) // T * T
    gend = jnp.cumsum(gsz)
    off = gend - gsz
    dest_all = off[None, :] + rank
    tile_start = jnp.arange(n_tiles, dtype=jnp.int32) * T
    tile_expert = jnp.minimum(jnp.searchsorted(gend, tile_start, side="right"), n_experts - 1).astype(jnp.int32)
    n_used = (gend[-1] // T).reshape(1).astype(jnp.int32)
    k_next = jnp.searchsorted(tile_expert, tile_expert, side="right")
    next_expert = jnp.where(k_next < n_tiles, tile_expert[jnp.minimum(k_next, n_tiles - 1)], -1).astype(jnp.int32)
    e_lo = jnp.argmax(sel, axis=1)
    e_hi = n_experts - 1 - jnp.argmax(sel[:, ::-1], axis=1)
    take = lambda a, e: jnp.take_along_axis(a, e[:, None], axis=1)[:, 0]
    two = e_hi != e_lo
    dest = jnp.stack([take(dest_all, e_lo), take(dest_all, e_hi)], axis=1).astype(jnp.int32)
    tok = lax.broadcasted_iota(jnp.int32, dest.shape, 0)
    src_tok = jnp.zeros((R,), jnp.int32).at[dest.reshape(-1)].set(tok.reshape(-1))
    pw = jnp.zeros((M, 128), F32).at[:, 0].set(take(w, e_lo)).at[:, 1].set(jnp.where(two, take(w, e_hi), 0.0))
    return src_tok, (tile_expert, next_expert, n_used), dest, pw


def _row_copy(src_hbm, row, dst_buf, slot, r, sem):
    return pltpu.make_async_copy(src_hbm.at[pl.ds(row, 1)], dst_buf.at[slot, pl.ds(r, 1)], sem.at[slot])


def _gather_tile(idx_of_row, src_hbm, buf, sem, n_rows_tile):
    i = pl.program_id(0)
    n = pl.num_programs(0)

    def start_tile(t, slot):
        def body(r2, carry):
            for pri in range(2):
                r = 2 * r2 + pri
                _row_copy(src_hbm, idx_of_row(t * n_rows_tile + r), buf, slot, r, sem).start(priority=pri)
            return carry
        lax.fori_loop(0, n_rows_tile // 2, body, 0, unroll=4)

    @pl.when(i == 0)
    def _():
        start_tile(0, 0)

    @pl.when(i + 1 < n)
    def _():
        start_tile(i + 1, (i + 1) % 2)

    slot = i % 2

    def wait_body(r, carry):
        _row_copy(src_hbm, 0, buf, slot, r, sem).wait()
        return carry
    lax.fori_loop(0, n_rows_tile, wait_body, 0, unroll=8)
    return slot


def _moe_gather_kernel(src_ref, h_hbm, o_ref, buf, sem):
    slot = _gather_tile(lambda r: src_ref[r], h_hbm, buf, sem, o_ref.shape[0])
    o_ref[...] = buf[slot].astype(BF16)


def moe_gather(hf, src_tok):
    R = src_tok.shape[0]
    D = hf.shape[1]
    T = MOE_TILE
    return pl.pallas_call(
        _moe_gather_kernel,
        grid_spec=pltpu.PrefetchScalarGridSpec(
            num_scalar_prefetch=1, grid=(R // T,),
            in_specs=[pl.BlockSpec(memory_space=pl.ANY)],
            out_specs=pl.BlockSpec((T, D), lambda i, src: (i, 0)),
            scratch_shapes=[pltpu.VMEM((2, T, D), F32), pltpu.SemaphoreType.DMA((2,))]),
        out_shape=jax.ShapeDtypeStruct((R, D), BF16),
        compiler_params=_cparams(1, 24), name="moe_gather",
    )(src_tok, hf)


def _moe_mm_kernel(te_ref, nx_ref, nu_ref, x_ref, *rest, e0, n_w):
    w_hbms = rest[:n_w]
    o_ref = rest[n_w]
    stages, wbfs, sem = rest[n_w + 1:2 * n_w + 1], rest[2 * n_w + 1:3 * n_w + 1], rest[3 * n_w + 1]
    j, i, nj = pl.program_id(0), pl.program_id(1), pl.num_programs(0)
    tn = stages[0].shape[1]

    def block_copies(e, jj):
        cols = pl.ds(pl.multiple_of(jj * tn, tn), tn)
        return [pltpu.make_async_copy(w.at[e0 + e, :, cols], st, sem.at[k])
                for k, (w, st) in enumerate(zip(w_hbms, stages))]

    fresh = jnp.logical_or(i == 0, te_ref[i] != te_ref[jnp.maximum(i - 1, 0)])

    @pl.when(fresh)
    def _():
        @pl.when(jnp.logical_and(i == 0, j == 0))
        def _():
            for cp in block_copies(te_ref[0], 0):
                cp.start()

        for cp in block_copies(te_ref[i], j):
            cp.wait()
        for st, wbf in zip(stages, wbfs):
            _cast_rows(st, wbf)
        nxt = nx_ref[i]

        @pl.when(nxt >= 0)
        def _():
            for cp in block_copies(nxt, j):
                cp.start()

        @pl.when(jnp.logical_and(nxt < 0, j + 1 < nj))
        def _():
            for cp in block_copies(te_ref[0], j + 1):
                cp.start()

    @pl.when(i < nu_ref[0])
    def _():
        x = x_ref[...]
        if n_w == 2:
            o_ref[...] = (_silu(_dot(x, wbfs[0][...])) * _dot(x, wbfs[1][...])).astype(o_ref.dtype)
        else:
            o_ref[...] = _dot(x, wbfs[0][...]).astype(o_ref.dtype)

    @pl.when(i >= nu_ref[0])
    def _():
        o_ref[...] = jnp.zeros_like(o_ref)


def _moe_mm(x, ws, e0, plan, tn, out_dtype, name):
    tile_expert, next_expert, n_used = plan
    R, K = x.shape
    N = ws[0].shape[2]
    T = MOE_TILE
    n_w = len(ws)
    return pl.pallas_call(
        functools.partial(_moe_mm_kernel, e0=e0, n_w=n_w),
        grid_spec=pltpu.PrefetchScalarGridSpec(
            num_scalar_prefetch=3, grid=(N // tn, R // T),
            in_specs=[pl.BlockSpec((T, K), lambda j, i, te, nx, nu: (i, 0))]
                     + [pl.BlockSpec(memory_space=pl.ANY)] * n_w,
            out_specs=pl.BlockSpec((T, tn), lambda j, i, te, nx, nu: (i, j)),
            scratch_shapes=[pltpu.VMEM((K, tn), F32)] * n_w + [pltpu.VMEM((K, tn), BF16)] * n_w
                           + [pltpu.SemaphoreType.DMA((n_w,))]),
        out_shape=jax.ShapeDtypeStruct((R, N), out_dtype),
        compiler_params=_cparams(2, 48), name=name,
    )(tile_expert, next_expert, n_used, x, *ws)


def moe_ffn_in(xs, w1, w3, e0, plan):
    return _moe_mm(xs, (w1, w3), e0, plan, 512, BF16, "moe_ffn_in")


def moe_ffn_out(u, w2, e0, plan):
    return _moe_mm(u, (w2,), e0, plan, 2048, F32, "moe_ffn_out")


def _moe_combine_kernel(dest_ref, y_hbm, pw_ref, x_ref, g_ref, *rest, with_norm):
    if with_norm:
        ng_ref, o_ref, buf, sem = rest
    else:
        o_ref, buf, sem = rest
    T = o_ref.shape[0]
    slot = _gather_tile(lambda r: dest_ref[r], y_hbm, buf, sem, 2 * T)
    pw = pw_ref[...]
    mix = pw[:, 0:1] * buf[slot, 0:T, :] + pw[:, 1:2] * buf[slot, T:2 * T, :]
    x = x_ref[...] + g_ref[...] * mix
    if with_norm:
        x = x * lax.rsqrt(jnp.mean(x * x, axis=-1, keepdims=True) + RMS_EPS) * ng_ref[...]
    o_ref[...] = x


def moe_combine(y, dest, pw, xres, gate, n_lat, n_samples, norm_g=None):
    M, D = xres.shape
    G = gate.shape[0]
    T = 128
    dest = dest.reshape(M // T, T, 2).transpose(0, 2, 1)
    in_specs = [pl.BlockSpec(memory_space=pl.ANY),
                pl.BlockSpec((T, 128), lambda i, d: (i, 0)),
                pl.BlockSpec((T, D), lambda i, d: (i, 0)),
                pl.BlockSpec((None, 1, D), lambda i, d: (_grp_of_tile(i, T, n_lat, n_samples), 0, 0))]
    args = [dest.reshape(-1), y, pw, xres, gate.reshape(G, 1, D)]
    if norm_g is not None:
        in_specs.append(pl.BlockSpec((1, D), lambda i, d: (0, 0)))
        args.append(norm_g.reshape(1, D))
    return pl.pallas_call(
        functools.partial(_moe_combine_kernel, with_norm=norm_g is not None),
        grid_spec=pltpu.PrefetchScalarGridSpec(
            num_scalar_prefetch=1, grid=(M // T,),
            in_specs=in_specs,
            out_specs=pl.BlockSpec((T, D), lambda i, d: (i, 0)),
            scratch_shapes=[pltpu.VMEM((2, 2 * T, D), F32), pltpu.SemaphoreType.DMA((2,))]),
        out_shape=jax.ShapeDtypeStruct((M, D), F32),
        compiler_params=_cparams(1, 32), name="moe_combine",
    )(*args)


def kernel(x, c, ctx, c_ctx, norm1_g, norm2_g, w_ada, b_ada, w_in, na_rpb, diff_lambda, diff_subln_g,
           mla_q_norm_g, mla_w_q_up, mla_kv_norm_g, mla_w_kv_up, hgrn_lower_bounds, hgrn_norm_g,
           w_branch, w_gate, b_gate, w_out, ffn_w1, ffn_w3, ffn_w2, moe_router, moe_w1, moe_w3, moe_w2,
           final_norm_g):
    B, N, D = x.shape
    NC = ctx.shape[1]
    L = w_ada.shape[0]
    n_lat, n_all = B * N, B * N + B * NC
    BW = w_branch.shape[2]
    n_heads = BW // HEAD_DIM
    HG0 = w_in.shape[2] - 5 * BW
    assert N % ROW_TILE == 0 and (B * NC) % ROW_TILE == 0 and NC % HGRN_CHUNK == 0

    xs = jnp.concatenate([x.reshape(n_lat, D), ctx.reshape(B * NC, D)], axis=0)

    cond8 = jnp.zeros((8, D), F32).at[:B].set(c).at[B].set(c_ctx)
    mods = adaln_all(cond8, w_ada, b_ada)[:, :B + 1].reshape(L, B + 1, 6, D)

    lb_all = jnp.cumsum(jax.nn.softmax(hgrn_lower_bounds.astype(F32), axis=0), axis=0)
    lb_all = lb_all - lb_all[0:1]

    w_in_t = jnp.swapaxes(w_in, 1, 2)

    cosd, sind = _rope_tables(N, B * NC, 2 * 64, ROW_TILE)
    cosm, sinm = _rope_tables(N, B * NC, 64, ROW_TILE)

    RQ, RKV = mla_w_q_up.shape[1], mla_w_kv_up.shape[1]
    wq4 = mla_w_q_up.reshape(L, RQ, n_heads, HEAD_DIM + MLA_ROPE)
    wq_cat = jnp.pad(wq4, ((0, 0), (0, 0), (0, 0), (0, MLA_SLOT - HEAD_DIM - MLA_ROPE))).reshape(L, RQ, -1)
    wkv4 = mla_w_kv_up.reshape(L, RKV, n_heads, 2 * HEAD_DIM)
    wkn = wkv4[..., :HEAD_DIM].reshape(L, RKV, -1)
    wvt = jnp.swapaxes(wkv4[..., HEAD_DIM:].reshape(L, RKV, -1), 1, 2)

    for l in range(L):
        need_ctx = l < L - 1
        n_rows = n_all if need_ctx else n_lat
        lambda_init = 0.8 - 0.6 * math.exp(-0.3 * l)
        sh1, sc1, g1, sh2, sc2, g2 = (mods[l, :, k] for k in range(6))

        h = norm_mod(xs, norm1_g[l], sh1, sc1, N, B)
        z1 = wt_stream_matmul(h, w_in_t, l, 0, -(-HG0 // 512) * 512, BF16, "w_in_attn")
        z2 = wt_stream_matmul(h, w_in_t, l, HG0, 5 * BW, F32, "w_in_hgrn")

        bias = _na_bias_tables(na_rpb[l])
        o_a = na_attention(z1, bias, B, N, NC, n_heads)
        dq, dk, dvt = diff_rope(z1, cosd, sind, N, n_lat, BW)
        lam_rows = jnp.zeros((8, HEAD_DIM), F32).at[:4].set(diff_lambda[l]).at[4].set(lambda_init)
        o_d = diff_attention(dq, dk, dvt, lam_rows, diff_subln_g[l], lambda_init, B, N, NC, False)
        qcat, kcat, mvt = mla_prep(z1, cosm, sinm, mla_q_norm_g[l], wq_cat[l], mla_kv_norm_g[l], wkn[l], wvt[l],
                                   N, n_lat, 6 * BW)
        o_m = mla_attention(qcat, kcat, mvt, B, N, NC, False)
        o_f, o_b = hgrn_scan(z2, lb_all[l], B, N, NC)
        o_r = hgrn_readout(o_f, o_b, z2, hgrn_norm_g[l], n_rows, N, NC)
        if need_ctx:
            o_a = jnp.concatenate([o_a, na_ctx_attention(z1, B, N, NC, n_heads)], axis=0)
            o_d = jnp.concatenate([o_d, diff_attention(dq, dk, dvt, lam_rows, diff_subln_g[l], lambda_init,
                                                       B, N, NC, True)], axis=0)
            o_m = jnp.concatenate([o_m, mla_attention(qcat, kcat, mvt, B, N, NC, True)], axis=0)

        s = merge_branches(h, (o_a, o_d, o_m, o_r), w_gate, b_gate, w_branch, l, n_rows)
        G = B + 1
        tmo = 2 * ROW_TILE
        xs = _mm_call(s, w_out, l, n_rows=n_rows, tn=512, tm=tmo, out_dtype=F32,
                      epilogue=_residual_epilogue, extras=(xs, g1.reshape(G, 1, D)),
                      extra_specs=(pl.BlockSpec((tmo, 512), lambda j, i: (i, j)), _group_spec(tmo, 512, N, B)),
                      name="w_out")

        j = l // 2
        if l % 2 == 0:
            h2 = norm_mod(xs, norm2_g[l], sh2, sc2, N, B)
            u = ffn_in(h2, ffn_w1, ffn_w3, j, n_rows)
            xs = ffn_out(u, ffn_w2, j, xs, g2, n_rows, N, B)
        else:
            hf, comb = norm_mod(xs, norm2_g[l], sh2, sc2, N, B, w_router=moe_router[j])
            NM, E, _, FE = moe_w1.shape
            src_tok, plan, dest, pw = moe_route_plan(comb, E)
            xg = moe_gather(hf, src_tok)
            u = moe_ffn_in(xg, moe_w1.reshape(NM * E, D, FE), moe_w3.reshape(NM * E, D, FE), j * E, plan)
            y = moe_ffn_out(u, moe_w2.reshape(NM * E, FE, D), j * E, plan)
            xs = moe_combine(y, dest, pw, xs, g2, N, B, norm_g=final_norm_g if l == L - 1 else None)

    if L % 2 == 1:
        xs = final_norm(xs[:n_lat], final_norm_g)
    return xs.reshape(B, N, D)
```

```python
import functools
import math

import numpy as np
import jax
import jax.numpy as jnp
from jax import lax
from jax.experimental import pallas as pl
from jax.experimental.pallas import tpu as pltpu

F32 = jnp.float32
BF16 = jnp.bfloat16

GRID_W = 64
RMS_EPS = 1e-6
ROPE_BASE = 10000.0
NA_KH, NA_KW = 8, 16
HEAD_DIM = 128
MLA_ROPE = 64
HGRN_CHUNK = 64
NEG_BIG = -1e30

VMEM_LIMIT_V7X = 60000 * 1024
ROW_TILE = 512


def _even_row_tile(n_rows, cap=1152):
    return next(t for t in range(cap, 15, -16) if n_rows % t == 0)


def _cparams(n_axes, vmem_mb):
    return pltpu.CompilerParams(dimension_semantics=("arbitrary",) * n_axes,
                                vmem_limit_bytes=min(int(vmem_mb * 2**20), VMEM_LIMIT_V7X))


def _silu(x):
    return x * jax.nn.sigmoid(x)


def _dot(a, b):
    return jnp.dot(a, b, preferred_element_type=F32)


def _dot_nt(a, b):
    return lax.dot_general(a, b, (((1,), (1,)), ((), ())), preferred_element_type=F32)


def _dot_tn(a, b):
    return lax.dot_general(a, b, (((0,), (0,)), ((), ())), preferred_element_type=F32)


def _adaln_kernel(c_ref, w_ref, b_ref, o_ref):
    @pl.when(pl.program_id(1) == 0)
    def _():
        o_ref[...] = jnp.broadcast_to(b_ref[...], o_ref.shape)

    x = _silu(c_ref[...]).astype(BF16)
    o_ref[...] += _dot(x, w_ref[...].astype(BF16))


def adaln_all(cond8, w_ada, b_ada):
    L, D, N6 = w_ada.shape
    tk = 128
    cond_k = cond8.reshape(8, D // tk, tk).transpose(1, 0, 2)
    return pl.pallas_call(
        _adaln_kernel,
        grid=(L, D // tk),
        in_specs=[pl.BlockSpec((None, 8, tk), lambda l, k: (k, 0, 0)),
                  pl.BlockSpec((None, tk, N6), lambda l, k: (l, k, 0)),
                  pl.BlockSpec((None, 1, N6), lambda l, k: (l, 0, 0))],
        out_specs=pl.BlockSpec((None, 8, N6), lambda l, k: (l, 0, 0)),
        out_shape=jax.ShapeDtypeStruct((L, 8, N6), F32),
        compiler_params=_cparams(2, 48),
        name="adaln",
    )(cond_k, w_ada, b_ada.reshape(L, 1, N6))


def _norm_mod_kernel(x_ref, g_ref, sh_ref, sc_ref, *rest, n_experts):
    with_router = n_experts is not None
    x = x_ref[...]
    y = x * lax.rsqrt(jnp.mean(x * x, axis=-1, keepdims=True) + RMS_EPS) * g_ref[...]
    h = y * (1.0 + sc_ref[...]) + sh_ref[...]
    if not with_router:
        (h_ref,) = rest
        h_ref[...] = h.astype(BF16)
        return
    wr_ref, h_ref, comb_ref = rest
    h_ref[...] = h
    logits = jnp.dot(h, wr_ref[...], precision=lax.Precision.HIGHEST, preferred_element_type=F32)
    lane = lax.broadcasted_iota(jnp.int32, logits.shape, 1).astype(F32)
    logits = jnp.where(lane < n_experts, logits, -jnp.inf)
    m1 = jnp.max(logits, axis=-1, keepdims=True)
    i1 = jnp.min(jnp.where(logits == m1, lane, 128.0), axis=-1, keepdims=True)
    rest_l = jnp.where(lane == i1, -jnp.inf, logits)
    m2 = jnp.max(rest_l, axis=-1, keepdims=True)
    i2 = jnp.min(jnp.where(rest_l == m2, lane, 128.0), axis=-1, keepdims=True)
    e2 = jnp.exp(m2 - m1)
    p1 = 1.0 / (1.0 + e2)
    p2 = e2 / (1.0 + e2)
    comb_ref[...] = jnp.where(lane == i1, p1, 0.0) + jnp.where(lane == i2, p2, 0.0)


def _grp_of_tile(i, tm, lat_rows_per_sample, n_samples):
    return jnp.minimum((i * tm) // lat_rows_per_sample, n_samples)


def norm_mod(x, g, shift, scale, n_lat, n_samples, w_router=None):
    M, D = x.shape
    tm = 256
    G = shift.shape[0]
    grp = lambda i: (_grp_of_tile(i, tm, n_lat, n_samples), 0, 0)
    in_specs = [pl.BlockSpec((tm, D), lambda i: (i, 0)),
                pl.BlockSpec((1, D), lambda i: (0, 0)),
                pl.BlockSpec((None, 1, D), grp),
                pl.BlockSpec((None, 1, D), grp)]
    args = [x, g.reshape(1, D), shift.reshape(G, 1, D), scale.reshape(G, 1, D)]
    out_specs = [pl.BlockSpec((tm, D), lambda i: (i, 0))]
    out_shape = [jax.ShapeDtypeStruct((M, D), BF16 if w_router is None else F32)]
    if w_router is not None:
        wr = jnp.zeros((D, 128), F32).at[:, :w_router.shape[1]].set(w_router)
        in_specs.append(pl.BlockSpec((D, 128), lambda i: (0, 0)))
        args.append(wr)
        out_specs.append(pl.BlockSpec((tm, 128), lambda i: (i, 0)))
        out_shape.append(jax.ShapeDtypeStruct((M, 128), F32))
    outs = pl.pallas_call(
        functools.partial(_norm_mod_kernel, n_experts=None if w_router is None else w_router.shape[1]),
        grid=(M // tm,),
        in_specs=in_specs, out_specs=out_specs, out_shape=out_shape,
        compiler_params=_cparams(1, 40),
        name="norm_mod_router" if w_router is not None else "norm_mod",
    )(*args)
    return outs if w_router is not None else outs[0]


def _final_norm_kernel(x_ref, g_ref, o_ref):
    x = x_ref[...]
    o_ref[...] = x * lax.rsqrt(jnp.mean(x * x, axis=-1, keepdims=True) + RMS_EPS) * g_ref[...]


def final_norm(x, g):
    M, D = x.shape
    tm = 256
    return pl.pallas_call(
        _final_norm_kernel, grid=(M // tm,),
        in_specs=[pl.BlockSpec((tm, D), lambda i: (i, 0)), pl.BlockSpec((1, D), lambda i: (0, 0))],
        out_specs=pl.BlockSpec((tm, D), lambda i: (i, 0)),
        out_shape=jax.ShapeDtypeStruct((M, D), F32),
        compiler_params=_cparams(1, 40), name="final_norm",
    )(x, g.reshape(1, D))


def _cast_rows(src_ref, dst_ref, rows=512):
    n = src_ref.shape[0]
    for r0 in range(0, n, rows):
        r1 = min(r0 + rows, n)
        dst_ref[r0:r1, :] = src_ref[r0:r1, :].astype(BF16)


def _stream_mm_kernel(x_ref, w_hbm, *rest, layer, n_extra, epilogue):
    extras = rest[:n_extra]
    o_ref, stage, wbf, sem = rest[n_extra:]
    j, i, nj = pl.program_id(0), pl.program_id(1), pl.num_programs(0)
    tn = stage.shape[1]

    def block_copy(jj):
        return pltpu.make_async_copy(w_hbm.at[layer, :, pl.ds(pl.multiple_of(jj * tn, tn), tn)], stage, sem)

    @pl.when(i == 0)
    def _():
        @pl.when(j == 0)
        def _():
            block_copy(0).start()

        block_copy(j).wait()
        _cast_rows(stage, wbf)

        @pl.when(j + 1 < nj)
        def _():
            block_copy(j + 1).start()

    acc = _dot(x_ref[...], wbf[...])
    if epilogue is not None:
        acc = epilogue(acc, *[e[...] for e in extras])
    o_ref[...] = acc.astype(o_ref.dtype)


def _mm_call(x, w, layer, *, n_rows, tn, tm, out_dtype, epilogue=None, extras=(), extra_specs=(), name="mm"):
    K, N = w.shape[1:]
    in_specs = [pl.BlockSpec((tm, K), lambda j, i: (i, 0)), pl.BlockSpec(memory_space=pl.ANY)]
    in_specs += list(extra_specs)
    out_bytes = jnp.dtype(out_dtype).itemsize
    vmem = (2 * tm * K * 2 + K * tn * 4 + K * tn * 2 + 2 * tm * tn * out_bytes + 3 * tm * tn * 4
            + len(extras) * 2 * tm * tn * 4) / 2**20 + 6
    return pl.pallas_call(
        functools.partial(_stream_mm_kernel, layer=layer, n_extra=len(extras), epilogue=epilogue),
        grid=(N // tn, pl.cdiv(n_rows, tm)),
        in_specs=in_specs,
        out_specs=pl.BlockSpec((tm, tn), lambda j, i: (i, j)),
        out_shape=jax.ShapeDtypeStruct((n_rows, N), out_dtype),
        scratch_shapes=[pltpu.VMEM((K, tn), F32), pltpu.VMEM((K, tn), BF16), pltpu.SemaphoreType.DMA(())],
        compiler_params=_cparams(2, vmem),
        name=name,
    )(x, w, *extras)


def _residual_epilogue(acc, xres, gate):
    return xres + gate * acc


def _wt_stream_mm_kernel(x_ref, wt_hbm, o_ref, stage, wbf, sem, *, layer, row0):
    j, i, nj = pl.program_id(0), pl.program_id(1), pl.num_programs(0)
    tn, K = stage.shape

    def block_copy(jj):
        return pltpu.make_async_copy(wt_hbm.at[layer, pl.ds(row0 + jj * tn, tn), :], stage, sem)

    @pl.when(i == 0)
    def _():
        @pl.when(j == 0)
        def _():
            block_copy(0).start()

        block_copy(j).wait()
        for c in range(0, K, tn):
            wbf[c:c + tn, :] = jnp.transpose(stage[:, c:c + tn]).astype(BF16)

        @pl.when(j + 1 < nj)
        def _():
            block_copy(j + 1).start()

    o_ref[...] = _dot(x_ref[...], wbf[...]).astype(o_ref.dtype)


def wt_stream_matmul(x, wt, layer, row0, n_cols, out_dtype, name):
    M, K = x.shape
    tm, tn = _even_row_tile(M), 512
    assert n_cols % tn == 0 and K % tn == 0 and row0 % 8 == 0
    return pl.pallas_call(
        functools.partial(_wt_stream_mm_kernel, layer=layer, row0=row0),
        grid=(n_cols // tn, pl.cdiv(M, tm)),
        in_specs=[pl.BlockSpec((tm, K), lambda j, i: (i, 0)), pl.BlockSpec(memory_space=pl.ANY)],
        out_specs=pl.BlockSpec((tm, tn), lambda j, i: (i, j)),
        out_shape=jax.ShapeDtypeStruct((M, n_cols), out_dtype),
        scratch_shapes=[pltpu.VMEM((tn, K), F32), pltpu.VMEM((K, tn), BF16), pltpu.SemaphoreType.DMA(())],
        compiler_params=_cparams(2, 48), name=name,
    )(x, wt)


def _rope_tables(n_tokens, n_ident, group, tile_rows):
    half = group // 2
    q = half // 2
    pos = np.arange(n_tokens)
    row, col = pos // GRID_W, pos % GRID_W
    freqs = ROPE_BASE ** (-(np.arange(q, dtype=np.float64) / q))
    lane = np.arange(128)
    in_group = lane % group
    axis_pos = np.where((in_group < half)[None, :], row[:, None], col[:, None]).astype(np.float64)
    ang = axis_pos * freqs[(in_group % half) % q][None, :]
    sign = np.where((in_group % half) < q, -1.0, 1.0)[None, :]
    cos, sin = np.cos(ang), np.sin(ang) * sign
    n_pad = -(-n_ident // tile_rows) * tile_rows
    cos = np.concatenate([cos, np.ones((n_pad, 128))], axis=0)
    sin = np.concatenate([sin, np.zeros((n_pad, 128))], axis=0)
    return jnp.asarray(cos, F32), jnp.asarray(sin, F32)


def _rope_apply(x, cos, sin, q):
    lane = lax.broadcasted_iota(jnp.int32, x.shape, 1)
    partner = jnp.where((lane % (2 * q)) < q, pltpu.roll(x, 128 - q, 1), pltpu.roll(x, q, 1))
    return x * cos + partner * sin


LOG2E = math.log2(math.e)


def _diff_rope_kernel(q_ref, k_ref, v_ref, cos_ref, sin_ref, qo_ref, ko_ref, vt_ref, *, q_scale):
    vt_ref[...] = jnp.transpose(v_ref[...].astype(F32)).astype(BF16)
    cos, sin = cos_ref[...], sin_ref[...]
    for src, dst, mult in ((q_ref, qo_ref, q_scale), (k_ref, ko_ref, None)):
        for c in range(src.shape[1] // 128):
            sl = slice(c * 128, (c + 1) * 128)
            y = _rope_apply(src[:, sl].astype(F32), cos, sin, 32)
            dst[:, sl] = (y if mult is None else y * mult).astype(BF16)


def _table_block(i, tm, n_tok, n_lat_total):
    return jnp.where(i * tm < n_lat_total, (i * tm % n_tok) // tm, n_tok // tm)


def diff_rope(z1, cosd, sind, n_tok, n_lat_total, W):
    M = z1.shape[0]
    tm = ROW_TILE
    tb = lambda i: (_table_block(i, tm, n_tok, n_lat_total), 0)
    return pl.pallas_call(
        functools.partial(_diff_rope_kernel, q_scale=HEAD_DIM ** -0.5 * LOG2E), grid=(M // tm,),
        in_specs=[pl.BlockSpec((tm, W), lambda i: (i, 3)), pl.BlockSpec((tm, W), lambda i: (i, 4)),
                  pl.BlockSpec((tm, W), lambda i: (i, 5)),
                  pl.BlockSpec((tm, 128), tb), pl.BlockSpec((tm, 128), tb)],
        out_specs=[pl.BlockSpec((tm, W), lambda i: (i, 0))] * 2 + [pl.BlockSpec((W, tm), lambda i: (0, i))],
        out_shape=[jax.ShapeDtypeStruct((M, W), BF16)] * 2 + [jax.ShapeDtypeStruct((W, M), BF16)],
        compiler_params=_cparams(1, 40), name="diff_rope",
    )(z1, z1, z1, cosd, sind)


MLA_SLOT = 2 * HEAD_DIM


def _mla_prep_kernel(cq_ref, ckva_ref, ckvb_ref, kr_ref, cos_ref, sin_ref, gq_ref, gkv_ref, wq_ref, wkn_ref,
                     wvt_ref, qcat_ref, kcat_ref, vt_ref, wq_bf, wkn_bf, wvt_bf, *, q_scale):
    @pl.when(pl.program_id(0) == 0)
    def _():
        wq_bf[...] = wq_ref[...].astype(BF16)
        wkn_bf[...] = wkn_ref[...].astype(BF16)
        wvt_bf[...] = wvt_ref[...].astype(BF16)

    cos, sin = cos_ref[...], sin_ref[...]
    d = HEAD_DIM
    n_heads = qcat_ref.shape[1] // MLA_SLOT
    cq = cq_ref[...].astype(F32)
    cqn = (cq * lax.rsqrt(jnp.mean(cq * cq, axis=-1, keepdims=True) + RMS_EPS) * gq_ref[...]).astype(BF16)
    q = _dot(cqn, wq_bf[...])
    a = ckva_ref[...].astype(F32)
    b = ckvb_ref[...].astype(F32)
    ha = a.shape[1]
    ms = (jnp.sum(a * a, axis=-1, keepdims=True) + jnp.sum(b * b, axis=-1, keepdims=True)) / (2 * ha)
    r = lax.rsqrt(ms + RMS_EPS)
    g = gkv_ref[...]
    ckvn = jnp.concatenate([(a * r * g[:, :ha]).astype(BF16), (b * r * g[:, ha:]).astype(BF16)], axis=1)
    kn = _dot(ckvn, wkn_bf[...])
    lane = lax.broadcasted_iota(jnp.int32, cos.shape, 1)
    kr = jnp.where(lane < MLA_ROPE, _rope_apply(kr_ref[...].astype(F32), cos, sin, 16), 0.0).astype(BF16)
    for h in range(n_heads):
        c0 = h * MLA_SLOT
        qcat_ref[:, c0:c0 + d] = (q[:, c0:c0 + d] * q_scale).astype(BF16)
        qcat_ref[:, c0 + d:c0 + 2 * d] = (_rope_apply(q[:, c0 + d:c0 + 2 * d], cos, sin, 16) * q_scale).astype(BF16)
        kcat_ref[:, c0:c0 + d] = kn[:, h * d:(h + 1) * d].astype(BF16)
        kcat_ref[:, c0 + d:c0 + 2 * d] = kr
    vt_ref[...] = _dot_nt(wvt_bf[...], ckvn).astype(BF16)


def mla_prep(z1, cosm, sinm, gq, wq_cat, gkv, wkn, wvt, n_tok, n_lat_total, col0):
    M = z1.shape[0]
    tm = ROW_TILE
    RQ, NQ = wq_cat.shape
    RKV, NKN = wkn.shape
    c_kv, c_kr = col0 + RQ, col0 + RQ + RKV
    assert col0 % RQ == 0 and c_kv % (RKV // 2) == 0 and c_kr % 128 == 0
    tb = lambda i: (_table_block(i, tm, n_tok, n_lat_total), 0)
    const = lambda i: (0, 0)
    return pl.pallas_call(
        functools.partial(_mla_prep_kernel, q_scale=(HEAD_DIM + MLA_ROPE) ** -0.5 * LOG2E), grid=(M // tm,),
        in_specs=[pl.BlockSpec((tm, RQ), lambda i: (i, col0 // RQ)),
                  pl.BlockSpec((tm, RKV // 2), lambda i: (i, c_kv // (RKV // 2))),
                  pl.BlockSpec((tm, RKV // 2), lambda i: (i, c_kv // (RKV // 2) + 1)),
                  pl.BlockSpec((tm, 128), lambda i: (i, c_kr // 128)),
                  pl.BlockSpec((tm, 128), tb), pl.BlockSpec((tm, 128), tb),
                  pl.BlockSpec((1, RQ), const), pl.BlockSpec((1, RKV), const),
                  pl.BlockSpec((RQ, NQ), const), pl.BlockSpec((RKV, NKN), const), pl.BlockSpec((NKN, RKV), const)],
        out_specs=[pl.BlockSpec((tm, NQ), lambda i: (i, 0)),
                   pl.BlockSpec((tm, NQ), lambda i: (i, 0)),
                   pl.BlockSpec((NKN, tm), lambda i: (0, i))],
        out_shape=[jax.ShapeDtypeStruct((M, NQ), BF16), jax.ShapeDtypeStruct((M, NQ), BF16),
                   jax.ShapeDtypeStruct((NKN, M), BF16)],
        scratch_shapes=[pltpu.VMEM((RQ, NQ), BF16), pltpu.VMEM((RKV, NKN), BF16), pltpu.VMEM((NKN, RKV), BF16)],
        compiler_params=_cparams(1, 56), name="mla_prep",
    )(z1, z1, z1, z1, cosm, sinm, gq.reshape(1, RQ), gkv.reshape(1, RKV), wq_cat, wkn, wvt)


def _softmax_parts(s_list):
    m = functools.reduce(jnp.maximum, [jnp.max(s, axis=-1, keepdims=True) for s in s_list])
    p_list = [jnp.exp(s - m) for s in s_list]
    l = functools.reduce(lambda a, b: a + b, [jnp.sum(p, axis=-1, keepdims=True) for p in p_list])
    return p_list, l


def _softmax_parts_t(st_list):
    m = functools.reduce(jnp.maximum, [jnp.max(s, axis=0, keepdims=True) for s in st_list])
    p_list = [jnp.exp2(s - m) for s in st_list]
    l = functools.reduce(lambda a, b: a + b, [jnp.sum(p, axis=0, keepdims=True) for p in p_list])
    return p_list, l


def _query_rows(n_samples, n_tok, n_ctx, tq, ctx_queries):
    if ctx_queries:
        nq = n_ctx // tq
        first = n_samples * n_tok // tq
        return nq, (lambda b, t: first + b * nq + t), n_samples * n_ctx, (lambda b, t: b * nq + t)
    nq = n_tok // tq
    qrow = lambda b, t: b * nq + t
    return nq, qrow, n_samples * n_tok, qrow


def _mla_attn_kernel(q_ref, *rest, with_latent):
    if with_latent:
        kl_ref, vtl_ref, kc_ref, vtc_ref, o_ref = rest
    else:
        kc_ref, vtc_ref, o_ref = rest
    for hh in range(q_ref.shape[1] // MLA_SLOT):
        ds = slice(hh * MLA_SLOT, (hh + 1) * MLA_SLOT)
        dv = slice(hh * HEAD_DIM, (hh + 1) * HEAD_DIM)
        q = q_ref[:, ds]
        k_list = [kc_ref[:, ds]] + ([kl_ref[:, ds]] if with_latent else [])
        vt_list = [vtc_ref[dv, :]] + ([vtl_ref[dv, :]] if with_latent else [])
        p_list, l = _softmax_parts_t([_dot_nt(k, q) for k in k_list])
        ot = functools.reduce(lambda a, b: a + b, [_dot(vt, p.astype(BF16)) for vt, p in zip(vt_list, p_list)])
        o_ref[:, dv] = jnp.transpose(ot * (1.0 / l)).astype(BF16)


def mla_attention(qcat, kcat, vt, n_samples, n_tok, n_ctx, ctx_queries):
    tq = min(512, n_ctx) if ctx_queries else 512
    HP = 1
    nq, qrow, out_rows, orow = _query_rows(n_samples, n_tok, n_ctx, tq, ctx_queries)
    cblk = n_samples * n_tok // n_ctx
    WS, WV = HP * MLA_SLOT, HP * HEAD_DIM
    in_specs = [pl.BlockSpec((tq, WS), lambda b, hp, t: (qrow(b, t), hp))]
    args = [qcat]
    if not ctx_queries:
        in_specs += [pl.BlockSpec((n_tok, WS), lambda b, hp, t: (b, hp)),
                     pl.BlockSpec((WV, n_tok), lambda b, hp, t: (hp, b))]
        args += [kcat, vt]
    in_specs += [pl.BlockSpec((n_ctx, WS), lambda b, hp, t: (cblk + b, hp)),
                 pl.BlockSpec((WV, n_ctx), lambda b, hp, t: (hp, cblk + b))]
    args += [kcat, vt]
    return pl.pallas_call(
        functools.partial(_mla_attn_kernel, with_latent=not ctx_queries),
        grid=(n_samples, vt.shape[0] // WV, nq),
        in_specs=in_specs,
        out_specs=pl.BlockSpec((tq, WV), lambda b, hp, t: (orow(b, t), hp)),
        out_shape=jax.ShapeDtypeStruct((out_rows, vt.shape[0]), BF16),
        compiler_params=_cparams(3, 56), name="mla_attn_ctx" if ctx_queries else "mla_attn",
    )(*args)


def _diff_attn_kernel(q_ref, *rest, with_latent, out_scale):
    if with_latent:
        kl_ref, vtl_ref, kc_ref, vtc_ref, lam_ref, g_ref, o_ref = rest
    else:
        kc_ref, vtc_ref, lam_ref, g_ref, o_ref = rest
    lp = lam_ref[...]
    lam = (jnp.exp(jnp.sum(lp[0:1] * lp[1:2], axis=-1, keepdims=True))
           - jnp.exp(jnp.sum(lp[2:3] * lp[3:4], axis=-1, keepdims=True)) + lam_ref[4:5, 0:1])
    d = HEAD_DIM
    k_refs = [kc_ref] + ([kl_ref] if with_latent else [])
    vt_refs = [vtc_ref] + ([vtl_ref] if with_latent else [])
    parts = []
    for half in range(2):
        sl = slice(half * d, (half + 1) * d)
        q = q_ref[:, sl]
        p_list, l = _softmax_parts_t([_dot_nt(k[:, sl], q) for k in k_refs])
        parts.append((p_list, 1.0 / l))
    (p1_list, inv1), (p2_list, inv2) = parts
    c2 = lam * inv2
    ot = functools.reduce(lambda a, b: a + b,
                          [_dot(vt[...], (p1 * inv1 - p2 * c2).astype(BF16))
                           for p1, p2, vt in zip(p1_list, p2_list, vt_refs)])
    o = jnp.transpose(ot)
    y = o * lax.rsqrt(jnp.mean(o * o, axis=-1, keepdims=True) + RMS_EPS) * g_ref[...]
    o_ref[...] = (y * out_scale).astype(BF16)


def diff_attention(dq, dk, vt, lam_rows, subln_g, lambda_init, n_samples, n_tok, n_ctx, ctx_queries):
    tq = min(512, n_ctx) if ctx_queries else 512
    W = 2 * HEAD_DIM
    nq, qrow, out_rows, orow = _query_rows(n_samples, n_tok, n_ctx, tq, ctx_queries)
    cblk = n_samples * n_tok // n_ctx
    in_specs = [pl.BlockSpec((tq, W), lambda b, h, t: (qrow(b, t), h))]
    args = [dq]
    if not ctx_queries:
        in_specs += [pl.BlockSpec((n_tok, W), lambda b, h, t: (b, h)),
                     pl.BlockSpec((W, n_tok), lambda b, h, t: (h, b))]
        args += [dk, vt]
    in_specs += [pl.BlockSpec((n_ctx, W), lambda b, h, t: (cblk + b, h)),
                 pl.BlockSpec((W, n_ctx), lambda b, h, t: (h, cblk + b)),
                 pl.BlockSpec((8, HEAD_DIM), lambda b, h, t: (0, 0)),
                 pl.BlockSpec((1, W), lambda b, h, t: (0, 0))]
    args += [dk, vt, lam_rows, subln_g.reshape(1, W)]
    n_heads = dk.shape[1] // W
    return pl.pallas_call(
        functools.partial(_diff_attn_kernel, with_latent=not ctx_queries, out_scale=1.0 - lambda_init),
        grid=(n_samples, n_heads, nq),
        in_specs=in_specs,
        out_specs=pl.BlockSpec((tq, W), lambda b, h, t: (orow(b, t), h)),
        out_shape=jax.ShapeDtypeStruct((out_rows, dk.shape[1]), BF16),
        compiler_params=_cparams(3, 56), name="diff_attn_ctx" if ctx_queries else "diff_attn",
    )(*args)


NA_DY = 2 * NA_KH - 1


def _na_bias_tables(rpb):
    W = GRID_W
    H = rpb.shape[0]
    qc = np.arange(W)[:, None]
    kc = np.arange(W)[None, :]
    cs = np.clip(qc - NA_KW // 2, 0, W - NA_KW)
    col_ok = (kc >= cs) & (kc < cs + NA_KW)
    dx = np.clip(kc - qc + NA_KW - 1, 0, 2 * NA_KW - 2)
    by_dx = jnp.take(rpb.astype(F32), jnp.asarray(dx.reshape(W * W)), axis=2).reshape(H, NA_DY, W, W)
    by_dx = jnp.where(jnp.asarray(col_ok)[None, None], by_dx, NEG_BIG)
    neg = jnp.full((H, 1, W, W), NEG_BIG, F32)
    padded = jnp.concatenate([neg, by_dx, neg, neg], axis=1)
    return jnp.concatenate([padded[:, :-1], padded[:, 1:]], axis=-1)


def _na_attn_kernel(q_ref, kl_ref, vl_ref, kc_ref, vc_ref, bias_ref, o_ref, *, scale, n_rows):
    rb = pl.program_id(2)
    r0 = jnp.clip(rb * 8 - 4, 0, n_rows - 16)
    start = pl.multiple_of(r0 * GRID_W, 256)
    kw = kl_ref[pl.ds(start, 16 * GRID_W), :]
    vw = vl_ref[pl.ds(start, 16 * GRID_W), :]
    q = q_ref[...]
    first, last_blk = rb == 0, rb == n_rows // 8 - 1
    dy0 = jnp.where(first, NA_KH - 1, jnp.where(last_blk, -1, NA_KH // 2 - 1))
    lane = lax.broadcasted_iota(jnp.int32, (GRID_W, 2 * GRID_W), 1)
    bias_rows = []
    for qa in range(8):
        win0 = jnp.where(first, max(qa - 4, 0), jnp.where(last_blk, min(qa + 4, 8), qa))
        tiles = []
        for kp in range(8):
            a1 = jnp.clip(2 * kp - qa + dy0 + 1, 0, NA_DY + 1)
            ok_even = jnp.logical_and(2 * kp >= win0, 2 * kp < win0 + NA_KH).astype(jnp.int32)
            ok_odd = jnp.logical_and(2 * kp + 1 >= win0, 2 * kp + 1 < win0 + NA_KH).astype(jnp.int32)
            ok = jnp.where(lane < GRID_W, ok_even, ok_odd)
            tiles.append(jnp.where(ok > 0, bias_ref[a1], NEG_BIG))
        bias_rows.append(jnp.concatenate(tiles, axis=1))
    s_lat = _dot_nt(q, kw) * scale + jnp.concatenate(bias_rows, axis=0)
    s_ctx = _dot_nt(q, kc_ref[...]) * scale
    (p_lat, p_ctx), l = _softmax_parts([s_lat, s_ctx])
    inv = 1.0 / l
    o = _dot((p_lat * inv).astype(BF16), vw) + _dot((p_ctx * inv).astype(BF16), vc_ref[...])
    o_ref[...] = o.astype(BF16)


def na_attention(z1, bias, n_samples, n_tok, n_ctx, n_heads):
    n_rows = n_tok // GRID_W
    assert n_rows >= 16 and n_rows % 8 == 0
    nrb = n_rows // 8
    tq = 8 * GRID_W
    d = HEAD_DIM
    cblk = n_samples * n_tok // n_ctx
    return pl.pallas_call(
        functools.partial(_na_attn_kernel, scale=d ** -0.5, n_rows=n_rows),
        grid=(n_samples, n_heads, nrb),
        in_specs=[pl.BlockSpec((tq, d), lambda b, h, rb: (b * nrb + rb, h)),
                  pl.BlockSpec((n_tok, d), lambda b, h, rb: (b, n_heads + h)),
                  pl.BlockSpec((n_tok, d), lambda b, h, rb: (b, 2 * n_heads + h)),
                  pl.BlockSpec((n_ctx, d), lambda b, h, rb: (cblk + b, n_heads + h)),
                  pl.BlockSpec((n_ctx, d), lambda b, h, rb: (cblk + b, 2 * n_heads + h)),
                  pl.BlockSpec((None, NA_DY + 2, GRID_W, 2 * GRID_W), lambda b, h, rb: (h, 0, 0, 0))],
        out_specs=pl.BlockSpec((tq, d), lambda b, h, rb: (b * nrb + rb, h)),
        out_shape=jax.ShapeDtypeStruct((n_samples * n_tok, n_heads * d), BF16),
        compiler_params=_cparams(3, 40), name="na_attn",
    )(z1, z1, z1, z1, z1, bias)


def _ctx_attn_kernel(q_ref, k_ref, v_ref, o_ref, *, scale):
    (p,), l = _softmax_parts([_dot_nt(q_ref[...], k_ref[...]) * scale])
    o_ref[...] = _dot((p * (1.0 / l)).astype(BF16), v_ref[...]).astype(BF16)


def na_ctx_attention(z1, n_samples, n_tok, n_ctx, n_heads):
    d = HEAD_DIM
    cblk = n_samples * n_tok // n_ctx
    return pl.pallas_call(
        functools.partial(_ctx_attn_kernel, scale=d ** -0.5),
        grid=(n_samples, n_heads),
        in_specs=[pl.BlockSpec((n_ctx, d), lambda b, h: (cblk + b, h)),
                  pl.BlockSpec((n_ctx, d), lambda b, h: (cblk + b, n_heads + h)),
                  pl.BlockSpec((n_ctx, d), lambda b, h: (cblk + b, 2 * n_heads + h))],
        out_specs=pl.BlockSpec((n_ctx, d), lambda b, h: (b, h)),
        out_shape=jax.ShapeDtypeStruct((n_samples * n_ctx, n_heads * d), BF16),
        compiler_params=_cparams(2, 16), name="na_attn_ctx",
    )(z1, z1, z1)


def _hgrn_chunk(q_raw, f_raw, v, lb, st_ref, reverse):
    C, HW = q_raw.shape
    dk = HEAD_DIM
    n_heads = HW // dk
    q = _silu(q_raw) * (dk ** -0.5)
    f = lb + (1.0 - lb) * jax.nn.sigmoid(f_raw)
    g = jnp.maximum(jnp.log(f) * LOG2E, -300.0)
    k = 1.0 - f
    t_idx = lax.broadcasted_iota(jnp.int32, (C, 1), 0)
    p = (C - 1 - t_idx) if reverse else t_idx

    def rotate_rows(x, s):
        if s % 8 == 0:
            return jnp.concatenate([x[C - s:], x[:C - s]], axis=0)
        return pltpu.roll(x, s, 0)

    def prev(x, s):
        return rotate_rows(x, (C - s) if reverse else s)

    def nxt(x, s):
        return rotate_rows(x, s if reverse else (C - s))

    def incl_scan(b):
        x = g
        pb = jnp.bitwise_and(p, b - 1)
        s = 1
        while s < b:
            x = x + jnp.where(pb >= s, prev(x, s), 0.0)
            s *= 2
        return x

    def excl_rscan(b):
        if b == 1:
            return jnp.zeros_like(g)
        pb = jnp.bitwise_and(p, b - 1)
        x = jnp.where(pb <= b - 2, nxt(g, 1), 0.0)
        s = 1
        while s < b:
            x = x + jnp.where(pb + s <= b - 1, nxt(x, s), 0.0)
            s *= 2
        return x

    cum = incl_scan(C)
    last = cum[0:1] if reverse else cum[C - 1:C]
    q_in = (q * jnp.exp2(cum)).astype(BF16)
    k_out = (k * jnp.exp2(last - cum)).astype(BF16)
    total = jnp.exp2(last)
    qb, kb, vb = q.astype(BF16), k.astype(BF16), v.astype(BF16)

    def boundary_rows(b):
        rows = []
        for j in range(C // (2 * b)):
            r = 2 * b * j + (b if reverse else b - 1)
            rows.append(jnp.broadcast_to(cum[r:r + 1], (2 * b, HW)))
        return jnp.concatenate(rows, axis=0)

    levels = []
    b = C // 2
    while b >= 1:
        upper = jnp.bitwise_and(p, b) != 0
        if b >= 8:
            x = cum - boundary_rows(b)
            e_up, e_lo = x, -x
        else:
            e_up, e_lo = incl_scan(b), excl_rscan(b)
        ql = jnp.where(upper, q * jnp.exp2(e_up), 0.0).astype(BF16)
        kl = jnp.where(upper, 0.0, k * jnp.exp2(e_lo)).astype(BF16)
        levels.append((b, ql, kl))
        b //= 2

    s_idx = lax.broadcasted_iota(jnp.int32, (1, C), 1)
    ps = (C - 1 - s_idx) if reverse else s_idx
    diag_w = (p == ps).astype(F32)
    pair_w = [(jnp.bitwise_and(p, -2 * b) == jnp.bitwise_and(ps, -2 * b)).astype(F32) for b, _, _ in levels]
    outs = []
    for h in range(n_heads):
        sl = slice(h * dk, (h + 1) * dk)
        st_old = st_ref[h]
        att = diag_w * _dot_nt(qb[:, sl], kb[:, sl])
        for (b, ql, kl), w in zip(levels, pair_w):
            att = att + w * _dot_nt(ql[:, sl], kl[:, sl])
        outs.append(_dot_nt(q_in[:, sl], st_old.astype(BF16)) + _dot(att.astype(BF16), vb[:, sl]))
        st_ref[h] = total[:, sl] * st_old + _dot_tn(vb[:, sl], k_out[:, sl])
    return jnp.concatenate(outs, axis=1)


def _hgrn_kernel(*refs, n_samples):
    ns = n_samples
    fwd_in, bwd_in = refs[:3 * ns], refs[3 * ns:6 * ns]
    lb_ref, of_ref, ob_ref, sf_ref, sb_ref = refs[6 * ns:]

    @pl.when(pl.program_id(0) == 0)
    def _():
        sf_ref[...] = jnp.zeros_like(sf_ref)
        sb_ref[...] = jnp.zeros_like(sb_ref)

    lb = jnp.concatenate([lb_ref[...]] * ns, axis=1)
    HW = lb_ref.shape[1]
    for ins, o_ref, st_ref, rev in ((fwd_in, of_ref, sf_ref, False), (bwd_in, ob_ref, sb_ref, True)):
        q, f, v = (jnp.concatenate([ins[3 * s + k][...] for s in range(ns)], axis=1) for k in range(3))
        o = _hgrn_chunk(q, f, v, lb, st_ref, reverse=rev)
        for s in range(ns):
            o_ref[s] = o[:, s * HW:(s + 1) * HW]


def hgrn_scan(z2, lb, n_samples, n_tok, n_ctx):
    HW = z2.shape[1] // 5
    C = HGRN_CHUNK
    ncc, ncl = n_ctx // C, n_tok // C
    lat_blocks = n_samples * ncl

    def fwd_row(b, c):
        return jnp.where(c < ncc, lat_blocks + b * ncc + c, b * ncl + (c - ncc))

    def bwd_row(b, c):
        return jnp.where(c < ncc, lat_blocks + b * ncc + (ncc - 1 - c), b * ncl + (ncl - 1 - (c - ncc)))

    spec = lambda rowf, b, col: pl.BlockSpec((C, HW), lambda c: (rowf(b, c), col))
    samples = range(n_samples)
    in_specs = ([spec(fwd_row, b, col) for b in samples for col in (0, 1, 3)]
                + [spec(bwd_row, b, col) for b in samples for col in (0, 2, 3)]
                + [pl.BlockSpec((1, HW), lambda c: (0, 0))])
    n_heads_all = n_samples * HW // HEAD_DIM
    bwd_local = lambda c: jnp.where(c < ncc, ncc - 1 - c, ncc + ncl - 1 - (c - ncc))
    return pl.pallas_call(
        functools.partial(_hgrn_kernel, n_samples=n_samples), grid=(ncc + ncl,),
        in_specs=in_specs,
        out_specs=[pl.BlockSpec((n_samples, C, HW), lambda c: (0, c, 0)),
                   pl.BlockSpec((n_samples, C, HW), lambda c: (0, bwd_local(c), 0))],
        out_shape=[jax.ShapeDtypeStruct((n_samples, n_ctx + n_tok, HW), F32)] * 2,
        scratch_shapes=[pltpu.VMEM((n_heads_all, HEAD_DIM, HEAD_DIM), F32)] * 2,
        compiler_params=_cparams(1, 56), name="hgrn_scan",
    )(*([z2] * (6 * n_samples)), lb.reshape(1, HW))


def _hgrn_readout_kernel(of_ref, ob_ref, gz_ref, g_ref, o_ref):
    for h in range(of_ref.shape[1] // HEAD_DIM):
        sl = slice(h * HEAD_DIM, (h + 1) * HEAD_DIM)
        x = of_ref[:, sl] + ob_ref[:, sl]
        y = x * lax.rsqrt(jnp.mean(x * x, axis=-1, keepdims=True) + RMS_EPS) * g_ref[...]
        o_ref[:, sl] = (y * _silu(gz_ref[:, sl])).astype(BF16)


def hgrn_readout(o_f, o_b, z2, norm_g, n_rows, n_tok, n_ctx):
    n_samples, _, HW = o_f.shape
    tm = math.gcd(n_ctx, ROW_TILE)
    lat_tiles, tiles_per_sample, ctx_tiles = n_samples * n_tok // tm, n_tok // tm, n_ctx // tm

    def o_idx(i):
        is_lat = i < lat_tiles
        jc = i - lat_tiles
        return (jnp.where(is_lat, i // tiles_per_sample, jc // ctx_tiles),
                jnp.where(is_lat, ctx_tiles + i % tiles_per_sample, jc % ctx_tiles), 0)

    return pl.pallas_call(
        _hgrn_readout_kernel, grid=(n_rows // tm,),
        in_specs=[pl.BlockSpec((None, tm, HW), o_idx), pl.BlockSpec((None, tm, HW), o_idx),
                  pl.BlockSpec((tm, HW), lambda i: (i, 4)), pl.BlockSpec((1, HEAD_DIM), lambda i: (0, 0))],
        out_specs=pl.BlockSpec((tm, HW), lambda i: (i, 0)),
        out_shape=jax.ShapeDtypeStruct((n_rows, HW), BF16),
        compiler_params=_cparams(1, 32), name="hgrn_readout",
    )(o_f, o_b, z2, norm_g.reshape(1, HEAD_DIM))


def _merge_kernel(h_ref, oa_ref, od_ref, om_ref, or_ref, wg_ref, bg_ref, wb_ref, o_ref, wg_bf, wb_bf):
    @pl.when(pl.program_id(1) == 0)
    def _():
        wg_bf[...] = wg_ref[...].astype(BF16)
        wb_bf[...] = wb_ref[...].astype(BF16)

    h = h_ref[...]
    acc = None
    for j, o_j in enumerate((oa_ref, od_ref, om_ref, or_ref)):
        gate = jax.nn.sigmoid(_dot(h, wg_bf[j]) + bg_ref[j])
        term = gate * _dot(o_j[...], wb_bf[j])
        acc = term if acc is None else acc + term
    o_ref[...] = acc.astype(BF16)


def merge_branches(h, outs, w_gate, b_gate, w_branch, layer, n_rows):
    D = h.shape[1]
    L, nb, BW = w_branch.shape[:3]
    tm, tn = ROW_TILE, 256
    single = dict(pipeline_mode=pl.Buffered(1))
    return pl.pallas_call(
        _merge_kernel, grid=(D // tn, n_rows // tm),
        in_specs=[pl.BlockSpec((tm, D), lambda j, i: (i, 0))]
                 + [pl.BlockSpec((tm, BW), lambda j, i: (i, 0))] * nb
                 + [pl.BlockSpec((None, nb, D, tn), lambda j, i: (layer, 0, 0, j), **single),
                    pl.BlockSpec((None, nb, 1, tn), lambda j, i: (layer, 0, 0, j)),
                    pl.BlockSpec((None, nb, BW, tn), lambda j, i: (layer, 0, 0, j), **single)],
        out_specs=pl.BlockSpec((tm, tn), lambda j, i: (i, j)),
        out_shape=jax.ShapeDtypeStruct((n_rows, D), BF16),
        scratch_shapes=[pltpu.VMEM((nb, D, tn), BF16), pltpu.VMEM((nb, BW, tn), BF16)],
        compiler_params=_cparams(2, 58), name="merge",
    )(h, *outs, w_gate, b_gate.reshape(L, nb, 1, D), w_branch)


def _ffn_in_kernel(x_ref, w1_ref, w3_ref, o_ref, w1_bf, w3_bf):
    @pl.when(pl.program_id(1) == 0)
    def _():
        w1_bf[...] = w1_ref[...].astype(BF16)
        w3_bf[...] = w3_ref[...].astype(BF16)

    x = x_ref[...]
    o_ref[...] = (_silu(_dot(x, w1_bf[...])) * _dot(x, w3_bf[...])).astype(BF16)


def ffn_in(h, w1, w3, layer, n_rows):
    _, D, Fd = w1.shape
    tm, tn = _even_row_tile(n_rows), 256
    w_spec = pl.BlockSpec((None, D, tn), lambda j, i: (layer, 0, j))
    return pl.pallas_call(
        _ffn_in_kernel,
        grid=(Fd // tn, pl.cdiv(n_rows, tm)),
        in_specs=[pl.BlockSpec((tm, D), lambda j, i: (i, 0)), w_spec, w_spec],
        out_specs=pl.BlockSpec((tm, tn), lambda j, i: (i, j)),
        out_shape=jax.ShapeDtypeStruct((n_rows, Fd), BF16),
        scratch_shapes=[pltpu.VMEM((D, tn), BF16)] * 2,
        compiler_params=_cparams(2, 50), name="ffn_in",
    )(h, w1, w3)


def _group_spec(tm, tn, n_lat, n_samples):
    return pl.BlockSpec((None, 1, tn), lambda j, i: (_grp_of_tile(i, tm, n_lat, n_samples), 0, j))


def ffn_out(u, w2, layer, xres, gate, n_rows, n_lat, n_samples):
    D = w2.shape[2]
    G = gate.shape[0]
    tm, tn = 256, 512
    return _mm_call(u, w2, layer, n_rows=n_rows, tn=tn, tm=tm, out_dtype=F32,
                    epilogue=_residual_epilogue, extras=(xres, gate.reshape(G, 1, D)),
                    extra_specs=(pl.BlockSpec((tm, tn), lambda j, i: (i, j)), _group_spec(tm, tn, n_lat, n_samples)),
                    name="ffn_out")


MOE_TILE = 256


def moe_route_plan(comb, n_experts):
    M = comb.shape[0]
    T = MOE_TILE
    w = comb[:, :n_experts]
    sel = w > 0
    n_tiles = 2 * M // T + n_experts
    R = n_tiles * T
    cnt = jnp.sum(sel, axis=0, dtype=jnp.int32)
    rank = jnp.cumsum(sel, axis=0, dtype=jnp.int32) - 1
    gsz = (cnt + T - 1) // T * T
    gend = jnp.cumsum(gsz)
    off = gend - gsz
    dest_all = off[None, :] + rank
    tile_start = jnp.arange(n_tiles, dtype=jnp.int32) * T
    tile_expert = jnp.minimum(jnp.searchsorted(gend, tile_start, side="right"), n_experts - 1).astype(jnp.int32)
    n_used = (gend[-1] // T).reshape(1).astype(jnp.int32)
    k_next = jnp.searchsorted(tile_expert, tile_expert, side="right")
    next_expert = jnp.where(k_next < n_tiles, tile_expert[jnp.minimum(k_next, n_tiles - 1)], -1).astype(jnp.int32)
    e_lo = jnp.argmax(sel, axis=1)
    e_hi = n_experts - 1 - jnp.argmax(sel[:, ::-1], axis=1)
    take = lambda a, e: jnp.take_along_axis(a, e[:, None], axis=1)[:, 0]
    two = e_hi != e_lo
    dest = jnp.stack([take(dest_all, e_lo), take(dest_all, e_hi)], axis=1).astype(jnp.int32)
    tok = lax.broadcasted_iota(jnp.int32, dest.shape, 0)
    src_tok = jnp.zeros((R,), jnp.int32).at[dest.reshape(-1)].set(tok.reshape(-1))
    pw = jnp.zeros((M, 128), F32).at[:, 0].set(take(w, e_lo)).at[:, 1].set(jnp.where(two, take(w, e_hi), 0.0))
    return src_tok, (tile_expert, next_expert, n_used), dest, pw


def _row_copy(src_hbm, row, dst_buf, slot, r, sem):
    return pltpu.make_async_copy(src_hbm.at[pl.ds(row, 1)], dst_buf.at[slot, pl.ds(r, 1)], sem.at[slot])


def _gather_tile(idx_of_row, src_hbm, buf, sem, n_rows_tile):
    i = pl.program_id(0)
    n = pl.num_programs(0)

    def start_tile(t, slot):
        def body(r2, carry):
            for pri in range(2):
                r = 2 * r2 + pri
                _row_copy(src_hbm, idx_of_row(t * n_rows_tile + r), buf, slot, r, sem).start(priority=pri)
            return carry
        lax.fori_loop(0, n_rows_tile // 2, body, 0, unroll=4)

    @pl.when(i == 0)
    def _():
        start_tile(0, 0)

    @pl.when(i + 1 < n)
    def _():
        start_tile(i + 1, (i + 1) % 2)

    slot = i % 2

    def wait_body(r, carry):
        _row_copy(src_hbm, 0, buf, slot, r, sem).wait()
        return carry
    lax.fori_loop(0, n_rows_tile, wait_body, 0, unroll=8)
    return slot


def _moe_gather_kernel(src_ref, h_hbm, o_ref, buf, sem):
    slot = _gather_tile(lambda r: src_ref[r], h_hbm, buf, sem, o_ref.shape[0])
    o_ref[...] = buf[slot].astype(BF16)


def moe_gather(hf, src_tok):
    R = src_tok.shape[0]
    D = hf.shape[1]
    T = MOE_TILE
    return pl.pallas_call(
        _moe_gather_kernel,
        grid_spec=pltpu.PrefetchScalarGridSpec(
            num_scalar_prefetch=1, grid=(R // T,),
            in_specs=[pl.BlockSpec(memory_space=pl.ANY)],
            out_specs=pl.BlockSpec((T, D), lambda i, src: (i, 0)),
            scratch_shapes=[pltpu.VMEM((2, T, D), F32), pltpu.SemaphoreType.DMA((2,))]),
        out_shape=jax.ShapeDtypeStruct((R, D), BF16),
        compiler_params=_cparams(1, 24), name="moe_gather",
    )(src_tok, hf)


def _moe_mm_kernel(te_ref, nx_ref, nu_ref, x_ref, *rest, e0, n_w):
    w_hbms = rest[:n_w]
    o_ref = rest[n_w]
    stages, wbfs, sem = rest[n_w + 1:2 * n_w + 1], rest[2 * n_w + 1:3 * n_w + 1], rest[3 * n_w + 1]
    j, i, nj = pl.program_id(0), pl.program_id(1), pl.num_programs(0)
    tn = stages[0].shape[1]

    def block_copies(e, jj):
        cols = pl.ds(pl.multiple_of(jj * tn, tn), tn)
        return [pltpu.make_async_copy(w.at[e0 + e, :, cols], st, sem.at[k])
                for k, (w, st) in enumerate(zip(w_hbms, stages))]

    fresh = jnp.logical_or(i == 0, te_ref[i] != te_ref[jnp.maximum(i - 1, 0)])

    @pl.when(fresh)
    def _():
        @pl.when(jnp.logical_and(i == 0, j == 0))
        def _():
            for cp in block_copies(te_ref[0], 0):
                cp.start()

        for cp in block_copies(te_ref[i], j):
            cp.wait()
        for st, wbf in zip(stages, wbfs):
            _cast_rows(st, wbf)
        nxt = nx_ref[i]

        @pl.when(nxt >= 0)
        def _():
            for cp in block_copies(nxt, j):
                cp.start()

        @pl.when(jnp.logical_and(nxt < 0, j + 1 < nj))
        def _():
            for cp in block_copies(te_ref[0], j + 1):
                cp.start()

    @pl.when(i < nu_ref[0])
    def _():
        x = x_ref[...]
        if n_w == 2:
            o_ref[...] = (_silu(_dot(x, wbfs[0][...])) * _dot(x, wbfs[1][...])).astype(o_ref.dtype)
        else:
            o_ref[...] = _dot(x, wbfs[0][...]).astype(o_ref.dtype)

    @pl.when(i >= nu_ref[0])
    def _():
        o_ref[...] = jnp.zeros_like(o_ref)


def _moe_mm(x, ws, e0, plan, tn, out_dtype, name):
    tile_expert, next_expert, n_used = plan
    R, K = x.shape
    N = ws[0].shape[2]
    T = MOE_TILE
    n_w = len(ws)
    return pl.pallas_call(
        functools.partial(_moe_mm_kernel, e0=e0, n_w=n_w),
        grid_spec=pltpu.PrefetchScalarGridSpec(
            num_scalar_prefetch=3, grid=(N // tn, R // T),
            in_specs=[pl.BlockSpec((T, K), lambda j, i, te, nx, nu: (i, 0))]
                     + [pl.BlockSpec(memory_space=pl.ANY)] * n_w,
            out_specs=pl.BlockSpec((T, tn), lambda j, i, te, nx, nu: (i, j)),
            scratch_shapes=[pltpu.VMEM((K, tn), F32)] * n_w + [pltpu.VMEM((K, tn), BF16)] * n_w
                           + [pltpu.SemaphoreType.DMA((n_w,))]),
        out_shape=jax.ShapeDtypeStruct((R, N), out_dtype),
        compiler_params=_cparams(2, 48), name=name,
    )(tile_expert, next_expert, n_used, x, *ws)


def moe_ffn_in(xs, w1, w3, e0, plan):
    return _moe_mm(xs, (w1, w3), e0, plan, 512, BF16, "moe_ffn_in")


def moe_ffn_out(u, w2, e0, plan):
    return _moe_mm(u, (w2,), e0, plan, 2048, F32, "moe_ffn_out")


def _moe_combine_kernel(dest_ref, y_hbm, pw_ref, x_ref, g_ref, *rest, with_norm):
    if with_norm:
        ng_ref, o_ref, buf, sem = rest
    else:
        o_ref, buf, sem = rest
    T = o_ref.shape[0]
    slot = _gather_tile(lambda r: dest_ref[r], y_hbm, buf, sem, 2 * T)
    pw = pw_ref[...]
    mix = pw[:, 0:1] * buf[slot, 0:T, :] + pw[:, 1:2] * buf[slot, T:2 * T, :]
    x = x_ref[...] + g_ref[...] * mix
    if with_norm:
        x = x * lax.rsqrt(jnp.mean(x * x, axis=-1, keepdims=True) + RMS_EPS) * ng_ref[...]
    o_ref[...] = x


def moe_combine(y, dest, pw, xres, gate, n_lat, n_samples, norm_g=None):
    M, D = xres.shape
    G = gate.shape[0]
    T = 128
    dest = dest.reshape(M // T, T, 2).transpose(0, 2, 1)
    in_specs = [pl.BlockSpec(memory_space=pl.ANY),
                pl.BlockSpec((T, 128), lambda i, d: (i, 0)),
                pl.BlockSpec((T, D), lambda i, d: (i, 0)),
                pl.BlockSpec((None, 1, D), lambda i, d: (_grp_of_tile(i, T, n_lat, n_samples), 0, 0))]
    args = [dest.reshape(-1), y, pw, xres, gate.reshape(G, 1, D)]
    if norm_g is not None:
        in_specs.append(pl.BlockSpec((1, D), lambda i, d: (0, 0)))
        args.append(norm_g.reshape(1, D))
    return pl.pallas_call(
        functools.partial(_moe_combine_kernel, with_norm=norm_g is not None),
        grid_spec=pltpu.PrefetchScalarGridSpec(
            num_scalar_prefetch=1, grid=(M // T,),
            in_specs=in_specs,
            out_specs=pl.BlockSpec((T, D), lambda i, d: (i, 0)),
            scratch_shapes=[pltpu.VMEM((2, 2 * T, D), F32), pltpu.SemaphoreType.DMA((2,))]),
        out_shape=jax.ShapeDtypeStruct((M, D), F32),
        compiler_params=_cparams(1, 32), name="moe_combine",
    )(*args)


def kernel(x, c, ctx, c_ctx, norm1_g, norm2_g, w_ada, b_ada, w_in, na_rpb, diff_lambda, diff_subln_g,
           mla_q_norm_g, mla_w_q_up, mla_kv_norm_g, mla_w_kv_up, hgrn_lower_bounds, hgrn_norm_g,
           w_branch, w_gate, b_gate, w_out, ffn_w1, ffn_w3, ffn_w2, moe_router, moe_w1, moe_w3, moe_w2,
           final_norm_g):
    B, N, D = x.shape
    NC = ctx.shape[1]
    L = w_ada.shape[0]
    n_lat, n_all = B * N, B * N + B * NC
    BW = w_branch.shape[2]
    n_heads = BW // HEAD_DIM
    HG0 = w_in.shape[2] - 5 * BW
    assert N % ROW_TILE == 0 and (B * NC) % ROW_TILE == 0 and NC % HGRN_CHUNK == 0

    xs = jnp.concatenate([x.reshape(n_lat, D), ctx.reshape(B * NC, D)], axis=0)

    cond8 = jnp.zeros((8, D), F32).at[:B].set(c).at[B].set(c_ctx)
    mods = adaln_all(cond8, w_ada, b_ada)[:, :B + 1].reshape(L, B + 1, 6, D)

    lb_all = jnp.cumsum(jax.nn.softmax(hgrn_lower_bounds.astype(F32), axis=0), axis=0)
    lb_all = lb_all - lb_all[0:1]

    w_in_t = jnp.swapaxes(w_in, 1, 2)

    cosd, sind = _rope_tables(N, B * NC, 2 * 64, ROW_TILE)
    cosm, sinm = _rope_tables(N, B * NC, 64, ROW_TILE)

    RQ, RKV = mla_w_q_up.shape[1], mla_w_kv_up.shape[1]
    wq4 = mla_w_q_up.reshape(L, RQ, n_heads, HEAD_DIM + MLA_ROPE)
    wq_cat = jnp.pad(wq4, ((0, 0), (0, 0), (0, 0), (0, MLA_SLOT - HEAD_DIM - MLA_ROPE))).reshape(L, RQ, -1)
    wkv4 = mla_w_kv_up.reshape(L, RKV, n_heads, 2 * HEAD_DIM)
    wkn = wkv4[..., :HEAD_DIM].reshape(L, RKV, -1)
    wvt = jnp.swapaxes(wkv4[..., HEAD_DIM:].reshape(L, RKV, -1), 1, 2)

    for l in range(L):
        need_ctx = l < L - 1
        n_rows = n_all if need_ctx else n_lat
        lambda_init = 0.8 - 0.6 * math.exp(-0.3 * l)
        sh1, sc1, g1, sh2, sc2, g2 = (mods[l, :, k] for k in range(6))

        h = norm_mod(xs, norm1_g[l], sh1, sc1, N, B)
        z1 = wt_stream_matmul(h, w_in_t, l, 0, -(-HG0 // 512) * 512, BF16, "w_in_attn")
        z2 = wt_stream_matmul(h, w_in_t, l, HG0, 5 * BW, F32, "w_in_hgrn")

        bias = _na_bias_tables(na_rpb[l])
        o_a = na_attention(z1, bias, B, N, NC, n_heads)
        dq, dk, dvt = diff_rope(z1, cosd, sind, N, n_lat, BW)
        lam_rows = jnp.zeros((8, HEAD_DIM), F32).at[:4].set(diff_lambda[l]).at[4].set(lambda_init)
        o_d = diff_attention(dq, dk, dvt, lam_rows, diff_subln_g[l], lambda_init, B, N, NC, False)
        qcat, kcat, mvt = mla_prep(z1, cosm, sinm, mla_q_norm_g[l], wq_cat[l], mla_kv_norm_g[l], wkn[l], wvt[l],
                                   N, n_lat, 6 * BW)
        o_m = mla_attention(qcat, kcat, mvt, B, N, NC, False)
        o_f, o_b = hgrn_scan(z2, lb_all[l], B, N, NC)
        o_r = hgrn_readout(o_f, o_b, z2, hgrn_norm_g[l], n_rows, N, NC)
        if need_ctx:
            o_a = jnp.concatenate([o_a, na_ctx_attention(z1, B, N, NC, n_heads)], axis=0)
            o_d = jnp.concatenate([o_d, diff_attention(dq, dk, dvt, lam_rows, diff_subln_g[l], lambda_init,
                                                       B, N, NC, True)], axis=0)
            o_m = jnp.concatenate([o_m, mla_attention(qcat, kcat, mvt, B, N, NC, True)], axis=0)

        s = merge_branches(h, (o_a, o_d, o_m, o_r), w_gate, b_gate, w_branch, l, n_rows)
        G = B + 1
        tmo = 2 * ROW_TILE
        xs = _mm_call(s, w_out, l, n_rows=n_rows, tn=512, tm=tmo, out_dtype=F32,
                      epilogue=_residual_epilogue, extras=(xs, g1.reshape(G, 1, D)),
                      extra_specs=(pl.BlockSpec((tmo, 512), lambda j, i: (i, j)), _group_spec(tmo, 512, N, B)),
                      name="w_out")

        j = l // 2
        if l % 2 == 0:
            h2 = norm_mod(xs, norm2_g[l], sh2, sc2, N, B)
            u = ffn_in(h2, ffn_w1, ffn_w3, j, n_rows)
            xs = ffn_out(u, ffn_w2, j, xs, g2, n_rows, N, B)
        else:
            hf, comb = norm_mod(xs, norm2_g[l], sh2, sc2, N, B, w_router=moe_router[j])
            NM, E, _, FE = moe_w1.shape
            src_tok, plan, dest, pw = moe_route_plan(comb, E)
            xg = moe_gather(hf, src_tok)
            u = moe_ffn_in(xg, moe_w1.reshape(NM * E, D, FE), moe_w3.reshape(NM * E, D, FE), j * E, plan)
            y = moe_ffn_out(u, moe_w2.reshape(NM * E, FE, D), j * E, plan)
            xs = moe_combine(y, dest, pw, xs, g2, N, B, norm_g=final_norm_g if l == L - 1 else None)

    if L % 2 == 1:
        xs = final_norm(xs[:n_lat], final_norm_g)
    return xs.reshape(B, N, D)
```

```python
import functools
import math

import numpy as np
import jax
import jax.numpy as jnp
from jax import lax
from jax.experimental import pallas as pl
from jax.experimental.pallas import tpu as pltpu

F32 = jnp.float32
BF16 = jnp.bfloat16

GRID_W = 64
RMS_EPS = 1e-6
ROPE_BASE = 10000.0
NA_KH, NA_KW = 8, 16
HEAD_DIM = 128
MLA_ROPE = 64
HGRN_CHUNK = 64
NEG_BIG = -1e30

VMEM_LIMIT_V7X = 60000 * 1024
ROW_TILE = 512


def _even_row_tile(n_rows, cap=1152):
    return next(t for t in range(cap, 15, -16) if n_rows % t == 0)


def _cparams(n_axes, vmem_mb):
    return pltpu.CompilerParams(dimension_semantics=("arbitrary",) * n_axes,
                                vmem_limit_bytes=min(int(vmem_mb * 2**20), VMEM_LIMIT_V7X))


def _silu(x):
    return x * jax.nn.sigmoid(x)


def _dot(a, b):
    return jnp.dot(a, b, preferred_element_type=F32)


def _dot_nt(a, b):
    return lax.dot_general(a, b, (((1,), (1,)), ((), ())), preferred_element_type=F32)


def _dot_tn(a, b):
    return lax.dot_general(a, b, (((0,), (0,)), ((), ())), preferred_element_type=F32)


def _adaln_kernel(c_ref, w_ref, b_ref, o_ref):
    @pl.when(pl.program_id(1) == 0)
    def _():
        o_ref[...] = jnp.broadcast_to(b_ref[...], o_ref.shape)

    x = _silu(c_ref[...]).astype(BF16)
    o_ref[...] += _dot(x, w_ref[...].astype(BF16))


def adaln_all(cond8, w_ada, b_ada):
    L, D, N6 = w_ada.shape
    tk = 128
    cond_k = cond8.reshape(8, D // tk, tk).transpose(1, 0, 2)
    return pl.pallas_call(
        _adaln_kernel,
        grid=(L, D // tk),
        in_specs=[pl.BlockSpec((None, 8, tk), lambda l, k: (k, 0, 0)),
                  pl.BlockSpec((None, tk, N6), lambda l, k: (l, k, 0)),
                  pl.BlockSpec((None, 1, N6), lambda l, k: (l, 0, 0))],
        out_specs=pl.BlockSpec((None, 8, N6), lambda l, k: (l, 0, 0)),
        out_shape=jax.ShapeDtypeStruct((L, 8, N6), F32),
        compiler_params=_cparams(2, 48),
        name="adaln",
    )(cond_k, w_ada, b_ada.reshape(L, 1, N6))


def _norm_mod_kernel(x_ref, *rest, n_experts, n_lat_tiles):
    with_router = n_experts is not None
    if n_lat_tiles is None:
        x = x_ref[...]
    else:
        xc_ref, *rest = rest
        x = jnp.where(pl.program_id(0) < n_lat_tiles, x_ref[...], xc_ref[...])
    g_ref, sh_ref, sc_ref, *rest = rest
    y = x * lax.rsqrt(jnp.mean(x * x, axis=-1, keepdims=True) + RMS_EPS) * g_ref[...]
    h = y * (1.0 + sc_ref[...]) + sh_ref[...]
    if not with_router:
        (h_ref,) = rest
        h_ref[...] = h.astype(BF16)
        return
    wr_ref, h_ref, comb_ref = rest
    h_ref[...] = h
    logits = jnp.dot(h, wr_ref[...], precision=lax.Precision.HIGHEST, preferred_element_type=F32)
    lane = lax.broadcasted_iota(jnp.int32, logits.shape, 1).astype(F32)
    logits = jnp.where(lane < n_experts, logits, -jnp.inf)
    m1 = jnp.max(logits, axis=-1, keepdims=True)
    i1 = jnp.min(jnp.where(logits == m1, lane, 128.0), axis=-1, keepdims=True)
    rest_l = jnp.where(lane == i1, -jnp.inf, logits)
    m2 = jnp.max(rest_l, axis=-1, keepdims=True)
    i2 = jnp.min(jnp.where(rest_l == m2, lane, 128.0), axis=-1, keepdims=True)
    e2 = jnp.exp(m2 - m1)
    p1 = 1.0 / (1.0 + e2)
    p2 = e2 / (1.0 + e2)
    comb_ref[...] = jnp.where(lane == i1, p1, 0.0) + jnp.where(lane == i2, p2, 0.0)


def _grp_of_tile(i, tm, lat_rows_per_sample, n_samples):
    return jnp.minimum((i * tm) // lat_rows_per_sample, n_samples)


def norm_mod(x, g, shift, scale, n_lat, n_samples, w_router=None, x_ctx=None):
    D = x.shape[1]
    tm = 256
    G = shift.shape[0]
    grp = lambda i: (_grp_of_tile(i, tm, n_lat, n_samples), 0, 0)
    if x_ctx is None:
        M, nl = x.shape[0], None
        in_specs, args = [pl.BlockSpec((tm, D), lambda i: (i, 0))], [x]
    else:
        M, nl = x.shape[0] + x_ctx.shape[0], x.shape[0] // tm
        in_specs = [pl.BlockSpec((tm, D), lambda i: (jnp.minimum(i, nl - 1), 0)),
                    pl.BlockSpec((tm, D), lambda i: (jnp.maximum(i - nl, 0), 0))]
        args = [x, x_ctx]
    in_specs += [pl.BlockSpec((1, D), lambda i: (0, 0)),
                 pl.BlockSpec((None, 1, D), grp),
                 pl.BlockSpec((None, 1, D), grp)]
    args += [g.reshape(1, D), shift.reshape(G, 1, D), scale.reshape(G, 1, D)]
    out_specs = [pl.BlockSpec((tm, D), lambda i: (i, 0))]
    out_shape = [jax.ShapeDtypeStruct((M, D), BF16 if w_router is None else F32)]
    if w_router is not None:
        wr = jnp.zeros((D, 128), F32).at[:, :w_router.shape[1]].set(w_router)
        in_specs.append(pl.BlockSpec((D, 128), lambda i: (0, 0)))
        args.append(wr)
        out_specs.append(pl.BlockSpec((tm, 128), lambda i: (i, 0)))
        out_shape.append(jax.ShapeDtypeStruct((M, 128), F32))
    outs = pl.pallas_call(
        functools.partial(_norm_mod_kernel, n_experts=None if w_router is None else w_router.shape[1],
                          n_lat_tiles=nl),
        grid=(M // tm,),
        in_specs=in_specs, out_specs=out_specs, out_shape=out_shape,
        compiler_params=_cparams(1, 40),
        name="norm_mod_router" if w_router is not None else "norm_mod",
    )(*args)
    return outs if w_router is not None else outs[0]


def _final_norm_kernel(x_ref, g_ref, o_ref):
    x = x_ref[...]
    o_ref[...] = x * lax.rsqrt(jnp.mean(x * x, axis=-1, keepdims=True) + RMS_EPS) * g_ref[...]


def final_norm(x, g):
    M, D = x.shape
    tm = 256
    return pl.pallas_call(
        _final_norm_kernel, grid=(M // tm,),
        in_specs=[pl.BlockSpec((tm, D), lambda i: (i, 0)), pl.BlockSpec((1, D), lambda i: (0, 0))],
        out_specs=pl.BlockSpec((tm, D), lambda i: (i, 0)),
        out_shape=jax.ShapeDtypeStruct((M, D), F32),
        compiler_params=_cparams(1, 40), name="final_norm",
    )(x, g.reshape(1, D))


def _cast_rows(src_ref, dst_ref, rows=512):
    n = src_ref.shape[0]
    for r0 in range(0, n, rows):
        r1 = min(r0 + rows, n)
        dst_ref[r0:r1, :] = src_ref[r0:r1, :].astype(BF16)


def _stream_mm_kernel(x_ref, w_hbm, *rest, layer, n_extra, epilogue):
    extras = rest[:n_extra]
    o_ref, stage, wbf, sem = rest[n_extra:]
    j, i, nj = pl.program_id(0), pl.program_id(1), pl.num_programs(0)
    tn = stage.shape[1]

    def block_copy(jj):
        return pltpu.make_async_copy(w_hbm.at[layer, :, pl.ds(pl.multiple_of(jj * tn, tn), tn)], stage, sem)

    @pl.when(i == 0)
    def _():
        @pl.when(j == 0)
        def _():
            block_copy(0).start()

        block_copy(j).wait()
        _cast_rows(stage, wbf)

        @pl.when(j + 1 < nj)
        def _():
            block_copy(j + 1).start()

    acc = _dot(x_ref[...], wbf[...])
    if epilogue is not None:
        acc = epilogue(acc, *[e[...] for e in extras])
    o_ref[...] = acc.astype(o_ref.dtype)


def _mm_call(x, w, layer, *, n_rows, tn, tm, out_dtype, epilogue=None, extras=(), extra_specs=(), name="mm"):
    K, N = w.shape[1:]
    in_specs = [pl.BlockSpec((tm, K), lambda j, i: (i, 0)), pl.BlockSpec(memory_space=pl.ANY)]
    in_specs += list(extra_specs)
    out_bytes = jnp.dtype(out_dtype).itemsize
    vmem = (2 * tm * K * 2 + K * tn * 4 + K * tn * 2 + 2 * tm * tn * out_bytes + 3 * tm * tn * 4
            + len(extras) * 2 * tm * tn * 4) / 2**20 + 6
    return pl.pallas_call(
        functools.partial(_stream_mm_kernel, layer=layer, n_extra=len(extras), epilogue=epilogue),
        grid=(N // tn, pl.cdiv(n_rows, tm)),
        in_specs=in_specs,
        out_specs=pl.BlockSpec((tm, tn), lambda j, i: (i, j)),
        out_shape=jax.ShapeDtypeStruct((n_rows, N), out_dtype),
        scratch_shapes=[pltpu.VMEM((K, tn), F32), pltpu.VMEM((K, tn), BF16), pltpu.SemaphoreType.DMA(())],
        compiler_params=_cparams(2, vmem),
        name=name,
    )(x, w, *extras)


def _residual_epilogue(acc, xres, gate):
    return xres + gate * acc


def _two_source_residual(n_lat_tiles):
    def epilogue(acc, x_lat, x_ctx, gate):
        return jnp.where(pl.program_id(1) < n_lat_tiles, x_lat, x_ctx) + gate * acc
    return epilogue


def _wt_stream_mm_kernel(x_ref, wt_hbm, o_ref, stage, wbf, sem, *, layer, row0):
    j, i, nj = pl.program_id(0), pl.program_id(1), pl.num_programs(0)
    tn, K = stage.shape

    def block_copy(jj):
        return pltpu.make_async_copy(wt_hbm.at[layer, pl.ds(row0 + jj * tn, tn), :], stage, sem)

    @pl.when(i == 0)
    def _():
        @pl.when(j == 0)
        def _():
            block_copy(0).start()

        block_copy(j).wait()
        for c in range(0, K, tn):
            wbf[c:c + tn, :] = jnp.transpose(stage[:, c:c + tn]).astype(BF16)

        @pl.when(j + 1 < nj)
        def _():
            block_copy(j + 1).start()

    o_ref[...] = _dot(x_ref[...], wbf[...]).astype(o_ref.dtype)


def wt_stream_matmul(x, wt, layer, row0, n_cols, out_dtype, name):
    M, K = x.shape
    tm, tn = _even_row_tile(M), 512
    assert n_cols % tn == 0 and K % tn == 0 and row0 % 8 == 0
    return pl.pallas_call(
        functools.partial(_wt_stream_mm_kernel, layer=layer, row0=row0),
        grid=(n_cols // tn, pl.cdiv(M, tm)),
        in_specs=[pl.BlockSpec((tm, K), lambda j, i: (i, 0)), pl.BlockSpec(memory_space=pl.ANY)],
        out_specs=pl.BlockSpec((tm, tn), lambda j, i: (i, j)),
        out_shape=jax.ShapeDtypeStruct((M, n_cols), out_dtype),
        scratch_shapes=[pltpu.VMEM((tn, K), F32), pltpu.VMEM((K, tn), BF16), pltpu.SemaphoreType.DMA(())],
        compiler_params=_cparams(2, 48), name=name,
    )(x, wt)


def _rope_tables(n_tokens, n_ident, group, tile_rows):
    half = group // 2
    q = half // 2
    pos = np.arange(n_tokens)
    row, col = pos // GRID_W, pos % GRID_W
    freqs = ROPE_BASE ** (-(np.arange(q, dtype=np.float64) / q))
    lane = np.arange(128)
    in_group = lane % group
    axis_pos = np.where((in_group < half)[None, :], row[:, None], col[:, None]).astype(np.float64)
    ang = axis_pos * freqs[(in_group % half) % q][None, :]
    sign = np.where((in_group % half) < q, -1.0, 1.0)[None, :]
    cos, sin = np.cos(ang), np.sin(ang) * sign
    n_pad = -(-n_ident // tile_rows) * tile_rows
    cos = np.concatenate([cos, np.ones((n_pad, 128))], axis=0)
    sin = np.concatenate([sin, np.zeros((n_pad, 128))], axis=0)
    return jnp.asarray(cos, F32), jnp.asarray(sin, F32)


def _rope_apply(x, cos, sin, q):
    lane = lax.broadcasted_iota(jnp.int32, x.shape, 1)
    partner = jnp.where((lane % (2 * q)) < q, pltpu.roll(x, 128 - q, 1), pltpu.roll(x, q, 1))
    return x * cos + partner * sin


LOG2E = math.log2(math.e)


def _diff_rope_kernel(q_ref, k_ref, v_ref, cos_ref, sin_ref, qo_ref, ko_ref, vt_ref, *, q_scale):
    vt_ref[...] = jnp.transpose(v_ref[...].astype(F32)).astype(BF16)
    cos, sin = cos_ref[...], sin_ref[...]
    for src, dst, mult in ((q_ref, qo_ref, q_scale), (k_ref, ko_ref, None)):
        for c in range(src.shape[1] // 128):
            sl = slice(c * 128, (c + 1) * 128)
            y = _rope_apply(src[:, sl].astype(F32), cos, sin, 32)
            dst[:, sl] = (y if mult is None else y * mult).astype(BF16)


def _table_block(i, tm, n_tok, n_lat_total):
    return jnp.where(i * tm < n_lat_total, (i * tm % n_tok) // tm, n_tok // tm)


def diff_rope(z1, cosd, sind, n_tok, n_lat_total, W):
    M = z1.shape[0]
    tm = ROW_TILE
    tb = lambda i: (_table_block(i, tm, n_tok, n_lat_total), 0)
    return pl.pallas_call(
        functools.partial(_diff_rope_kernel, q_scale=HEAD_DIM ** -0.5 * LOG2E), grid=(M // tm,),
        in_specs=[pl.BlockSpec((tm, W), lambda i: (i, 3)), pl.BlockSpec((tm, W), lambda i: (i, 4)),
                  pl.BlockSpec((tm, W), lambda i: (i, 5)),
                  pl.BlockSpec((tm, 128), tb), pl.BlockSpec((tm, 128), tb)],
        out_specs=[pl.BlockSpec((tm, W), lambda i: (i, 0))] * 2 + [pl.BlockSpec((W, tm), lambda i: (0, i))],
        out_shape=[jax.ShapeDtypeStruct((M, W), BF16)] * 2 + [jax.ShapeDtypeStruct((W, M), BF16)],
        compiler_params=_cparams(1, 40), name="diff_rope",
    )(z1, z1, z1, cosd, sind)


MLA_SLOT = 2 * HEAD_DIM


def _mla_prep_kernel(cq_ref, ckva_ref, ckvb_ref, kr_ref, cos_ref, sin_ref, gq_ref, gkv_ref, wq_ref, wkn_ref,
                     wvt_ref, qcat_ref, kcat_ref, vt_ref, wq_bf, wkn_bf, wvt_bf, *, q_scale):
    @pl.when(pl.program_id(0) == 0)
    def _():
        wq_bf[...] = wq_ref[...].astype(BF16)
        wkn_bf[...] = wkn_ref[...].astype(BF16)
        wvt_bf[...] = wvt_ref[...].astype(BF16)

    cos, sin = cos_ref[...], sin_ref[...]
    d = HEAD_DIM
    n_heads = qcat_ref.shape[1] // MLA_SLOT
    cq = cq_ref[...].astype(F32)
    cqn = (cq * lax.rsqrt(jnp.mean(cq * cq, axis=-1, keepdims=True) + RMS_EPS) * gq_ref[...]).astype(BF16)
    q = _dot(cqn, wq_bf[...])
    a = ckva_ref[...].astype(F32)
    b = ckvb_ref[...].astype(F32)
    ha = a.shape[1]
    ms = (jnp.sum(a * a, axis=-1, keepdims=True) + jnp.sum(b * b, axis=-1, keepdims=True)) / (2 * ha)
    r = lax.rsqrt(ms + RMS_EPS)
    g = gkv_ref[...]
    ckvn = jnp.concatenate([(a * r * g[:, :ha]).astype(BF16), (b * r * g[:, ha:]).astype(BF16)], axis=1)
    kn = _dot(ckvn, wkn_bf[...])
    lane = lax.broadcasted_iota(jnp.int32, cos.shape, 1)
    kr = jnp.where(lane < MLA_ROPE, _rope_apply(kr_ref[...].astype(F32), cos, sin, 16), 0.0).astype(BF16)
    for h in range(n_heads):
        c0 = h * MLA_SLOT
        qcat_ref[:, c0:c0 + d] = (q[:, c0:c0 + d] * q_scale).astype(BF16)
        qcat_ref[:, c0 + d:c0 + 2 * d] = (_rope_apply(q[:, c0 + d:c0 + 2 * d], cos, sin, 16) * q_scale).astype(BF16)
        kcat_ref[:, c0:c0 + d] = kn[:, h * d:(h + 1) * d].astype(BF16)
        kcat_ref[:, c0 + d:c0 + 2 * d] = kr
    vt_ref[...] = _dot_nt(wvt_bf[...], ckvn).astype(BF16)


def mla_prep(z1, cosm, sinm, gq, wq_cat, gkv, wkn, wvt, n_tok, n_lat_total, col0):
    M = z1.shape[0]
    tm = ROW_TILE
    RQ, NQ = wq_cat.shape
    RKV, NKN = wkn.shape
    c_kv, c_kr = col0 + RQ, col0 + RQ + RKV
    assert col0 % RQ == 0 and c_kv % (RKV // 2) == 0 and c_kr % 128 == 0
    tb = lambda i: (_table_block(i, tm, n_tok, n_lat_total), 0)
    const = lambda i: (0, 0)
    return pl.pallas_call(
        functools.partial(_mla_prep_kernel, q_scale=(HEAD_DIM + MLA_ROPE) ** -0.5 * LOG2E), grid=(M // tm,),
        in_specs=[pl.BlockSpec((tm, RQ), lambda i: (i, col0 // RQ)),
                  pl.BlockSpec((tm, RKV // 2), lambda i: (i, c_kv // (RKV // 2))),
                  pl.BlockSpec((tm, RKV // 2), lambda i: (i, c_kv // (RKV // 2) + 1)),
                  pl.BlockSpec((tm, 128), lambda i: (i, c_kr // 128)),
                  pl.BlockSpec((tm, 128), tb), pl.BlockSpec((tm, 128), tb),
                  pl.BlockSpec((1, RQ), const), pl.BlockSpec((1, RKV), const),
                  pl.BlockSpec((RQ, NQ), const), pl.BlockSpec((RKV, NKN), const), pl.BlockSpec((NKN, RKV), const)],
        out_specs=[pl.BlockSpec((tm, NQ), lambda i: (i, 0)),
                   pl.BlockSpec((tm, NQ), lambda i: (i, 0)),
                   pl.BlockSpec((NKN, tm), lambda i: (0, i))],
        out_shape=[jax.ShapeDtypeStruct((M, NQ), BF16), jax.ShapeDtypeStruct((M, NQ), BF16),
                   jax.ShapeDtypeStruct((NKN, M), BF16)],
        scratch_shapes=[pltpu.VMEM((RQ, NQ), BF16), pltpu.VMEM((RKV, NKN), BF16), pltpu.VMEM((NKN, RKV), BF16)],
        compiler_params=_cparams(1, 56), name="mla_prep",
    )(z1, z1, z1, z1, cosm, sinm, gq.reshape(1, RQ), gkv.reshape(1, RKV), wq_cat, wkn, wvt)


def _softmax_parts(s_list):
    m = functools.reduce(jnp.maximum, [jnp.max(s, axis=-1, keepdims=True) for s in s_list])
    p_list = [jnp.exp(s - m) for s in s_list]
    l = functools.reduce(lambda a, b: a + b, [jnp.sum(p, axis=-1, keepdims=True) for p in p_list])
    return p_list, l


def _softmax_parts_t(st_list):
    m = functools.reduce(jnp.maximum, [jnp.max(s, axis=0, keepdims=True) for s in st_list])
    p_list = [jnp.exp2(s - m) for s in st_list]
    l = functools.reduce(lambda a, b: a + b, [jnp.sum(p, axis=0, keepdims=True) for p in p_list])
    return p_list, l


def _query_rows(n_samples, n_tok, n_ctx, tq, ctx_queries):
    if ctx_queries:
        nq = n_ctx // tq
        first = n_samples * n_tok // tq
        return nq, (lambda b, t: first + b * nq + t), n_samples * n_ctx, (lambda b, t: b * nq + t)
    nq = n_tok // tq
    qrow = lambda b, t: b * nq + t
    return nq, qrow, n_samples * n_tok, qrow


def _mla_attn_kernel(q_ref, *rest, with_latent):
    if with_latent:
        kl_ref, vtl_ref, kc_ref, vtc_ref, o_ref = rest
    else:
        kc_ref, vtc_ref, o_ref = rest
    for hh in range(q_ref.shape[1] // MLA_SLOT):
        ds = slice(hh * MLA_SLOT, (hh + 1) * MLA_SLOT)
        dv = slice(hh * HEAD_DIM, (hh + 1) * HEAD_DIM)
        q = q_ref[:, ds]
        k_list = [kc_ref[:, ds]] + ([kl_ref[:, ds]] if with_latent else [])
        vt_list = [vtc_ref[dv, :]] + ([vtl_ref[dv, :]] if with_latent else [])
        p_list, l = _softmax_parts_t([_dot_nt(k, q) for k in k_list])
        ot = functools.reduce(lambda a, b: a + b, [_dot(vt, p.astype(BF16)) for vt, p in zip(vt_list, p_list)])
        o_ref[:, dv] = jnp.transpose(ot * (1.0 / l)).astype(BF16)


def mla_attention(qcat, kcat, vt, n_samples, n_tok, n_ctx, ctx_queries):
    tq = min(512, n_ctx) if ctx_queries else 512
    HP = 1
    nq, qrow, out_rows, orow = _query_rows(n_samples, n_tok, n_ctx, tq, ctx_queries)
    cblk = n_samples * n_tok // n_ctx
    WS, WV = HP * MLA_SLOT, HP * HEAD_DIM
    in_specs = [pl.BlockSpec((tq, WS), lambda b, hp, t: (qrow(b, t), hp))]
    args = [qcat]
    if not ctx_queries:
        in_specs += [pl.BlockSpec((n_tok, WS), lambda b, hp, t: (b, hp)),
                     pl.BlockSpec((WV, n_tok), lambda b, hp, t: (hp, b))]
        args += [kcat, vt]
    in_specs += [pl.BlockSpec((n_ctx, WS), lambda b, hp, t: (cblk + b, hp)),
                 pl.BlockSpec((WV, n_ctx), lambda b, hp, t: (hp, cblk + b))]
    args += [kcat, vt]
    return pl.pallas_call(
        functools.partial(_mla_attn_kernel, with_latent=not ctx_queries),
        grid=(n_samples, vt.shape[0] // WV, nq),
        in_specs=in_specs,
        out_specs=pl.BlockSpec((tq, WV), lambda b, hp, t: (orow(b, t), hp)),
        out_shape=jax.ShapeDtypeStruct((out_rows, vt.shape[0]), BF16),
        compiler_params=_cparams(3, 56), name="mla_attn_ctx" if ctx_queries else "mla_attn",
    )(*args)


def _diff_attn_kernel(q_ref, *rest, with_latent, out_scale):
    if with_latent:
        kl_ref, vtl_ref, kc_ref, vtc_ref, lam_ref, g_ref, o_ref = rest
    else:
        kc_ref, vtc_ref, lam_ref, g_ref, o_ref = rest
    lp = lam_ref[...]
    lam = (jnp.exp(jnp.sum(lp[0:1] * lp[1:2], axis=-1, keepdims=True))
           - jnp.exp(jnp.sum(lp[2:3] * lp[3:4], axis=-1, keepdims=True)) + lam_ref[4:5, 0:1])
    d = HEAD_DIM
    k_refs = [kc_ref] + ([kl_ref] if with_latent else [])
    vt_refs = [vtc_ref] + ([vtl_ref] if with_latent else [])
    parts = []
    for half in range(2):
        sl = slice(half * d, (half + 1) * d)
        q = q_ref[:, sl]
        p_list, l = _softmax_parts_t([_dot_nt(k[:, sl], q) for k in k_refs])
        parts.append((p_list, 1.0 / l))
    (p1_list, inv1), (p2_list, inv2) = parts
    c2 = lam * inv2
    ot = functools.reduce(lambda a, b: a + b,
                          [_dot(vt[...], (p1 * inv1 - p2 * c2).astype(BF16))
                           for p1, p2, vt in zip(p1_list, p2_list, vt_refs)])
    o = jnp.transpose(ot)
    y = o * lax.rsqrt(jnp.mean(o * o, axis=-1, keepdims=True) + RMS_EPS) * g_ref[...]
    o_ref[...] = (y * out_scale).astype(BF16)


def diff_attention(dq, dk, vt, lam_rows, subln_g, lambda_init, n_samples, n_tok, n_ctx, ctx_queries):
    tq = min(512, n_ctx) if ctx_queries else 512
    W = 2 * HEAD_DIM
    nq, qrow, out_rows, orow = _query_rows(n_samples, n_tok, n_ctx, tq, ctx_queries)
    cblk = n_samples * n_tok // n_ctx
    in_specs = [pl.BlockSpec((tq, W), lambda b, h, t: (qrow(b, t), h))]
    args = [dq]
    if not ctx_queries:
        in_specs += [pl.BlockSpec((n_tok, W), lambda b, h, t: (b, h)),
                     pl.BlockSpec((W, n_tok), lambda b, h, t: (h, b))]
        args += [dk, vt]
    in_specs += [pl.BlockSpec((n_ctx, W), lambda b, h, t: (cblk + b, h)),
                 pl.BlockSpec((W, n_ctx), lambda b, h, t: (h, cblk + b)),
                 pl.BlockSpec((8, HEAD_DIM), lambda b, h, t: (0, 0)),
                 pl.BlockSpec((1, W), lambda b, h, t: (0, 0))]
    args += [dk, vt, lam_rows, subln_g.reshape(1, W)]
    n_heads = dk.shape[1] // W
    return pl.pallas_call(
        functools.partial(_diff_attn_kernel, with_latent=not ctx_queries, out_scale=1.0 - lambda_init),
        grid=(n_samples, n_heads, nq),
        in_specs=in_specs,
        out_specs=pl.BlockSpec((tq, W), lambda b, h, t: (orow(b, t), h)),
        out_shape=jax.ShapeDtypeStruct((out_rows, dk.shape[1]), BF16),
        compiler_params=_cparams(3, 56), name="diff_attn_ctx" if ctx_queries else "diff_attn",
    )(*args)


NA_DY = 2 * NA_KH - 1


def _na_bias_tables(rpb):
    W = GRID_W
    H = rpb.shape[0]
    qc = np.arange(W)[:, None]
    kc = np.arange(W)[None, :]
    cs = np.clip(qc - NA_KW // 2, 0, W - NA_KW)
    col_ok = (kc >= cs) & (kc < cs + NA_KW)
    dx = np.clip(kc - qc + NA_KW - 1, 0, 2 * NA_KW - 2)
    by_dx = jnp.take(rpb.astype(F32), jnp.asarray(dx.reshape(W * W)), axis=2).reshape(H, NA_DY, W, W)
    by_dx = jnp.where(jnp.asarray(col_ok)[None, None], by_dx, NEG_BIG)
    neg = jnp.full((H, 1, W, W), NEG_BIG, F32)
    padded = jnp.concatenate([neg, by_dx, neg, neg], axis=1)
    return jnp.concatenate([padded[:, :-1], padded[:, 1:]], axis=-1)


def _na_attn_kernel(q_ref, kl_ref, vl_ref, kc_ref, vc_ref, bias_ref, o_ref, *, scale, n_rows):
    rb = pl.program_id(2)
    r0 = jnp.clip(rb * 8 - 4, 0, n_rows - 16)
    start = pl.multiple_of(r0 * GRID_W, 256)
    kw = kl_ref[pl.ds(start, 16 * GRID_W), :]
    vw = vl_ref[pl.ds(start, 16 * GRID_W), :]
    q = q_ref[...]
    first, last_blk = rb == 0, rb == n_rows // 8 - 1
    dy0 = jnp.where(first, NA_KH - 1, jnp.where(last_blk, -1, NA_KH // 2 - 1))
    lane = lax.broadcasted_iota(jnp.int32, (GRID_W, 2 * GRID_W), 1)
    bias_rows = []
    for qa in range(8):
        win0 = jnp.where(first, max(qa - 4, 0), jnp.where(last_blk, min(qa + 4, 8), qa))
        tiles = []
        for kp in range(8):
            a1 = jnp.clip(2 * kp - qa + dy0 + 1, 0, NA_DY + 1)
            ok_even = jnp.logical_and(2 * kp >= win0, 2 * kp < win0 + NA_KH).astype(jnp.int32)
            ok_odd = jnp.logical_and(2 * kp + 1 >= win0, 2 * kp + 1 < win0 + NA_KH).astype(jnp.int32)
            ok = jnp.where(lane < GRID_W, ok_even, ok_odd)
            tiles.append(jnp.where(ok > 0, bias_ref[a1], NEG_BIG))
        bias_rows.append(jnp.concatenate(tiles, axis=1))
    s_lat = _dot_nt(q, kw) * scale + jnp.concatenate(bias_rows, axis=0)
    s_ctx = _dot_nt(q, kc_ref[...]) * scale
    (p_lat, p_ctx), l = _softmax_parts([s_lat, s_ctx])
    inv = 1.0 / l
    o = _dot((p_lat * inv).astype(BF16), vw) + _dot((p_ctx * inv).astype(BF16), vc_ref[...])
    o_ref[...] = o.astype(BF16)


def na_attention(z1, bias, n_samples, n_tok, n_ctx, n_heads):
    n_rows = n_tok // GRID_W
    assert n_rows >= 16 and n_rows % 8 == 0
    nrb = n_rows // 8
    tq = 8 * GRID_W
    d = HEAD_DIM
    cblk = n_samples * n_tok // n_ctx
    return pl.pallas_call(
        functools.partial(_na_attn_kernel, scale=d ** -0.5, n_rows=n_rows),
        grid=(n_samples, n_heads, nrb),
        in_specs=[pl.BlockSpec((tq, d), lambda b, h, rb: (b * nrb + rb, h)),
                  pl.BlockSpec((n_tok, d), lambda b, h, rb: (b, n_heads + h)),
                  pl.BlockSpec((n_tok, d), lambda b, h, rb: (b, 2 * n_heads + h)),
                  pl.BlockSpec((n_ctx, d), lambda b, h, rb: (cblk + b, n_heads + h)),
                  pl.BlockSpec((n_ctx, d), lambda b, h, rb: (cblk + b, 2 * n_heads + h)),
                  pl.BlockSpec((None, NA_DY + 2, GRID_W, 2 * GRID_W), lambda b, h, rb: (h, 0, 0, 0))],
        out_specs=pl.BlockSpec((tq, d), lambda b, h, rb: (b * nrb + rb, h)),
        out_shape=jax.ShapeDtypeStruct((n_samples * n_tok, n_heads * d), BF16),
        compiler_params=_cparams(3, 40), name="na_attn",
    )(z1, z1, z1, z1, z1, bias)


def _ctx_attn_kernel(q_ref, k_ref, v_ref, o_ref, *, scale):
    (p,), l = _softmax_parts([_dot_nt(q_ref[...], k_ref[...]) * scale])
    o_ref[...] = _dot((p * (1.0 / l)).astype(BF16), v_ref[...]).astype(BF16)


def na_ctx_attention(z1, n_samples, n_tok, n_ctx, n_heads):
    d = HEAD_DIM
    cblk = n_samples * n_tok // n_ctx
    return pl.pallas_call(
        functools.partial(_ctx_attn_kernel, scale=d ** -0.5),
        grid=(n_samples, n_heads),
        in_specs=[pl.BlockSpec((n_ctx, d), lambda b, h: (cblk + b, h)),
                  pl.BlockSpec((n_ctx, d), lambda b, h: (cblk + b, n_heads + h)),
                  pl.BlockSpec((n_ctx, d), lambda b, h: (cblk + b, 2 * n_heads + h))],
        out_specs=pl.BlockSpec((n_ctx, d), lambda b, h: (b, h)),
        out_shape=jax.ShapeDtypeStruct((n_samples * n_ctx, n_heads * d), BF16),
        compiler_params=_cparams(2, 16), name="na_attn_ctx",
    )(z1, z1, z1)


def _hgrn_chunk(q_raw, f_raw, v, lb, st_ref, reverse):
    C, HW = q_raw.shape
    dk = HEAD_DIM
    n_heads = HW // dk
    q = _silu(q_raw) * (dk ** -0.5)
    f = lb + (1.0 - lb) * jax.nn.sigmoid(f_raw)
    g = jnp.maximum(jnp.log(f) * LOG2E, -300.0)
    k = 1.0 - f
    t_idx = lax.broadcasted_iota(jnp.int32, (C, 1), 0)
    p = (C - 1 - t_idx) if reverse else t_idx

    def rotate_rows(x, s):
        if s % 8 == 0:
            return jnp.concatenate([x[C - s:], x[:C - s]], axis=0)
        return pltpu.roll(x, s, 0)

    def prev(x, s):
        return rotate_rows(x, (C - s) if reverse else s)

    def nxt(x, s):
        return rotate_rows(x, s if reverse else (C - s))

    def incl_scan(b):
        x = g
        pb = jnp.bitwise_and(p, b - 1)
        s = 1
        while s < b:
            x = x + jnp.where(pb >= s, prev(x, s), 0.0)
            s *= 2
        return x

    def excl_rscan(b):
        if b == 1:
            return jnp.zeros_like(g)
        pb = jnp.bitwise_and(p, b - 1)
        x = jnp.where(pb <= b - 2, nxt(g, 1), 0.0)
        s = 1
        while s < b:
            x = x + jnp.where(pb + s <= b - 1, nxt(x, s), 0.0)
            s *= 2
        return x

    cum = incl_scan(C)
    last = cum[0:1] if reverse else cum[C - 1:C]
    q_in = (q * jnp.exp2(cum)).astype(BF16)
    k_out = (k * jnp.exp2(last - cum)).astype(BF16)
    total = jnp.exp2(last)
    qb, kb, vb = q.astype(BF16), k.astype(BF16), v.astype(BF16)

    def boundary_rows(b):
        rows = []
        for j in range(C // (2 * b)):
            r = 2 * b * j + (b if reverse else b - 1)
            rows.append(jnp.broadcast_to(cum[r:r + 1], (2 * b, HW)))
        return jnp.concatenate(rows, axis=0)

    levels = []
    b = C // 2
    while b >= 1:
        upper = jnp.bitwise_and(p, b) != 0
        if b >= 8:
            x = cum - boundary_rows(b)
            e_up, e_lo = x, -x
        else:
            e_up, e_lo = incl_scan(b), excl_rscan(b)
        ql = jnp.where(upper, q * jnp.exp2(e_up), 0.0).astype(BF16)
        kl = jnp.where(upper, 0.0, k * jnp.exp2(e_lo)).astype(BF16)
        levels.append((b, ql, kl))
        b //= 2

    s_idx = lax.broadcasted_iota(jnp.int32, (1, C), 1)
    ps = (C - 1 - s_idx) if reverse else s_idx
    diag_w = (p == ps).astype(F32)
    pair_w = [(jnp.bitwise_and(p, -2 * b) == jnp.bitwise_and(ps, -2 * b)).astype(F32) for b, _, _ in levels]
    outs = []
    for h in range(n_heads):
        sl = slice(h * dk, (h + 1) * dk)
        st_old = st_ref[h]
        att = diag_w * _dot_nt(qb[:, sl], kb[:, sl])
        for (b, ql, kl), w in zip(levels, pair_w):
            att = att + w * _dot_nt(ql[:, sl], kl[:, sl])
        outs.append(_dot_nt(q_in[:, sl], st_old.astype(BF16)) + _dot(att.astype(BF16), vb[:, sl]))
        st_ref[h] = total[:, sl] * st_old + _dot_tn(vb[:, sl], k_out[:, sl])
    return jnp.concatenate(outs, axis=1)


def _hgrn_kernel(*refs, n_samples):
    ns = n_samples
    fwd_in, bwd_in = refs[:3 * ns], refs[3 * ns:6 * ns]
    lb_ref, of_ref, ob_ref, sf_ref, sb_ref = refs[6 * ns:]

    @pl.when(pl.program_id(0) == 0)
    def _():
        sf_ref[...] = jnp.zeros_like(sf_ref)
        sb_ref[...] = jnp.zeros_like(sb_ref)

    lb = jnp.concatenate([lb_ref[...]] * ns, axis=1)
    HW = lb_ref.shape[1]
    for ins, o_ref, st_ref, rev in ((fwd_in, of_ref, sf_ref, False), (bwd_in, ob_ref, sb_ref, True)):
        q, f, v = (jnp.concatenate([ins[3 * s + k][...] for s in range(ns)], axis=1) for k in range(3))
        o = _hgrn_chunk(q, f, v, lb, st_ref, reverse=rev)
        for s in range(ns):
            o_ref[s] = o[:, s * HW:(s + 1) * HW]


def hgrn_scan(z2, lb, n_samples, n_tok, n_ctx):
    HW = z2.shape[1] // 5
    C = HGRN_CHUNK
    ncc, ncl = n_ctx // C, n_tok // C
    lat_blocks = n_samples * ncl

    def fwd_row(b, c):
        return jnp.where(c < ncc, lat_blocks + b * ncc + c, b * ncl + (c - ncc))

    def bwd_row(b, c):
        return jnp.where(c < ncc, lat_blocks + b * ncc + (ncc - 1 - c), b * ncl + (ncl - 1 - (c - ncc)))

    spec = lambda rowf, b, col: pl.BlockSpec((C, HW), lambda c: (rowf(b, c), col))
    samples = range(n_samples)
    in_specs = ([spec(fwd_row, b, col) for b in samples for col in (0, 1, 3)]
                + [spec(bwd_row, b, col) for b in samples for col in (0, 2, 3)]
                + [pl.BlockSpec((1, HW), lambda c: (0, 0))])
    n_heads_all = n_samples * HW // HEAD_DIM
    bwd_local = lambda c: jnp.where(c < ncc, ncc - 1 - c, ncc + ncl - 1 - (c - ncc))
    return pl.pallas_call(
        functools.partial(_hgrn_kernel, n_samples=n_samples), grid=(ncc + ncl,),
        in_specs=in_specs,
        out_specs=[pl.BlockSpec((n_samples, C, HW), lambda c: (0, c, 0)),
                   pl.BlockSpec((n_samples, C, HW), lambda c: (0, bwd_local(c), 0))],
        out_shape=[jax.ShapeDtypeStruct((n_samples, n_ctx + n_tok, HW), F32)] * 2,
        scratch_shapes=[pltpu.VMEM((n_heads_all, HEAD_DIM, HEAD_DIM), F32)] * 2,
        compiler_params=_cparams(1, 56), name="hgrn_scan",
    )(*([z2] * (6 * n_samples)), lb.reshape(1, HW))


def _hgrn_readout_kernel(of_ref, ob_ref, gz_ref, g_ref, o_ref):
    for h in range(of_ref.shape[1] // HEAD_DIM):
        sl = slice(h * HEAD_DIM, (h + 1) * HEAD_DIM)
        x = of_ref[:, sl] + ob_ref[:, sl]
        y = x * lax.rsqrt(jnp.mean(x * x, axis=-1, keepdims=True) + RMS_EPS) * g_ref[...]
        o_ref[:, sl] = (y * _silu(gz_ref[:, sl])).astype(BF16)


def hgrn_readout(o_f, o_b, z2, norm_g, n_rows, n_tok, n_ctx):
    n_samples, _, HW = o_f.shape
    tm = math.gcd(n_ctx, ROW_TILE)
    lat_tiles, tiles_per_sample, ctx_tiles = n_samples * n_tok // tm, n_tok // tm, n_ctx // tm

    def o_idx(i):
        is_lat = i < lat_tiles
        jc = i - lat_tiles
        return (jnp.where(is_lat, i // tiles_per_sample, jc // ctx_tiles),
                jnp.where(is_lat, ctx_tiles + i % tiles_per_sample, jc % ctx_tiles), 0)

    return pl.pallas_call(
        _hgrn_readout_kernel, grid=(n_rows // tm,),
        in_specs=[pl.BlockSpec((None, tm, HW), o_idx), pl.BlockSpec((None, tm, HW), o_idx),
                  pl.BlockSpec((tm, HW), lambda i: (i, 4)), pl.BlockSpec((1, HEAD_DIM), lambda i: (0, 0))],
        out_specs=pl.BlockSpec((tm, HW), lambda i: (i, 0)),
        out_shape=jax.ShapeDtypeStruct((n_rows, HW), BF16),
        compiler_params=_cparams(1, 32), name="hgrn_readout",
    )(o_f, o_b, z2, norm_g.reshape(1, HEAD_DIM))


def _merge_kernel(h_ref, oa_ref, od_ref, om_ref, or_ref, wg_ref, bg_ref, wb_ref, o_ref, wg_bf, wb_bf):
    @pl.when(pl.program_id(1) == 0)
    def _():
        wg_bf[...] = wg_ref[...].astype(BF16)
        wb_bf[...] = wb_ref[...].astype(BF16)

    h = h_ref[...]
    acc = None
    for j, o_j in enumerate((oa_ref, od_ref, om_ref, or_ref)):
        gate = jax.nn.sigmoid(_dot(h, wg_bf[j]) + bg_ref[j])
        term = gate * _dot(o_j[...], wb_bf[j])
        acc = term if acc is None else acc + term
    o_ref[...] = acc.astype(BF16)


def merge_branches(h, outs, w_gate, b_gate, w_branch, layer, n_rows):
    D = h.shape[1]
    L, nb, BW = w_branch.shape[:3]
    tm, tn = ROW_TILE, 256
    single = dict(pipeline_mode=pl.Buffered(1))
    return pl.pallas_call(
        _merge_kernel, grid=(D // tn, n_rows // tm),
        in_specs=[pl.BlockSpec((tm, D), lambda j, i: (i, 0))]
                 + [pl.BlockSpec((tm, BW), lambda j, i: (i, 0))] * nb
                 + [pl.BlockSpec((None, nb, D, tn), lambda j, i: (layer, 0, 0, j), **single),
                    pl.BlockSpec((None, nb, 1, tn), lambda j, i: (layer, 0, 0, j)),
                    pl.BlockSpec((None, nb, BW, tn), lambda j, i: (layer, 0, 0, j), **single)],
        out_specs=pl.BlockSpec((tm, tn), lambda j, i: (i, j)),
        out_shape=jax.ShapeDtypeStruct((n_rows, D), BF16),
        scratch_shapes=[pltpu.VMEM((nb, D, tn), BF16), pltpu.VMEM((nb, BW, tn), BF16)],
        compiler_params=_cparams(2, 58), name="merge",
    )(h, *outs, w_gate, b_gate.reshape(L, nb, 1, D), w_branch)


def _ffn_in_kernel(x_ref, w1_ref, w3_ref, o_ref, w1_bf, w3_bf):
    @pl.when(pl.program_id(1) == 0)
    def _():
        w1_bf[...] = w1_ref[...].astype(BF16)
        w3_bf[...] = w3_ref[...].astype(BF16)

    x = x_ref[...]
    o_ref[...] = (_silu(_dot(x, w1_bf[...])) * _dot(x, w3_bf[...])).astype(BF16)


def ffn_in(h, w1, w3, layer, n_rows):
    _, D, Fd = w1.shape
    tm, tn = _even_row_tile(n_rows), 256
    w_spec = pl.BlockSpec((None, D, tn), lambda j, i: (layer, 0, j))
    return pl.pallas_call(
        _ffn_in_kernel,
        grid=(Fd // tn, pl.cdiv(n_rows, tm)),
        in_specs=[pl.BlockSpec((tm, D), lambda j, i: (i, 0)), w_spec, w_spec],
        out_specs=pl.BlockSpec((tm, tn), lambda j, i: (i, j)),
        out_shape=jax.ShapeDtypeStruct((n_rows, Fd), BF16),
        scratch_shapes=[pltpu.VMEM((D, tn), BF16)] * 2,
        compiler_params=_cparams(2, 50), name="ffn_in",
    )(h, w1, w3)


def _group_spec(tm, tn, n_lat, n_samples):
    return pl.BlockSpec((None, 1, tn), lambda j, i: (_grp_of_tile(i, tm, n_lat, n_samples), 0, j))


def ffn_out(u, w2, layer, xres, gate, n_rows, n_lat, n_samples):
    D = w2.shape[2]
    G = gate.shape[0]
    tm, tn = 256, 512
    return _mm_call(u, w2, layer, n_rows=n_rows, tn=tn, tm=tm, out_dtype=F32,
                    epilogue=_residual_epilogue, extras=(xres, gate.reshape(G, 1, D)),
                    extra_specs=(pl.BlockSpec((tm, tn), lambda j, i: (i, j)), _group_spec(tm, tn, n_lat, n_samples)),
                    name="ffn_out")


MOE_TILE = 256


def moe_route_plan(comb, n_experts):
    M = comb.shape[0]
    T = MOE_TILE
    w = comb[:, :n_experts]
    sel = w > 0
    n_tiles = 2 * M // T + n_experts
    R = n_tiles * T
    cnt = jnp.sum(sel, axis=0, dtype=jnp.int32)
    rank = jnp.cumsum(sel, axis=0, dtype=jnp.int32) - 1
    gsz = (cnt + T - 1) // T * T
    gend = jnp.cumsum(gsz)
    off = gend - gsz
    dest_all = off[None, :] + rank
    tile_start = jnp.arange(n_tiles, dtype=jnp.int32) * T
    tile_expert = jnp.minimum(jnp.searchsorted(gend, tile_start, side="right"), n_experts - 1).astype(jnp.int32)
    n_used = (gend[-1] // T).reshape(1).astype(jnp.int32)
    k_next = jnp.searchsorted(tile_expert, tile_expert, side="right")
    next_expert = jnp.where(k_next < n_tiles, tile_expert[jnp.minimum(k_next, n_tiles - 1)], -1).astype(jnp.int32)
    e_lo = jnp.argmax(sel, axis=1)
    e_hi = n_experts - 1 - jnp.argmax(sel[:, ::-1], axis=1)
    take = lambda a, e: jnp.take_along_axis(a, e[:, None], axis=1)[:, 0]
    two = e_hi != e_lo
    dest = jnp.stack([take(dest_all, e_lo), take(dest_all, e_hi)], axis=1).astype(jnp.int32)
    tok = lax.broadcasted_iota(jnp.int32, dest.shape, 0)
    src_tok = jnp.zeros((R,), jnp.int32).at[dest.reshape(-1)].set(tok.reshape(-1))
    pw = jnp.zeros((M, 128), F32).at[:, 0].set(take(w, e_lo)).at[:, 1].set(jnp.where(two, take(w, e_hi), 0.0))
    return src_tok, (tile_expert, next_expert, n_used), dest, pw


def _row_copy(src_hbm, row, dst_buf, slot, r, sem):
    return pltpu.make_async_copy(src_hbm.at[pl.ds(row, 1)], dst_buf.at[slot, pl.ds(r, 1)], sem.at[slot])


def _gather_tile(idx_of_row, src_hbm, buf, sem, n_rows_tile):
    i = pl.program_id(0)
    n = pl.num_programs(0)

    def start_tile(t, slot):
        def body(r2, carry):
            for pri in range(2):
                r = 2 * r2 + pri
                _row_copy(src_hbm, idx_of_row(t * n_rows_tile + r), buf, slot, r, sem).start(priority=pri)
            return carry
        lax.fori_loop(0, n_rows_tile // 2, body, 0, unroll=4)

    @pl.when(i == 0)
    def _():
        start_tile(0, 0)

    @pl.when(i + 1 < n)
    def _():
        start_tile(i + 1, (i + 1) % 2)

    slot = i % 2

    def wait_body(r, carry):
        _row_copy(src_hbm, 0, buf, slot, r, sem).wait()
        return carry
    lax.fori_loop(0, n_rows_tile, wait_body, 0, unroll=8)
    return slot


def _moe_gather_kernel(src_ref, h_hbm, o_ref, buf, sem):
    slot = _gather_tile(lambda r: src_ref[r], h_hbm, buf, sem, o_ref.shape[0])
    o_ref[...] = buf[slot].astype(BF16)


def moe_gather(hf, src_tok):
    R = src_tok.shape[0]
    D = hf.shape[1]
    T = MOE_TILE
    return pl.pallas_call(
        _moe_gather_kernel,
        grid_spec=pltpu.PrefetchScalarGridSpec(
            num_scalar_prefetch=1, grid=(R // T,),
            in_specs=[pl.BlockSpec(memory_space=pl.ANY)],
            out_specs=pl.BlockSpec((T, D), lambda i, src: (i, 0)),
            scratch_shapes=[pltpu.VMEM((2, T, D), F32), pltpu.SemaphoreType.DMA((2,))]),
        out_shape=jax.ShapeDtypeStruct((R, D), BF16),
        compiler_params=_cparams(1, 24), name="moe_gather",
    )(src_tok, hf)


def _moe_mm_kernel(te_ref, nx_ref, nu_ref, x_ref, *rest, e0, n_w):
    w_hbms = rest[:n_w]
    o_ref = rest[n_w]
    stages, wbfs, sem = rest[n_w + 1:2 * n_w + 1], rest[2 * n_w + 1:3 * n_w + 1], rest[3 * n_w + 1]
    j, i, nj = pl.program_id(0), pl.program_id(1), pl.num_programs(0)
    tn = stages[0].shape[1]

    def block_copies(e, jj):
        cols = pl.ds(pl.multiple_of(jj * tn, tn), tn)
        return [pltpu.make_async_copy(w.at[e0 + e, :, cols], st, sem.at[k])
                for k, (w, st) in enumerate(zip(w_hbms, stages))]

    fresh = jnp.logical_or(i == 0, te_ref[i] != te_ref[jnp.maximum(i - 1, 0)])

    @pl.when(fresh)
    def _():
        @pl.when(jnp.logical_and(i == 0, j == 0))
        def _():
            for cp in block_copies(te_ref[0], 0):
                cp.start()

        for cp in block_copies(te_ref[i], j):
            cp.wait()
        for st, wbf in zip(stages, wbfs):
            _cast_rows(st, wbf)
        nxt = nx_ref[i]

        @pl.when(nxt >= 0)
        def _():
            for cp in block_copies(nxt, j):
                cp.start()

        @pl.when(jnp.logical_and(nxt < 0, j + 1 < nj))
        def _():
            for cp in block_copies(te_ref[0], j + 1):
                cp.start()

    @pl.when(i < nu_ref[0])
    def _():
        x = x_ref[...]
        if n_w == 2:
            o_ref[...] = (_silu(_dot(x, wbfs[0][...])) * _dot(x, wbfs[1][...])).astype(o_ref.dtype)
        else:
            o_ref[...] = _dot(x, wbfs[0][...]).astype(o_ref.dtype)

    @pl.when(i >= nu_ref[0])
    def _():
        o_ref[...] = jnp.zeros_like(o_ref)


def _moe_mm(x, ws, e0, plan, tn, out_dtype, name):
    tile_expert, next_expert, n_used = plan
    R, K = x.shape
    N = ws[0].shape[2]
    T = MOE_TILE
    n_w = len(ws)
    return pl.pallas_call(
        functools.partial(_moe_mm_kernel, e0=e0, n_w=n_w),
        grid_spec=pltpu.PrefetchScalarGridSpec(
            num_scalar_prefetch=3, grid=(N // tn, R // T),
            in_specs=[pl.BlockSpec((T, K), lambda j, i, te, nx, nu: (i, 0))]
                     + [pl.BlockSpec(memory_space=pl.ANY)] * n_w,
            out_specs=pl.BlockSpec((T, tn), lambda j, i, te, nx, nu: (i, j)),
            scratch_shapes=[pltpu.VMEM((K, tn), F32)] * n_w + [pltpu.VMEM((K, tn), BF16)] * n_w
                           + [pltpu.SemaphoreType.DMA((n_w,))]),
        out_shape=jax.ShapeDtypeStruct((R, N), out_dtype),
        compiler_params=_cparams(2, 48), name=name,
    )(tile_expert, next_expert, n_used, x, *ws)


def moe_ffn_in(xs, w1, w3, e0, plan):
    return _moe_mm(xs, (w1, w3), e0, plan, 512, BF16, "moe_ffn_in")


def moe_ffn_out(u, w2, e0, plan):
    return _moe_mm(u, (w2,), e0, plan, 2048, F32, "moe_ffn_out")


def _moe_combine_kernel(dest_ref, y_hbm, pw_ref, x_ref, g_ref, *rest, with_norm):
    if with_norm:
        ng_ref, o_ref, buf, sem = rest
    else:
        o_ref, buf, sem = rest
    T = o_ref.shape[0]
    slot = _gather_tile(lambda r: dest_ref[r], y_hbm, buf, sem, 2 * T)
    pw = pw_ref[...]
    mix = pw[:, 0:1] * buf[slot, 0:T, :] + pw[:, 1:2] * buf[slot, T:2 * T, :]
    x = x_ref[...] + g_ref[...] * mix
    if with_norm:
        x = x * lax.rsqrt(jnp.mean(x * x, axis=-1, keepdims=True) + RMS_EPS) * ng_ref[...]
    o_ref[...] = x


def moe_combine(y, dest, pw, xres, gate, n_lat, n_samples, norm_g=None):
    M, D = xres.shape
    G = gate.shape[0]
    T = 128
    dest = dest.reshape(M // T, T, 2).transpose(0, 2, 1)
    in_specs = [pl.BlockSpec(memory_space=pl.ANY),
                pl.BlockSpec((T, 128), lambda i, d: (i, 0)),
                pl.BlockSpec((T, D), lambda i, d: (i, 0)),
                pl.BlockSpec((None, 1, D), lambda i, d: (_grp_of_tile(i, T, n_lat, n_samples), 0, 0))]
    args = [dest.reshape(-1), y, pw, xres, gate.reshape(G, 1, D)]
    if norm_g is not None:
        in_specs.append(pl.BlockSpec((1, D), lambda i, d: (0, 0)))
        args.append(norm_g.reshape(1, D))
    return pl.pallas_call(
        functools.partial(_moe_combine_kernel, with_norm=norm_g is not None),
        grid_spec=pltpu.PrefetchScalarGridSpec(
            num_scalar_prefetch=1, grid=(M // T,),
            in_specs=in_specs,
            out_specs=pl.BlockSpec((T, D), lambda i, d: (i, 0)),
            scratch_shapes=[pltpu.VMEM((2, 2 * T, D), F32), pltpu.SemaphoreType.DMA((2,))]),
        out_shape=jax.ShapeDtypeStruct((M, D), F32),
        compiler_params=_cparams(1, 32), name="moe_combine",
    )(*args)


def kernel(x, c, ctx, c_ctx, norm1_g, norm2_g, w_ada, b_ada, w_in, na_rpb, diff_lambda, diff_subln_g,
           mla_q_norm_g, mla_w_q_up, mla_kv_norm_g, mla_w_kv_up, hgrn_lower_bounds, hgrn_norm_g,
           w_branch, w_gate, b_gate, w_out, ffn_w1, ffn_w3, ffn_w2, moe_router, moe_w1, moe_w3, moe_w2,
           final_norm_g):
    B, N, D = x.shape
    NC = ctx.shape[1]
    L = w_ada.shape[0]
    n_lat, n_all = B * N, B * N + B * NC
    BW = w_branch.shape[2]
    n_heads = BW // HEAD_DIM
    HG0 = w_in.shape[2] - 5 * BW
    assert N % ROW_TILE == 0 and (B * NC) % ROW_TILE == 0 and NC % HGRN_CHUNK == 0

    x_lat, x_ctx = x.reshape(n_lat, D), ctx.reshape(B * NC, D)
    xs = None

    cond8 = jnp.zeros((8, D), F32).at[:B].set(c).at[B].set(c_ctx)
    mods = adaln_all(cond8, w_ada, b_ada)[:, :B + 1].reshape(L, B + 1, 6, D)

    lb_all = jnp.cumsum(jax.nn.softmax(hgrn_lower_bounds.astype(F32), axis=0), axis=0)
    lb_all = lb_all - lb_all[0:1]

    w_in_t = jnp.swapaxes(w_in, 1, 2)

    cosd, sind = _rope_tables(N, B * NC, 2 * 64, ROW_TILE)
    cosm, sinm = _rope_tables(N, B * NC, 64, ROW_TILE)

    RQ, RKV = mla_w_q_up.shape[1], mla_w_kv_up.shape[1]
    wq4 = mla_w_q_up.reshape(L, RQ, n_heads, HEAD_DIM + MLA_ROPE)
    wq_cat = jnp.pad(wq4, ((0, 0), (0, 0), (0, 0), (0, MLA_SLOT - HEAD_DIM - MLA_ROPE))).reshape(L, RQ, -1)
    wkv4 = mla_w_kv_up.reshape(L, RKV, n_heads, 2 * HEAD_DIM)
    wkn = wkv4[..., :HEAD_DIM].reshape(L, RKV, -1)
    wvt = jnp.swapaxes(wkv4[..., HEAD_DIM:].reshape(L, RKV, -1), 1, 2)

    for l in range(L):
        need_ctx = l < L - 1
        n_rows = n_all if need_ctx else n_lat
        lambda_init = 0.8 - 0.6 * math.exp(-0.3 * l)
        sh1, sc1, g1, sh2, sc2, g2 = (mods[l, :, k] for k in range(6))

        if l == 0:
            h = norm_mod(x_lat, norm1_g[l], sh1, sc1, N, B, x_ctx=x_ctx)
        else:
            h = norm_mod(xs, norm1_g[l], sh1, sc1, N, B)
        z1 = wt_stream_matmul(h, w_in_t, l, 0, -(-HG0 // 512) * 512, BF16, "w_in_attn")
        z2 = wt_stream_matmul(h, w_in_t, l, HG0, 5 * BW, F32, "w_in_hgrn")

        bias = _na_bias_tables(na_rpb[l])
        o_a = na_attention(z1, bias, B, N, NC, n_heads)
        dq, dk, dvt = diff_rope(z1, cosd, sind, N, n_lat, BW)
        lam_rows = jnp.zeros((8, HEAD_DIM), F32).at[:4].set(diff_lambda[l]).at[4].set(lambda_init)
        o_d = diff_attention(dq, dk, dvt, lam_rows, diff_subln_g[l], lambda_init, B, N, NC, False)
        qcat, kcat, mvt = mla_prep(z1, cosm, sinm, mla_q_norm_g[l], wq_cat[l], mla_kv_norm_g[l], wkn[l], wvt[l],
                                   N, n_lat, 6 * BW)
        o_m = mla_attention(qcat, kcat, mvt, B, N, NC, False)
        o_f, o_b = hgrn_scan(z2, lb_all[l], B, N, NC)
        o_r = hgrn_readout(o_f, o_b, z2, hgrn_norm_g[l], n_rows, N, NC)
        if need_ctx:
            o_a = jnp.concatenate([o_a, na_ctx_attention(z1, B, N, NC, n_heads)], axis=0)
            o_d = jnp.concatenate([o_d, diff_attention(dq, dk, dvt, lam_rows, diff_subln_g[l], lambda_init,
                                                       B, N, NC, True)], axis=0)
            o_m = jnp.concatenate([o_m, mla_attention(qcat, kcat, mvt, B, N, NC, True)], axis=0)

        s = merge_branches(h, (o_a, o_d, o_m, o_r), w_gate, b_gate, w_branch, l, n_rows)
        G = B + 1
        if l == 0:
            tmo, nl = ROW_TILE, n_lat // ROW_TILE
            xs = _mm_call(s, w_out, l, n_rows=n_rows, tn=512, tm=tmo, out_dtype=F32,
                          epilogue=_two_source_residual(nl), extras=(x_lat, x_ctx, g1.reshape(G, 1, D)),
                          extra_specs=(pl.BlockSpec((tmo, 512), lambda j, i: (jnp.minimum(i, nl - 1), j)),
                                       pl.BlockSpec((tmo, 512), lambda j, i: (jnp.maximum(i - nl, 0), j)),
                                       _group_spec(tmo, 512, N, B)),
                          name="w_out")
        else:
            tmo = 2 * ROW_TILE
            xs = _mm_call(s, w_out, l, n_rows=n_rows, tn=512, tm=tmo, out_dtype=F32,
                          epilogue=_residual_epilogue, extras=(xs, g1.reshape(G, 1, D)),
                          extra_specs=(pl.BlockSpec((tmo, 512), lambda j, i: (i, j)), _group_spec(tmo, 512, N, B)),
                          name="w_out")

        j = l // 2
        if l % 2 == 0:
            h2 = norm_mod(xs, norm2_g[l], sh2, sc2, N, B)
            u = ffn_in(h2, ffn_w1, ffn_w3, j, n_rows)
            xs = ffn_out(u, ffn_w2, j, xs, g2, n_rows, N, B)
        else:
            hf, comb = norm_mod(xs, norm2_g[l], sh2, sc2, N, B, w_router=moe_router[j])
            NM, E, _, FE = moe_w1.shape
            src_tok, plan, dest, pw = moe_route_plan(comb, E)
            xg = moe_gather(hf, src_tok)
            u = moe_ffn_in(xg, moe_w1.reshape(NM * E, D, FE), moe_w3.reshape(NM * E, D, FE), j * E, plan)
            y = moe_ffn_out(u, moe_w2.reshape(NM * E, FE, D), j * E, plan)
            xs = moe_combine(y, dest, pw, xs, g2, N, B, norm_g=final_norm_g if l == L - 1 else None)

    if L % 2 == 1:
        xs = final_norm(xs[:n_lat], final_norm_g)
    return xs.reshape(B, N, D)
```
